```python
import math
import jax, jax.numpy as jnp
from jax import lax
import numpy as np

D_MODEL = 4096
BATCH = 2
SEQ = 4096
DEPTH = 2

HEAD_DIM = 128
MOBA_HEADS = D_MODEL // (4 * HEAD_DIM)
SB_HEADS = D_MODEL // (4 * HEAD_DIM)
SWA_HEADS = D_MODEL // (2 * HEAD_DIM)
SWA_KV_HEADS = max(1, SWA_HEADS // 8)
MOBA_WIDTH = MOBA_HEADS * HEAD_DIM
SB_WIDTH = SB_HEADS * HEAD_DIM
SWA_Q_WIDTH = SWA_HEADS * HEAD_DIM
SWA_KV_WIDTH = SWA_KV_HEADS * HEAD_DIM
MIX_WIDTH = MOBA_WIDTH + SB_WIDTH + SWA_Q_WIDTH
IN_WIDTH = 3 * MOBA_WIDTH + 3 * SB_WIDTH + SWA_Q_WIDTH + 2 * SWA_KV_WIDTH
MOBA_BLOCK = 256
MOBA_TOPK = 3
MOBA_Q_CHUNK = 32
SB_Q_BLOCK = 128
SWA_WINDOW = 128
REL_BUCKETS = 32
REL_MAX_EXACT = 16
REL_MAX_DISTANCE = 128
N_BIAS_HEADS = MOBA_HEADS + SWA_HEADS
D_FF = 7 * D_MODEL // 2
N_EXPERTS = 8
TOP_K = 2
D_EXPERT = D_MODEL
N_DENSE = (DEPTH + 1) // 2
N_MOE = DEPTH // 2
NORM_EPS = 1e-6
ADA_CHUNKS = 6

kernel_name = 'hybrid_moba_stickbreak_swa_block'


def rms_norm(x, g):
    xf = x.astype(jnp.float32)
    y = xf * lax.rsqrt(jnp.mean(xf * xf, axis=-1, keepdims=True) + NORM_EPS)
    return (y * g.astype(jnp.float32)).astype(x.dtype)


def head_rms_norm(o, g):
    nh, dh = o.shape[1], o.shape[3]
    y = o * lax.rsqrt(jnp.mean(o * o, axis=-1, keepdims=True) + NORM_EPS)
    return y * g.astype(jnp.float32).reshape(1, nh, 1, dh)


def split_heads(t, nh):
    b, s, _ = t.shape
    return t.reshape(b, s, nh, HEAD_DIM).transpose(0, 2, 1, 3)


def merge_heads(o):
    b, nh, s, dh = o.shape
    return o.transpose(0, 2, 1, 3).reshape(b, s, nh * dh)


def rel_bucket(dist):
    n = jnp.maximum(dist, 0)
    nf = jnp.maximum(n, 1).astype(jnp.float32)
    large = REL_MAX_EXACT + (jnp.log(nf / REL_MAX_EXACT) / math.log(REL_MAX_DISTANCE / REL_MAX_EXACT) * (REL_BUCKETS - REL_MAX_EXACT)).astype(jnp.int32)
    return jnp.where(n < REL_MAX_EXACT, n, jnp.minimum(large, REL_BUCKETS - 1))


def moba_attention(q, k, v, bias_table):
    b, nh, s, dh = q.shape
    nb = -(-s // MOBA_BLOCK)
    sp = nb * MOBA_BLOCK
    pad = ((0, 0), (0, 0), (0, sp - s), (0, 0))
    qf = jnp.pad(q.astype(jnp.float32), pad)
    kf = jnp.pad(k.astype(jnp.float32), pad)
    vf = jnp.pad(v.astype(jnp.float32), pad)
    k_blocks = kf.reshape(b, nh, nb, MOBA_BLOCK, dh)
    v_blocks = vf.reshape(b, nh, nb, MOBA_BLOCK, dh)
    k_mean = k_blocks.mean(axis=3)
    gate = jnp.einsum('bhtd,bhnd->bhtn', qf, k_mean)
    q_block = jnp.arange(sp) // MOBA_BLOCK
    fully_past = jnp.arange(nb)[None, :] < q_block[:, None]
    gate = jnp.where(fully_past, gate, -jnp.inf)
    n_sel = min(MOBA_TOPK, nb)
    _, sel = lax.top_k(gate, n_sel)
    sel_ok = sel < q_block[:, None]
    qs = qf * (dh ** -0.5)
    bi = jnp.arange(b)[:, None, None, None]
    hi = jnp.arange(nh)[None, :, None, None]
    hb = jnp.arange(nh)[None, :, None, None, None]
    offs = jnp.arange(MOBA_BLOCK)

    def chunk(start):
        t_pos = start + jnp.arange(MOBA_Q_CHUNK)
        qc = lax.dynamic_slice_in_dim(qs, start, MOBA_Q_CHUNK, axis=2)
        selc = lax.dynamic_slice_in_dim(sel, start, MOBA_Q_CHUNK, axis=2)
        okc = lax.dynamic_slice_in_dim(sel_ok, start, MOBA_Q_CHUNK, axis=2)
        k_sel = k_blocks[bi, hi, selc]
        v_sel = v_blocks[bi, hi, selc]
        kpos = selc[..., None] * MOBA_BLOCK + offs
        bias_sel = bias_table[rel_bucket(t_pos[:, None, None] - kpos), hb].astype(jnp.float32)
        s_sel = jnp.einsum('bhcd,bhcnkd->bhcnk', qc, k_sel) + bias_sel
        s_sel = jnp.where(okc[..., None], s_sel, -jnp.inf).reshape(b, nh, MOBA_Q_CHUNK, n_sel * MOBA_BLOCK)
        own0 = (start // MOBA_BLOCK) * MOBA_BLOCK
        k_own = lax.dynamic_slice_in_dim(kf, own0, MOBA_BLOCK, axis=2)
        v_own = lax.dynamic_slice_in_dim(vf, own0, MOBA_BLOCK, axis=2)
        kpos_own = own0 + offs
        bias_own = bias_table[rel_bucket(t_pos[:, None] - kpos_own[None, :])].astype(jnp.float32).transpose(2, 0, 1)
        s_own = jnp.einsum('bhcd,bhkd->bhck', qc, k_own) + bias_own
        s_own = jnp.where(kpos_own[None, :] <= t_pos[:, None], s_own, -jnp.inf)
        p = jax.nn.softmax(jnp.concatenate([s_sel, s_own], axis=-1), axis=-1)
        p_sel = p[..., :n_sel * MOBA_BLOCK].reshape(b, nh, MOBA_Q_CHUNK, n_sel, MOBA_BLOCK)
        p_own = p[..., n_sel * MOBA_BLOCK:]
        return jnp.einsum('bhcnk,bhcnkd->bhcd', p_sel, v_sel) + jnp.einsum('bhck,bhkd->bhcd', p_own, v_own)

    out = lax.map(chunk, jnp.arange(0, sp, MOBA_Q_CHUNK))
    out = out.transpose(1, 2, 0, 3, 4).reshape(b, nh, sp, dh)
    return out[:, :, :s]


def stick_breaking_attention(q, k, v):
    b, nh, s, dh = q.shape
    qf = q.astype(jnp.float32) * (dh ** -0.5)
    kf = k.astype(jnp.float32)
    vf = v.astype(jnp.float32)
    kpos = jnp.arange(s)

    def block(start):
        qc = lax.dynamic_slice_in_dim(qf, start, SB_Q_BLOCK, axis=2)
        z = jnp.einsum('bhqd,bhkd->bhqk', qc, kf)
        t_pos = start + jnp.arange(SB_Q_BLOCK)
        strict = kpos[None, :] < t_pos[:, None]
        log_keep = jnp.where(strict, jax.nn.log_sigmoid(-z), 0.0)
        after = lax.cumsum(log_keep, axis=3, reverse=True) - log_keep
        a = jnp.where(strict, jnp.exp(jax.nn.log_sigmoid(z) + after), 0.0)
        return jnp.einsum('bhqk,bhkd->bhqd', a, vf)

    out = lax.map(block, jnp.arange(0, s, SB_Q_BLOCK))
    return out.transpose(1, 2, 0, 3, 4).reshape(b, nh, s, dh)


def swa_sink_attention(q, k, v, bias_table, sinks):
    b, hq, s, dh = q.shape
    hkv = k.shape[1]
    g = hq // hkv
    w = SWA_WINDOW
    nb = s // w
    qf = q.astype(jnp.float32).reshape(b, hkv, g, nb, w, dh) * (dh ** -0.5)
    kf = k.astype(jnp.float32).reshape(b, hkv, nb, w, dh)
    vf = v.astype(jnp.float32).reshape(b, hkv, nb, w, dh)
    prev = lambda a: jnp.pad(a, ((0, 0), (0, 0), (1, 0), (0, 0), (0, 0)))[:, :, :-1]
    k_band = jnp.concatenate([prev(kf), kf], axis=3)
    v_band = jnp.concatenate([prev(vf), vf], axis=3)
    scores = jnp.einsum('bkgnqd,bknsd->bkgnqs', qf, k_band)
    i = jnp.arange(w)[:, None]
    j = jnp.arange(2 * w)[None, :]
    dist = i + w - j
    in_window = (dist >= 0) & (dist < w)
    blk_ok = (jnp.arange(nb)[:, None, None] > 0) | (j >= w)[None]
    mask = in_window[None] & blk_ok
    bias = bias_table[rel_bucket(dist)].astype(jnp.float32).transpose(2, 0, 1).reshape(hkv, g, 1, w, 2 * w)
    logits = jnp.where(mask, scores + bias, -jnp.inf)
    sink = sinks.astype(jnp.float32).reshape(1, hkv, g, 1, 1, 1)
    m = jnp.maximum(jnp.max(logits, axis=-1, keepdims=True), sink)
    e = jnp.exp(logits - m)
    p = e / (jnp.sum(e, axis=-1, keepdims=True) + jnp.exp(sink - m))
    out = jnp.einsum('bkgnqs,bknsd->bkgnqd', p, v_band)
    return out.reshape(b, hq, s, dh)


def mixing_sublayer(h, w_in, w_out, g_moba, g_sb, g_swa, sinks, rel_bias):
    proj = jnp.einsum('bsd,de->bse', h, w_in)
    sizes = [MOBA_WIDTH] * 3 + [SB_WIDTH] * 3 + [SWA_Q_WIDTH, SWA_KV_WIDTH, SWA_KV_WIDTH]
    cuts = [int(v) for v in np.cumsum(sizes)[:-1]]
    qa, ka, va, qb, kb, vb, qc, kc, vc = jnp.split(proj, cuts, axis=-1)
    o_a = moba_attention(split_heads(qa, MOBA_HEADS), split_heads(ka, MOBA_HEADS), split_heads(va, MOBA_HEADS), rel_bias[:, :MOBA_HEADS])
    o_b = stick_breaking_attention(split_heads(qb, SB_HEADS), split_heads(kb, SB_HEADS), split_heads(vb, SB_HEADS))
    o_c = swa_sink_attention(split_heads(qc, SWA_HEADS), split_heads(kc, SWA_KV_HEADS), split_heads(vc, SWA_KV_HEADS), rel_bias[:, MOBA_HEADS:], sinks)
    o = jnp.concatenate([merge_heads(head_rms_norm(o_a, g_moba)), merge_heads(head_rms_norm(o_b, g_sb)), merge_heads(head_rms_norm(o_c, g_swa))], axis=-1).astype(h.dtype)
    return jnp.einsum('bse,ed->bsd', o, w_out)


def swiglu(h, w_gate, w_up, w_down):
    a = jax.nn.silu(jnp.einsum('bsd,df->bsf', h, w_gate)) * jnp.einsum('bsd,df->bsf', h, w_up)
    return jnp.einsum('bsf,fd->bsd', a, w_down)


def moe_swiglu(h, w_router, w_gate, w_up, w_down):
    logits = jnp.einsum('bsd,de->bse', h.astype(jnp.float32), w_router.astype(jnp.float32))
    top_val, top_idx = lax.top_k(logits, TOP_K)
    top_w = jax.nn.softmax(top_val, axis=-1)
    combine = jnp.sum(jax.nn.one_hot(top_idx, N_EXPERTS, dtype=jnp.float32) * top_w[..., None], axis=-2)
    y = jnp.zeros_like(h)
    for e in range(N_EXPERTS):
        y = y + combine[..., e:e + 1].astype(h.dtype) * swiglu(h, w_gate[e], w_up[e], w_down[e])
    return y


def setup_inputs(seed: int = 0) -> dict:
    key = jax.random.key(seed)
    ks = jax.random.split(key, 22)
    f32 = jnp.float32

    def nrm(k, shape, scale):
        return jax.random.normal(k, shape, f32) * scale

    def gain(k, shape):
        return 1.0 + 0.02 * jax.random.normal(k, shape, f32)

    return {
        'x': nrm(ks[0], (BATCH, SEQ, D_MODEL), 1.0),
        'c': nrm(ks[1], (BATCH, D_MODEL), 1.0),
        'rel_bias': nrm(ks[2], (REL_BUCKETS, N_BIAS_HEADS), 0.5),
        'w_ada': nrm(ks[3], (DEPTH, D_MODEL, ADA_CHUNKS * D_MODEL), 0.5 * D_MODEL ** -0.5),
        'b_ada': nrm(ks[4], (DEPTH, ADA_CHUNKS * D_MODEL), 0.02),
        'g_pre_mix': gain(ks[5], (DEPTH, D_MODEL)),
        'w_in': nrm(ks[6], (DEPTH, D_MODEL, IN_WIDTH), D_MODEL ** -0.5),
        'g_grp_moba': gain(ks[7], (DEPTH, MOBA_WIDTH)),
        'g_grp_sb': gain(ks[8], (DEPTH, SB_WIDTH)),
        'g_grp_swa': gain(ks[9], (DEPTH, SWA_Q_WIDTH)),
        'swa_sinks': nrm(ks[10], (DEPTH, SWA_HEADS), 0.5),
        'w_out': nrm(ks[11], (DEPTH, MIX_WIDTH, D_MODEL), MIX_WIDTH ** -0.5),
        'g_post_mix': gain(ks[12], (DEPTH, D_MODEL)),
        'g_pre_ffn': gain(ks[13], (DEPTH, D_MODEL)),
        'w_ff_gate': nrm(ks[14], (N_DENSE, D_MODEL, D_FF), D_MODEL ** -0.5),
        'w_ff_up': nrm(ks[15], (N_DENSE, D_MODEL, D_FF), D_MODEL ** -0.5),
        'w_ff_down': nrm(ks[16], (N_DENSE, D_FF, D_MODEL), D_FF ** -0.5),
        'w_router': nrm(ks[17], (N_MOE, D_MODEL, N_EXPERTS), D_MODEL ** -0.5),
        'w_moe_gate': nrm(ks[18], (N_MOE, N_EXPERTS, D_MODEL, D_EXPERT), D_MODEL ** -0.5),
        'w_moe_up': nrm(ks[19], (N_MOE, N_EXPERTS, D_MODEL, D_EXPERT), D_MODEL ** -0.5),
        'w_moe_down': nrm(ks[20], (N_MOE, N_EXPERTS, D_EXPERT, D_MODEL), D_EXPERT ** -0.5),
        'g_post_ffn': gain(ks[21], (DEPTH, D_MODEL)),
    }


def reference(x, c, rel_bias, w_ada, b_ada, g_pre_mix, w_in, g_grp_moba, g_grp_sb, g_grp_swa, swa_sinks, w_out, g_post_mix, g_pre_ffn, w_ff_gate, w_ff_up, w_ff_down, w_router, w_moe_gate, w_moe_up, w_moe_down, g_post_ffn):
    for layer in range(DEPTH):
        mod = jnp.einsum('bd,de->be', jax.nn.silu(c), w_ada[layer]) + b_ada[layer]
        shift_m, scale_m, gate_m, shift_f, scale_f, gate_f = jnp.split(mod[:, None, :], ADA_CHUNKS, axis=-1)
        h = rms_norm(x, g_pre_mix[layer]) * (1 + scale_m) + shift_m
        y = mixing_sublayer(h, w_in[layer], w_out[layer], g_grp_moba[layer], g_grp_sb[layer], g_grp_swa[layer], swa_sinks[layer], rel_bias)
        x = x + gate_m * rms_norm(y, g_post_mix[layer])
        h = rms_norm(x, g_pre_ffn[layer]) * (1 + scale_f) + shift_f
        i = layer // 2
        if layer % 2 == 0:
            y = swiglu(h, w_ff_gate[i], w_ff_up[i], w_ff_down[i])
        else:
            y = moe_swiglu(h, w_router[i], w_moe_gate[i], w_moe_up[i], w_moe_down[i])
        x = x + gate_f * rms_norm(y, g_post_ffn[layer])
    return x
```

```python
import functools
import math

import jax
import jax.numpy as jnp
from jax import lax
from jax.experimental import pallas as pl
from jax.experimental.pallas import tpu as pltpu

HEAD_DIM = 128
MOBA_BLOCK = 256
MOBA_TOPK = 3
SB_BLOCK = 256
SWA_WINDOW = 128
REL_BUCKETS = 32
REL_MAX_EXACT = 16
REL_MAX_DISTANCE = 128
NORM_EPS = 1e-6
ADA_CHUNKS = 6
TOP_K = 2
LANES = 128
MOE_ROW_TILE = 512
VMEM_LIMIT = 56 * 1024 * 1024

F32 = jnp.float32
BF16 = jnp.bfloat16
NEG_INF = float("-inf")


def _params(sem, vmem=VMEM_LIMIT):
    return pltpu.CompilerParams(dimension_semantics=sem, vmem_limit_bytes=vmem)


def _nt_dot(a, b):
    return lax.dot_general(a, b, (((1,), (1,)), ((), ())), preferred_element_type=F32)


def _dot(a, b):
    return jnp.dot(a, b, preferred_element_type=F32)


def _split_bf16(x):
    hi = x.astype(BF16)
    lo = (x - hi.astype(F32)).astype(BF16)
    return hi, lo


def _rms(x):
    return x * lax.rsqrt(jnp.mean(x * x, axis=-1, keepdims=True) + NORM_EPS)


def _ada_kernel(c_ref, w_ref, b_ref, o_ref):
    c = c_ref[...]
    sc = c * jax.nn.sigmoid(c)
    hi, lo = _split_bf16(sc)
    w = w_ref[...].astype(BF16)
    o_ref[...] = _dot(hi, w) + _dot(lo, w) + b_ref[...]


def _ada_mod(c, w_ada, b_ada):
    depth, d, n = w_ada.shape
    b = c.shape[0]
    rows = 8
    c_pad = jnp.zeros((rows, d), F32).at[:b].set(c)
    tn = _col_tile(n, 512)
    out = pl.pallas_call(
        _ada_kernel,
        grid=(depth, n // tn),
        in_specs=[
            pl.BlockSpec((rows, d), lambda l, j: (0, 0)),
            pl.BlockSpec((None, d, tn), lambda l, j: (l, 0, j)),
            pl.BlockSpec((None, 1, tn), lambda l, j: (l, 0, j)),
        ],
        out_specs=pl.BlockSpec((None, rows, tn), lambda l, j: (l, 0, j)),
        out_shape=jax.ShapeDtypeStruct((depth, rows, n), F32),
        compiler_params=_params(("parallel", "parallel")),
        name="ada_mod",
    )(c_pad, w_ada, b_ada.reshape(depth, 1, n))
    return out[:, :b]


def _prenorm_kernel(x_ref, g_ref, sc_ref, sh_ref, h_ref):
    x = x_ref[...]
    h = (_rms(x) * g_ref[...]) * (1.0 + sc_ref[...]) + sh_ref[...]
    h_ref[...] = h.astype(h_ref.dtype)


def _prenorm(x, g, scale, shift, ts=256):
    b, s, d = x.shape
    return pl.pallas_call(
        _prenorm_kernel,
        grid=(b, s // ts),
        in_specs=[
            pl.BlockSpec((None, ts, d), lambda i, j: (i, j, 0)),
            pl.BlockSpec((1, d), lambda i, j: (0, 0)),
            pl.BlockSpec((None, 1, d), lambda i, j: (i, 0, 0)),
            pl.BlockSpec((None, 1, d), lambda i, j: (i, 0, 0)),
        ],
        out_specs=pl.BlockSpec((None, ts, d), lambda i, j: (i, j, 0)),
        out_shape=jax.ShapeDtypeStruct((b, s, d), BF16),
        compiler_params=_params(("parallel", "parallel")),
        name="prenorm",
    )(x, g.reshape(1, d), scale, shift)


def _post_kernel(y_ref, x_ref, gate_ref, gpost_ref, gnext_ref, sc_ref, sh_ref, *rest,
                 emit_h, route):
    if route:
        wr_ref, rest = rest[0], rest[1:]
    xo_ref, rest = rest[0], rest[1:]
    x_new = x_ref[...] + gate_ref[...] * (_rms(y_ref[...]) * gpost_ref[...])
    xo_ref[...] = x_new
    if not emit_h:
        return
    h = (_rms(x_new) * gnext_ref[...]) * (1.0 + sc_ref[...]) + sh_ref[...]
    if not route:
        rest[0][...] = h.astype(BF16)
        return
    hf_ref, comb_ref, sel_ref = rest
    hf_ref[...] = h
    hi, lo = _split_bf16(h)
    w = wr_ref[...]
    whi, wlo = _split_bf16(w)
    logits = _dot(hi, whi) + _dot(lo, whi) + _dot(hi, wlo)
    n_exp = route
    lane = lax.broadcasted_iota(jnp.int32, logits.shape, 1)
    lg = jnp.where(lane < n_exp, logits, NEG_INF)
    m1 = jnp.max(lg, axis=1, keepdims=True)
    i1 = jnp.min(jnp.where(lg == m1, lane, LANES), axis=1, keepdims=True)
    lg2 = jnp.where(lane == i1, NEG_INF, lg)
    m2 = jnp.max(lg2, axis=1, keepdims=True)
    i2 = jnp.min(jnp.where(lg2 == m2, lane, LANES), axis=1, keepdims=True)
    e2 = jnp.exp(m2 - m1)
    w1 = 1.0 / (1.0 + e2)
    w2 = e2 / (1.0 + e2)
    comb_ref[...] = jnp.where(lane == i1, w1, 0.0) + jnp.where(lane == i2, w2, 0.0)
    sel_ref[...] = jnp.where((lane == i1) | (lane == i2), 1.0, 0.0)


def _post(y, x, gate, g_post, g_next=None, scale=None, shift=None, w_router=None, ts=256):
    b, s, d = x.shape
    emit_h = g_next is not None
    n_exp = 0 if w_router is None else w_router.shape[1]
    if not emit_h:
        g_next, scale, shift = g_post, gate, gate
    row = pl.BlockSpec((None, ts, d), lambda i, j: (i, j, 0))
    vec = pl.BlockSpec((1, d), lambda i, j: (0, 0))
    per_b = pl.BlockSpec((None, 1, d), lambda i, j: (i, 0, 0))
    in_specs = [row, row, per_b, vec, vec, per_b, per_b]
    args = [y.reshape(b, s, d), x, gate, g_post.reshape(1, d), g_next.reshape(1, d), scale, shift]
    out_specs = [row]
    out_shape = [jax.ShapeDtypeStruct((b, s, d), F32)]
    if n_exp:
        wr = jnp.zeros((d, LANES), F32).at[:, :n_exp].set(w_router.astype(F32))
        in_specs.append(pl.BlockSpec((d, LANES), lambda i, j: (0, 0)))
        args.append(wr)
        lane_blk = pl.BlockSpec((None, ts, LANES), lambda i, j: (i, j, 0))
        out_specs += [row, lane_blk, lane_blk]
        out_shape += [jax.ShapeDtypeStruct((b, s, d), F32),
                      jax.ShapeDtypeStruct((b, s, LANES), F32),
                      jax.ShapeDtypeStruct((b, s, LANES), F32)]
    elif emit_h:
        out_specs.append(row)
        out_shape.append(jax.ShapeDtypeStruct((b, s, d), BF16))
    return pl.pallas_call(
        functools.partial(_post_kernel, emit_h=emit_h, route=n_exp),
        grid=(b, s // ts),
        in_specs=in_specs,
        out_specs=out_specs,
        out_shape=out_shape,
        compiler_params=_params(("parallel", "parallel")),
        name="post_norm_residual",
    )(*args)


def _mm_kernel(a_ref, w_ref, o_ref):
    o_ref[...] = _dot(a_ref[...], w_ref[...].astype(BF16)).astype(o_ref.dtype)


def _matmul(a, w, layer, out_dtype, tm=1024, tn=512):
    m, k = a.shape
    n = w.shape[2]
    tm, tn = _col_tile(m, tm), _col_tile(n, tn)
    return pl.pallas_call(
        _mm_kernel,
        grid=(n // tn, m // tm),
        in_specs=[
            pl.BlockSpec((tm, k), lambda j, i: (i, 0)),
            pl.BlockSpec((None, k, tn), lambda j, i: (layer, 0, j)),
        ],
        out_specs=pl.BlockSpec((tm, tn), lambda j, i: (i, j)),
        out_shape=jax.ShapeDtypeStruct((m, n), out_dtype),
        compiler_params=_params(("parallel", "parallel")),
        name="matmul",
    )(a, w)


def _out_proj_kernel(oa_ref, ob_ref, oc_ref, w_ref, y_ref, *, cuts):
    c1, c2 = cuts
    w = w_ref[...].astype(BF16)
    y = _dot(oa_ref[...], w[:c1]) + _dot(ob_ref[...], w[c1:c2]) + _dot(oc_ref[...], w[c2:])
    y_ref[...] = y


def _out_proj(oa, ob, oc, w, layer, tm=1024, tn=512):
    m = oa.shape[0]
    k, n = w.shape[1], w.shape[2]
    c1 = oa.shape[1]
    c2 = c1 + ob.shape[1]
    tm, tn = _col_tile(m, tm), _col_tile(n, tn)
    return pl.pallas_call(
        functools.partial(_out_proj_kernel, cuts=(c1, c2)),
        grid=(n // tn, m // tm),
        in_specs=[
            pl.BlockSpec((tm, oa.shape[1]), lambda j, i: (i, 0)),
            pl.BlockSpec((tm, ob.shape[1]), lambda j, i: (i, 0)),
            pl.BlockSpec((tm, oc.shape[1]), lambda j, i: (i, 0)),
            pl.BlockSpec((None, k, tn), lambda j, i: (layer, 0, j)),
        ],
        out_specs=pl.BlockSpec((tm, tn), lambda j, i: (i, j)),
        out_shape=jax.ShapeDtypeStruct((m, n), F32),
        compiler_params=_params(("parallel", "parallel")),
        name="out_proj",
    )(oa, ob, oc, w)


def _swiglu_up_kernel(a_ref, wg_ref, wu_ref, o_ref):
    a = a_ref[...]
    g = _dot(a, wg_ref[...].astype(BF16))
    u = _dot(a, wu_ref[...].astype(BF16))
    o_ref[...] = (g * jax.nn.sigmoid(g) * u).astype(o_ref.dtype)


def _swiglu_up(a, w_gate, w_up, layer, tm=1024, tn=256):
    m, k = a.shape
    n = w_gate.shape[2]
    tm, tn = _col_tile(m, tm), _col_tile(n, tn)
    wspec = pl.BlockSpec((None, k, tn), lambda j, i: (layer, 0, j))
    return pl.pallas_call(
        _swiglu_up_kernel,
        grid=(n // tn, m // tm),
        in_specs=[pl.BlockSpec((tm, k), lambda j, i: (i, 0)), wspec, wspec],
        out_specs=pl.BlockSpec((tm, tn), lambda j, i: (i, j)),
        out_shape=jax.ShapeDtypeStruct((m, n), BF16),
        compiler_params=_params(("parallel", "parallel")),
        name="swiglu_up",
    )(a, w_gate, w_up)


def _mm_acc_kernel(a_ref, w_ref, o_ref):
    kk = pl.program_id(2)
    p = _dot(a_ref[...], w_ref[...].astype(BF16))

    @pl.when(kk == 0)
    def _():
        o_ref[...] = p

    @pl.when(kk > 0)
    def _():
        o_ref[...] += p


def _matmul_ktiled(a, w, layer, tm=2048, tn=1024, tk=1024):
    m, k = a.shape
    n = w.shape[2]
    tm, tn, tk = _col_tile(m, tm), _col_tile(n, tn), _col_tile(k, tk)
    return pl.pallas_call(
        _mm_acc_kernel,
        grid=(n // tn, m // tm, k // tk),
        in_specs=[
            pl.BlockSpec((tm, tk), lambda j, i, kk: (i, kk)),
            pl.BlockSpec((None, tk, tn), lambda j, i, kk: (layer, kk, j)),
        ],
        out_specs=pl.BlockSpec((tm, tn), lambda j, i, kk: (i, j)),
        out_shape=jax.ShapeDtypeStruct((m, n), F32),
        compiler_params=_params(("parallel", "parallel", "arbitrary")),
        name="matmul_ktiled",
    )(a, w)


def _rel_bucket(dist):
    n = jnp.maximum(dist, 0)
    nf = jnp.maximum(n, 1).astype(F32)
    large = REL_MAX_EXACT + (jnp.log(nf / REL_MAX_EXACT) / math.log(REL_MAX_DISTANCE / REL_MAX_EXACT)
                             * (REL_BUCKETS - REL_MAX_EXACT)).astype(jnp.int32)
    return jnp.where(n < REL_MAX_EXACT, n, jnp.minimum(large, REL_BUCKETS - 1))


def _head_norm(o, g):
    return (_rms(o) * g).astype(BF16)


def _moba_kernel(q_ref, k_ref, v_ref, bias_ref, g_ref, o_ref, kmean_ref, m_ref, l_ref, acc_ref,
                 *, nb, scale):
    i = pl.program_id(2)
    blk = MOBA_BLOCK

    @pl.when(i == 0)
    def _():
        kf = k_ref[...].astype(F32).reshape(nb, blk, HEAD_DIM)
        kmean_ref[...] = jnp.mean(kf, axis=1)

    q = q_ref[...]
    khi, klo = _split_bf16(kmean_ref[...])
    gate = _nt_dot(q, khi) + _nt_dot(q, klo)
    col = lax.broadcasted_iota(jnp.int32, gate.shape, 1)
    gate = jnp.where(col < i, gate, NEG_INF)
    sel = jnp.zeros(gate.shape, F32)
    for _ in range(min(MOBA_TOPK, nb)):
        m = jnp.max(gate, axis=1, keepdims=True)
        first = jnp.min(jnp.where((gate == m) & (m > NEG_INF), col, nb), axis=1, keepdims=True)
        pick = col == first
        sel = jnp.where(pick, 1.0, sel)
        gate = jnp.where(pick, NEG_INF, gate)

    r2 = lax.broadcasted_iota(jnp.int32, (blk, blk), 0)
    c2 = lax.broadcasted_iota(jnp.int32, (blk, blk), 1)
    own = pl.multiple_of(i * blk, blk)
    s = _nt_dot(q, k_ref[pl.ds(own, blk), :]) * scale + bias_ref[0]
    s = jnp.where(c2 <= r2, s, NEG_INF)
    m0 = jnp.max(s, axis=1, keepdims=True)
    p = jnp.exp(s - m0)
    m_ref[...] = m0
    l_ref[...] = jnp.sum(p, axis=1, keepdims=True)
    acc_ref[...] = _dot(p.astype(BF16), v_ref[pl.ds(own, blk), :])

    for n in range(nb - 1):
        @pl.when(n < i)
        def _(n=n):
            bias = bias_ref[jnp.minimum(i - n, 2)]
            sn = _nt_dot(q, k_ref[n * blk:(n + 1) * blk, :]) * scale + bias
            sn = jnp.where(sel[:, n:n + 1] > 0.0, sn, NEG_INF)
            m_old = m_ref[...]
            m_new = jnp.maximum(m_old, jnp.max(sn, axis=1, keepdims=True))
            alpha = jnp.exp(m_old - m_new)
            pn = jnp.exp(sn - m_new)
            m_ref[...] = m_new
            l_ref[...] = alpha * l_ref[...] + jnp.sum(pn, axis=1, keepdims=True)
            acc_ref[...] = alpha * acc_ref[...] + _dot(pn.astype(BF16), v_ref[n * blk:(n + 1) * blk, :])

    o_ref[...] = _head_norm(acc_ref[...] / l_ref[...], g_ref[...])


def _moba(proj, rel_bd, g, heads, q0, k0, v0, rel_head0):
    b, s, _ = proj.shape
    blk = MOBA_BLOCK
    nb = s // blk
    r = jnp.arange(blk)[:, None]
    c = jnp.arange(blk)[None, :]
    hb = rel_bd[:, rel_head0:rel_head0 + heads]
    far = jnp.broadcast_to(hb[2 * blk - 1][None, None, :], (blk, blk, heads))
    bias = jnp.stack([hb[jnp.clip(r - c, 0, None)], hb[blk + r - c], far], axis=0)
    bias = bias.transpose(3, 0, 1, 2).astype(F32)
    seq = lambda c0: pl.BlockSpec((None, s, HEAD_DIM), lambda bi, h, i: (bi, 0, c0 + h))
    return pl.pallas_call(
        functools.partial(_moba_kernel, nb=nb, scale=HEAD_DIM ** -0.5),
        grid=(b, heads, nb),
        in_specs=[
            pl.BlockSpec((None, blk, HEAD_DIM), lambda bi, h, i: (bi, i, q0 + h)),
            seq(k0), seq(v0),
            pl.BlockSpec((None, 3, blk, blk), lambda bi, h, i: (h, 0, 0, 0)),
            pl.BlockSpec((1, HEAD_DIM), lambda bi, h, i: (0, h)),
        ],
        out_specs=pl.BlockSpec((None, blk, HEAD_DIM), lambda bi, h, i: (bi, i, h)),
        out_shape=jax.ShapeDtypeStruct((b, s, heads * HEAD_DIM), BF16),
        scratch_shapes=[pltpu.VMEM((nb, HEAD_DIM), F32), pltpu.VMEM((blk, 1), F32),
                        pltpu.VMEM((blk, 1), F32), pltpu.VMEM((blk, HEAD_DIM), F32)],
        compiler_params=_params(("parallel", "parallel", "arbitrary")),
        name="moba_attention",
    )(proj, proj, proj, bias, g.reshape(1, -1))


def _sb_kernel(q_ref, k_ref, v_ref, g_ref, o_ref, *, scale):
    i = pl.program_id(2)
    blk = SB_BLOCK
    q = q_ref[...]
    r2 = lax.broadcasted_iota(jnp.int32, (blk, blk), 0)
    c2 = lax.broadcasted_iota(jnp.int32, (blk, blk), 1)
    suffix = jnp.where(r2 >= c2, 1.0, 0.0).astype(BF16)

    def body(step, state):
        acc, carry = state
        j = i - step
        start = pl.multiple_of(j * blk, blk)
        z = _nt_dot(q, k_ref[pl.ds(start, blk), :]) * scale
        strict = c2 + (j - i) * blk < r2
        sp = jnp.maximum(z, 0.0) + jnp.log(1.0 + jnp.exp(-jnp.abs(z)))
        sp = jnp.where(strict, sp, 0.0)
        hi, lo = _split_bf16(sp)
        tot = _dot(hi, suffix) + _dot(lo, suffix) + carry
        a = jnp.where(strict, jnp.exp(z - tot), 0.0)
        acc = acc + _dot(a.astype(BF16), v_ref[pl.ds(start, blk), :])
        return acc, tot[:, 0:1]

    acc, _ = lax.fori_loop(0, i + 1, body,
                           (jnp.zeros((blk, HEAD_DIM), F32), jnp.zeros((blk, 1), F32)))
    o_ref[...] = _head_norm(acc, g_ref[...])


def _stick_breaking(proj, g, heads, q0, k0, v0):
    b, s, _ = proj.shape
    blk = SB_BLOCK
    seq = lambda c0: pl.BlockSpec((None, s, HEAD_DIM), lambda bi, h, i: (bi, 0, c0 + h))
    return pl.pallas_call(
        functools.partial(_sb_kernel, scale=HEAD_DIM ** -0.5),
        grid=(b, heads, s // blk),
        in_specs=[
            pl.BlockSpec((None, blk, HEAD_DIM), lambda bi, h, i: (bi, i, q0 + h)),
            seq(k0), seq(v0),
            pl.BlockSpec((1, HEAD_DIM), lambda bi, h, i: (0, h)),
        ],
        out_specs=pl.BlockSpec((None, blk, HEAD_DIM), lambda bi, h, i: (bi, i, h)),
        out_shape=jax.ShapeDtypeStruct((b, s, heads * HEAD_DIM), BF16),
        compiler_params=_params(("parallel", "parallel", "parallel")),
        name="stick_breaking_attention",
    )(proj, proj, proj, g.reshape(1, -1))


def _swa_kernel(q_ref, kp_ref, kc_ref, vp_ref, vc_ref, bias_ref, sink_ref, g_ref, o_ref,
                *, group, scale):
    n = pl.program_id(2)
    w = SWA_WINDOW
    kk = jnp.concatenate([kp_ref[...], kc_ref[...]], axis=0)
    vv = jnp.concatenate([vp_ref[...], vc_ref[...]], axis=0)
    r2 = lax.broadcasted_iota(jnp.int32, (w, 2 * w), 0)
    c2 = lax.broadcasted_iota(jnp.int32, (w, 2 * w), 1)
    dist = r2 + w - c2
    mask = (dist >= 0) & (dist < w) & (c2 + n * w >= w)
    for gi in range(group):
        cols = slice(gi * HEAD_DIM, (gi + 1) * HEAD_DIM)
        s = _nt_dot(q_ref[:, cols], kk) * scale + bias_ref[gi]
        s = jnp.where(mask, s, NEG_INF)
        sink = sink_ref[gi:gi + 1, 0:1]
        m = jnp.maximum(jnp.max(s, axis=1, keepdims=True), sink)
        e = jnp.exp(s - m)
        p = e / (jnp.sum(e, axis=1, keepdims=True) + jnp.exp(sink - m))
        o = _dot(p.astype(BF16), vv)
        o_ref[:, cols] = _head_norm(o, g_ref[:, cols])


def _swa(proj, rel_bd, sinks, g, q_heads, kv_heads, q0, k0, v0, rel_head0):
    b, s, _ = proj.shape
    w = SWA_WINDOW
    group = q_heads // kv_heads
    r = jnp.arange(w)[:, None]
    c = jnp.arange(2 * w)[None, :]
    bias = rel_bd[jnp.clip(r + w - c, 0, None)][:, :, rel_head0:rel_head0 + q_heads]
    bias = bias.transpose(2, 0, 1).astype(F32)
    sink_rows = jnp.broadcast_to(sinks.astype(F32)[:, None], (q_heads, LANES))
    kv = lambda c0, prev: pl.BlockSpec(
        (None, w, HEAD_DIM),
        lambda bi, kh, n: (bi, jnp.maximum(n - 1, 0) if prev else n, c0 + kh))
    return pl.pallas_call(
        functools.partial(_swa_kernel, group=group, scale=HEAD_DIM ** -0.5),
        grid=(b, kv_heads, s // w),
        in_specs=[
            pl.BlockSpec((None, w, group * HEAD_DIM), lambda bi, kh, n: (bi, n, q0 // group + kh)),
            kv(k0, True), kv(k0, False), kv(v0, True), kv(v0, False),
            pl.BlockSpec((group, w, 2 * w), lambda bi, kh, n: (kh, 0, 0)),
            pl.BlockSpec((group, LANES), lambda bi, kh, n: (kh, 0)),
            pl.BlockSpec((1, group * HEAD_DIM), lambda bi, kh, n: (0, kh)),
        ],
        out_specs=pl.BlockSpec((None, w, group * HEAD_DIM), lambda bi, kh, n: (bi, n, kh)),
        out_shape=jax.ShapeDtypeStruct((b, s, q_heads * HEAD_DIM), BF16),
        compiler_params=_params(("parallel", "parallel", "parallel")),
        name="swa_sink_attention",
    )(proj, proj, proj, proj, proj, bias, sink_rows, g.reshape(1, -1))


def _mixing(h, rel_bias, w_in, w_out, g_moba, g_sb, g_swa, sinks, layer):
    b, s, d = h.shape
    moba_w, sb_w, swa_w = g_moba.shape[1], g_sb.shape[1], g_swa.shape[1]
    in_w = w_in.shape[2]
    kv_w = (in_w - 3 * moba_w - 3 * sb_w - swa_w) // 2
    mh, sh, qh, kvh = (moba_w // HEAD_DIM, sb_w // HEAD_DIM, swa_w // HEAD_DIM, kv_w // HEAD_DIM)
    assert qh % kvh == 0 and (3 * mh + 3 * sh) % (qh // kvh) == 0
    proj = _matmul(h.reshape(b * s, d), w_in, layer, BF16).reshape(b, s, in_w)
    dists = jnp.arange(2 * MOBA_BLOCK)
    rel_bd = rel_bias[_rel_bucket(dists)]
    o_a = _moba(proj, rel_bd, g_moba[layer], mh, 0, mh, 2 * mh, 0)
    o_b = _stick_breaking(proj, g_sb[layer], sh, 3 * mh, 3 * mh + sh, 3 * mh + 2 * sh)
    c0 = 3 * mh + 3 * sh
    o_c = _swa(proj, rel_bd, sinks[layer], g_swa[layer], qh, kvh, c0, c0 + qh, c0 + qh + kvh, mh)
    m = b * s
    return _out_proj(o_a.reshape(m, -1), o_b.reshape(m, -1), o_c.reshape(m, -1), w_out, layer)


def _gather_kernel(tok_ref, src_ref, o_ref, buf_ref, sem_ref, *, rows):
    base = pl.program_id(0) * rows

    def copy(r):
        return pltpu.make_async_copy(src_ref.at[pl.ds(tok_ref[base + r], 1)],
                                     buf_ref.at[pl.ds(r, 1)], sem_ref)

    def start(r, c):
        copy(r).start()
        return c

    def wait(r, c):
        copy(r).wait()
        return c

    lax.fori_loop(0, rows, start, 0)
    lax.fori_loop(0, rows, wait, 0)
    o_ref[...] = buf_ref[...].astype(o_ref.dtype)


def _gather_rows(src, token_of, rows=256):
    p = token_of.shape[0]
    d = src.shape[1]
    return pl.pallas_call(
        functools.partial(_gather_kernel, rows=rows),
        grid_spec=pltpu.PrefetchScalarGridSpec(
            num_scalar_prefetch=1,
            grid=(p // rows,),
            in_specs=[pl.BlockSpec(memory_space=pl.ANY)],
            out_specs=pl.BlockSpec((rows, d), lambda i, tok: (i, 0)),
            scratch_shapes=[pltpu.VMEM((rows, d), F32), pltpu.SemaphoreType.DMA(())],
        ),
        out_shape=jax.ShapeDtypeStruct((p, d), BF16),
        compiler_params=_params(("arbitrary",)),
        name="moe_gather",
    )(token_of, src)


def _gmm_kernel(tile_ref, jn_ref, exp_ref, otile_ref, ojn_ref, live_ref, a_ref, *rest, dual):
    s = pl.program_id(0)

    @pl.when(live_ref[s] == 0)
    def _():
        rest[-1][...] = jnp.zeros_like(rest[-1])

    @pl.when(live_ref[s] > 0)
    def _():
        a = a_ref[...]
        if dual:
            wg_ref, wu_ref, o_ref = rest
            g = _dot(a, wg_ref[...].astype(BF16))
            u = _dot(a, wu_ref[...].astype(BF16))
            o_ref[...] = (g * jax.nn.sigmoid(g) * u).astype(o_ref.dtype)
        else:
            w_ref, o_ref = rest
            o_ref[...] = _dot(a, w_ref[...].astype(BF16)).astype(o_ref.dtype)


def _grouped_matmul(a, weights, layer, sched, out_dtype, tn):
    p, k = a.shape
    n = weights[0].shape[3]
    tm = MOE_ROW_TILE
    steps = sched[0].shape[0]
    wspec = pl.BlockSpec((None, None, k, tn),
                         lambda s, t, j, e, ot, oj, v: (layer, e[s], 0, j[s]))
    return pl.pallas_call(
        functools.partial(_gmm_kernel, dual=len(weights) == 2),
        grid_spec=pltpu.PrefetchScalarGridSpec(
            num_scalar_prefetch=6,
            grid=(steps,),
            in_specs=[pl.BlockSpec((tm, k), lambda s, t, j, e, ot, oj, v: (t[s], 0))]
            + [wspec] * len(weights),
            out_specs=pl.BlockSpec((tm, tn), lambda s, t, j, e, ot, oj, v: (ot[s], oj[s])),
        ),
        out_shape=jax.ShapeDtypeStruct((p, n), out_dtype),
        compiler_params=_params(("arbitrary",)),
        name="moe_grouped_matmul",
    )(*sched, a, *weights)


def _combine_kernel(pos_ref, src_ref, w_ref, o_ref, buf_ref, sem_ref, *, rows, tokens):
    base = pl.program_id(0) * rows

    def copy(r, slot):
        return pltpu.make_async_copy(src_ref.at[pl.ds(pos_ref[slot * tokens + base + r], 1)],
                                     buf_ref.at[slot, pl.ds(r, 1)], sem_ref)

    def start(r, c):
        for slot in range(TOP_K):
            copy(r, slot).start()
        return c

    def wait(r, c):
        for slot in range(TOP_K):
            copy(r, slot).wait()
        return c

    lax.fori_loop(0, rows, start, 0)
    lax.fori_loop(0, rows, wait, 0)
    wts = w_ref[...]
    o_ref[...] = wts[:, 0:1] * buf_ref[0] + wts[:, 1:2] * buf_ref[1]


def _combine(expert_out, pos, wts, rows=256):
    t = pos.shape[0] // TOP_K
    d = expert_out.shape[1]
    return pl.pallas_call(
        functools.partial(_combine_kernel, rows=rows, tokens=t),
        grid_spec=pltpu.PrefetchScalarGridSpec(
            num_scalar_prefetch=1,
            grid=(t // rows,),
            in_specs=[pl.BlockSpec(memory_space=pl.ANY),
                      pl.BlockSpec((rows, LANES), lambda i, pos: (i, 0))],
            out_specs=pl.BlockSpec((rows, d), lambda i, pos: (i, 0)),
            scratch_shapes=[pltpu.VMEM((TOP_K, rows, d), F32), pltpu.SemaphoreType.DMA(())],
        ),
        out_shape=jax.ShapeDtypeStruct((t, d), F32),
        compiler_params=_params(("arbitrary",)),
        name="moe_combine",
    )(pos, expert_out, wts)


def _route_plan(sel, comb, n_exp):
    t = sel.shape[0]
    tm = MOE_ROW_TILE
    n_tiles = (t * TOP_K) // tm + n_exp
    mask = sel > 0.5
    cnt = jnp.sum(mask, axis=0).astype(jnp.int32)
    tiles_e = (cnt + tm - 1) // tm
    tile_end = jnp.cumsum(tiles_e)
    tile_start = tile_end - tiles_e
    rank = jnp.cumsum(mask, axis=0).astype(jnp.int32) - 1
    pos_te = tile_start[None, :] * tm + rank
    order = jnp.argsort(jnp.logical_not(mask), axis=1, stable=True)[:, :TOP_K].astype(jnp.int32)
    pos = jnp.take_along_axis(pos_te, order, axis=1)
    wts = jnp.take_along_axis(comb, order, axis=1)
    token_of = jnp.zeros((n_tiles * tm,), jnp.int32).at[pos.reshape(-1)].set(
        jnp.repeat(jnp.arange(t, dtype=jnp.int32), TOP_K))
    wts_pad = jnp.zeros((t, LANES), F32).at[:, :TOP_K].set(wts)

    def schedule(col_tiles):
        steps = jnp.arange(n_tiles * col_tiles, dtype=jnp.int32)
        live = tile_end[-1] * col_tiles
        last = jnp.maximum(live - 1, 0)
        steps_c = jnp.minimum(steps, last)
        e_of = jnp.minimum(jnp.searchsorted(tile_end * col_tiles, steps_c, side="right"),
                           n_exp - 1).astype(jnp.int32)
        local = steps_c - tile_start[e_of] * col_tiles
        n_e = jnp.maximum(tiles_e[e_of], 1)
        j_of = (local // n_e).astype(jnp.int32)
        t_of = (tile_start[e_of] + local % n_e).astype(jnp.int32)
        n_dead = jnp.maximum(n_tiles - tile_end[-1], 1)
        dead = jnp.maximum(steps - live, 0)
        is_live = steps < live
        ot_of = jnp.where(is_live, t_of, tile_end[-1] + dead % n_dead).astype(jnp.int32)
        oj_of = jnp.where(is_live, j_of, dead // n_dead).astype(jnp.int32)
        return t_of, j_of, e_of, ot_of, oj_of, is_live.astype(jnp.int32)

    return token_of, schedule, pos.T.reshape(-1), wts_pad


def _col_tile(n, pref):
    while n % pref:
        pref //= 2
    return pref


def _moe(h_f32, comb, sel, w_gate, w_up, w_down, layer_idx):
    t, d = h_f32.shape
    n_exp = w_gate.shape[1]
    d_exp = w_gate.shape[3]
    tn_up, tn_down = _col_tile(d_exp, 256), _col_tile(d, 512)
    token_of, schedule, pos, wts = _route_plan(sel[:, :n_exp], comb[:, :n_exp], n_exp)
    hs = _gather_rows(h_f32, token_of)
    act = _grouped_matmul(hs, (w_gate, w_up), layer_idx, schedule(d_exp // tn_up), BF16, tn_up)
    out = _grouped_matmul(act, (w_down,), layer_idx, schedule(d // tn_down), F32, tn_down)
    return _combine(out, pos, wts)


def kernel(x, c, rel_bias, w_ada, b_ada, g_pre_mix, w_in, g_grp_moba, g_grp_sb, g_grp_swa, swa_sinks, w_out, g_post_mix, g_pre_ffn, w_ff_gate, w_ff_up, w_ff_down, w_router, w_moe_gate, w_moe_up, w_moe_down, g_post_ffn):
    b, s, d = x.shape
    depth = w_ada.shape[0]
    m = b * s
    mod = _ada_mod(c, w_ada, b_ada).reshape(depth, b, ADA_CHUNKS, 1, d)
    chunk = lambda layer, idx: mod[layer, :, idx]
    h = _prenorm(x, g_pre_mix[0], chunk(0, 1), chunk(0, 0))
    for layer in range(depth):
        shift_f, scale_f, gate_f = chunk(layer, 3), chunk(layer, 4), chunk(layer, 5)
        y = _mixing(h, rel_bias, w_in, w_out, g_grp_moba, g_grp_sb, g_grp_swa, swa_sinks, layer)
        idx = layer // 2
        if layer % 2 == 0:
            x, h = _post(y, x, chunk(layer, 2), g_post_mix[layer], g_pre_ffn[layer], scale_f, shift_f)
            act = _swiglu_up(h.reshape(m, d), w_ff_gate, w_ff_up, idx)
            y = _matmul_ktiled(act, w_ff_down, idx)
        else:
            x, hf, comb, sel = _post(y, x, chunk(layer, 2), g_post_mix[layer], g_pre_ffn[layer],
                                     scale_f, shift_f, w_router=w_router[idx])
            y = _moe(hf.reshape(m, d), comb.reshape(m, LANES), sel.reshape(m, LANES),
                     w_moe_gate, w_moe_up, w_moe_down, idx)
        if layer + 1 < depth:
            x, h = _post(y, x, gate_f, g_post_ffn[layer], g_pre_mix[layer + 1],
                         chunk(layer + 1, 1), chunk(layer + 1, 0))
        else:
            (x,) = _post(y, x, gate_f, g_post_ffn[layer])
    return x
```

```python
import functools
import math

import jax
import jax.numpy as jnp
from jax import lax
from jax.experimental import pallas as pl
from jax.experimental.pallas import tpu as pltpu

HEAD_DIM = 128
MOBA_BLOCK = 256
MOBA_TOPK = 3
SB_BLOCK = 256
SWA_WINDOW = 128
REL_BUCKETS = 32
REL_MAX_EXACT = 16
REL_MAX_DISTANCE = 128
NORM_EPS = 1e-6
ADA_CHUNKS = 6
TOP_K = 2
LANES = 128
MOE_ROW_TILE = 512
VMEM_LIMIT = 56 * 1024 * 1024

F32 = jnp.float32
BF16 = jnp.bfloat16
NEG_INF = float("-inf")
LOG2E = math.log2(math.e)


def _params(sem, vmem=VMEM_LIMIT):
    return pltpu.CompilerParams(dimension_semantics=sem, vmem_limit_bytes=vmem)


def _nt_dot(a, b):
    return lax.dot_general(a, b, (((1,), (1,)), ((), ())), preferred_element_type=F32)


def _dot(a, b):
    return jnp.dot(a, b, preferred_element_type=F32)


def _tn_dot(a, b):
    return lax.dot_general(a, b, (((0,), (0,)), ((), ())), preferred_element_type=F32)


def _split_bf16(x):
    hi = x.astype(BF16)
    lo = (x - hi.astype(F32)).astype(BF16)
    return hi, lo


def _rms(x):
    return x * lax.rsqrt(jnp.mean(x * x, axis=-1, keepdims=True) + NORM_EPS)


def _ada_kernel(c_ref, w_ref, b_ref, o_ref):
    c = c_ref[...]
    sc = c * jax.nn.sigmoid(c)
    hi, lo = _split_bf16(sc)
    w = w_ref[...].astype(BF16)
    o_ref[...] = _dot(hi, w) + _dot(lo, w) + b_ref[...]


def _ada_mod(c, w_ada, b_ada):
    depth, d, n = w_ada.shape
    b = c.shape[0]
    rows = 8
    c_pad = jnp.zeros((rows, d), F32).at[:b].set(c)
    tn = _col_tile(n, 512)
    out = pl.pallas_call(
        _ada_kernel,
        grid=(depth, n // tn),
        in_specs=[
            pl.BlockSpec((rows, d), lambda l, j: (0, 0)),
            pl.BlockSpec((None, d, tn), lambda l, j: (l, 0, j)),
            pl.BlockSpec((None, 1, tn), lambda l, j: (l, 0, j)),
        ],
        out_specs=pl.BlockSpec((None, rows, tn), lambda l, j: (l, 0, j)),
        out_shape=jax.ShapeDtypeStruct((depth, rows, n), F32),
        compiler_params=_params(("parallel", "parallel")),
        name="ada_mod",
    )(c_pad, w_ada, b_ada.reshape(depth, 1, n))
    return out[:, :b]


def _prenorm_kernel(x_ref, g_ref, sc_ref, sh_ref, h_ref):
    x = x_ref[...]
    h = (_rms(x) * g_ref[...]) * (1.0 + sc_ref[...]) + sh_ref[...]
    h_ref[...] = h.astype(h_ref.dtype)


def _prenorm(x, g, scale, shift, ts=256):
    b, s, d = x.shape
    return pl.pallas_call(
        _prenorm_kernel,
        grid=(b, s // ts),
        in_specs=[
            pl.BlockSpec((None, ts, d), lambda i, j: (i, j, 0)),
            pl.BlockSpec((1, d), lambda i, j: (0, 0)),
            pl.BlockSpec((None, 1, d), lambda i, j: (i, 0, 0)),
            pl.BlockSpec((None, 1, d), lambda i, j: (i, 0, 0)),
        ],
        out_specs=pl.BlockSpec((None, ts, d), lambda i, j: (i, j, 0)),
        out_shape=jax.ShapeDtypeStruct((b, s, d), BF16),
        compiler_params=_params(("parallel", "parallel")),
        name="prenorm",
    )(x, g.reshape(1, d), scale, shift)


def _post_kernel(y_ref, x_ref, gate_ref, gpost_ref, gnext_ref, sc_ref, sh_ref, *rest,
                 emit_h, route):
    if route:
        wr_ref, rest = rest[0], rest[1:]
    xo_ref, rest = rest[0], rest[1:]
    x_new = x_ref[...] + gate_ref[...] * (_rms(y_ref[...]) * gpost_ref[...])
    xo_ref[...] = x_new
    if not emit_h:
        return
    h = (_rms(x_new) * gnext_ref[...]) * (1.0 + sc_ref[...]) + sh_ref[...]
    if not route:
        rest[0][...] = h.astype(BF16)
        return
    hf_ref, comb_ref, sel_ref = rest
    hf_ref[...] = h
    hi, lo = _split_bf16(h)
    w = wr_ref[...]
    whi, wlo = _split_bf16(w)
    logits = _dot(hi, whi) + _dot(lo, whi) + _dot(hi, wlo)
    n_exp = route
    lane = lax.broadcasted_iota(jnp.int32, logits.shape, 1)
    lg = jnp.where(lane < n_exp, logits, NEG_INF)
    m1 = jnp.max(lg, axis=1, keepdims=True)
    i1 = jnp.min(jnp.where(lg == m1, lane, LANES), axis=1, keepdims=True)
    lg2 = jnp.where(lane == i1, NEG_INF, lg)
    m2 = jnp.max(lg2, axis=1, keepdims=True)
    i2 = jnp.min(jnp.where(lg2 == m2, lane, LANES), axis=1, keepdims=True)
    e2 = jnp.exp(m2 - m1)
    w1 = 1.0 / (1.0 + e2)
    w2 = e2 / (1.0 + e2)
    comb_ref[...] = jnp.where(lane == i1, w1, 0.0) + jnp.where(lane == i2, w2, 0.0)
    sel_ref[...] = jnp.where((lane == i1) | (lane == i2), 1.0, 0.0)


def _post(y, x, gate, g_post, g_next=None, scale=None, shift=None, w_router=None, ts=256):
    b, s, d = x.shape
    emit_h = g_next is not None
    n_exp = 0 if w_router is None else w_router.shape[1]
    if not emit_h:
        g_next, scale, shift = g_post, gate, gate
    row = pl.BlockSpec((None, ts, d), lambda i, j: (i, j, 0))
    vec = pl.BlockSpec((1, d), lambda i, j: (0, 0))
    per_b = pl.BlockSpec((None, 1, d), lambda i, j: (i, 0, 0))
    in_specs = [row, row, per_b, vec, vec, per_b, per_b]
    args = [y.reshape(b, s, d), x, gate, g_post.reshape(1, d), g_next.reshape(1, d), scale, shift]
    out_specs = [row]
    out_shape = [jax.ShapeDtypeStruct((b, s, d), F32)]
    if n_exp:
        wr = jnp.zeros((d, LANES), F32).at[:, :n_exp].set(w_router.astype(F32))
        in_specs.append(pl.BlockSpec((d, LANES), lambda i, j: (0, 0)))
        args.append(wr)
        lane_blk = pl.BlockSpec((None, ts, LANES), lambda i, j: (i, j, 0))
        out_specs += [row, lane_blk, lane_blk]
        out_shape += [jax.ShapeDtypeStruct((b, s, d), F32),
                      jax.ShapeDtypeStruct((b, s, LANES), F32),
                      jax.ShapeDtypeStruct((b, s, LANES), F32)]
    elif emit_h:
        out_specs.append(row)
        out_shape.append(jax.ShapeDtypeStruct((b, s, d), BF16))
    return pl.pallas_call(
        functools.partial(_post_kernel, emit_h=emit_h, route=n_exp),
        grid=(b, s // ts),
        in_specs=in_specs,
        out_specs=out_specs,
        out_shape=out_shape,
        compiler_params=_params(("parallel", "parallel")),
        name="post_norm_residual",
    )(*args)


def _mm_kernel(a_ref, w_ref, o_ref):
    o_ref[...] = _dot(a_ref[...], w_ref[...].astype(BF16)).astype(o_ref.dtype)


def _matmul(a, w, layer, out_dtype, tm=1024, tn=512):
    m, k = a.shape
    n = w.shape[2]
    tm, tn = _col_tile(m, tm), _col_tile(n, tn)
    return pl.pallas_call(
        _mm_kernel,
        grid=(n // tn, m // tm),
        in_specs=[
            pl.BlockSpec((tm, k), lambda j, i: (i, 0)),
            pl.BlockSpec((None, k, tn), lambda j, i: (layer, 0, j)),
        ],
        out_specs=pl.BlockSpec((tm, tn), lambda j, i: (i, j)),
        out_shape=jax.ShapeDtypeStruct((m, n), out_dtype),
        compiler_params=_params(("parallel", "parallel")),
        name="matmul",
    )(a, w)


def _out_proj_kernel(oa_ref, ob_ref, oc_ref, w_ref, y_ref, *, cuts):
    c1, c2 = cuts
    w = w_ref[...].astype(BF16)
    y = _dot(oa_ref[...], w[:c1]) + _dot(ob_ref[...], w[c1:c2]) + _dot(oc_ref[...], w[c2:])
    y_ref[...] = y


def _out_proj(oa, ob, oc, w, layer, tm=1024, tn=512):
    m = oa.shape[0]
    k, n = w.shape[1], w.shape[2]
    c1 = oa.shape[1]
    c2 = c1 + ob.shape[1]
    tm, tn = _col_tile(m, tm), _col_tile(n, tn)
    return pl.pallas_call(
        functools.partial(_out_proj_kernel, cuts=(c1, c2)),
        grid=(n // tn, m // tm),
        in_specs=[
            pl.BlockSpec((tm, oa.shape[1]), lambda j, i: (i, 0)),
            pl.BlockSpec((tm, ob.shape[1]), lambda j, i: (i, 0)),
            pl.BlockSpec((tm, oc.shape[1]), lambda j, i: (i, 0)),
            pl.BlockSpec((None, k, tn), lambda j, i: (layer, 0, j)),
        ],
        out_specs=pl.BlockSpec((tm, tn), lambda j, i: (i, j)),
        out_shape=jax.ShapeDtypeStruct((m, n), F32),
        compiler_params=_params(("parallel", "parallel")),
        name="out_proj",
    )(oa, ob, oc, w)


def _swiglu_up_kernel(a_ref, wg_ref, wu_ref, o_ref):
    a = a_ref[...]
    g = _dot(a, wg_ref[...].astype(BF16))
    u = _dot(a, wu_ref[...].astype(BF16))
    o_ref[...] = (g * jax.nn.sigmoid(g) * u).astype(o_ref.dtype)


def _swiglu_up(a, w_gate, w_up, layer, tm=1024, tn=256):
    m, k = a.shape
    n = w_gate.shape[2]
    tm, tn = _col_tile(m, tm), _col_tile(n, tn)
    wspec = pl.BlockSpec((None, k, tn), lambda j, i: (layer, 0, j))
    return pl.pallas_call(
        _swiglu_up_kernel,
        grid=(n // tn, m // tm),
        in_specs=[pl.BlockSpec((tm, k), lambda j, i: (i, 0)), wspec, wspec],
        out_specs=pl.BlockSpec((tm, tn), lambda j, i: (i, j)),
        out_shape=jax.ShapeDtypeStruct((m, n), BF16),
        compiler_params=_params(("parallel", "parallel")),
        name="swiglu_up",
    )(a, w_gate, w_up)


def _mm_acc_kernel(a_ref, w_ref, o_ref):
    kk = pl.program_id(2)
    p = _dot(a_ref[...], w_ref[...].astype(BF16))

    @pl.when(kk == 0)
    def _():
        o_ref[...] = p

    @pl.when(kk > 0)
    def _():
        o_ref[...] += p


def _matmul_ktiled(a, w, layer, tm=2048, tn=1024, tk=1024):
    m, k = a.shape
    n = w.shape[2]
    tm, tn, tk = _col_tile(m, tm), _col_tile(n, tn), _col_tile(k, tk)
    return pl.pallas_call(
        _mm_acc_kernel,
        grid=(n // tn, m // tm, k // tk),
        in_specs=[
            pl.BlockSpec((tm, tk), lambda j, i, kk: (i, kk)),
            pl.BlockSpec((None, tk, tn), lambda j, i, kk: (layer, kk, j)),
        ],
        out_specs=pl.BlockSpec((tm, tn), lambda j, i, kk: (i, j)),
        out_shape=jax.ShapeDtypeStruct((m, n), F32),
        compiler_params=_params(("parallel", "parallel", "arbitrary")),
        name="matmul_ktiled",
    )(a, w)


def _rel_bucket(dist):
    n = jnp.maximum(dist, 0)
    nf = jnp.maximum(n, 1).astype(F32)
    large = REL_MAX_EXACT + (jnp.log(nf / REL_MAX_EXACT) / math.log(REL_MAX_DISTANCE / REL_MAX_EXACT)
                             * (REL_BUCKETS - REL_MAX_EXACT)).astype(jnp.int32)
    return jnp.where(n < REL_MAX_EXACT, n, jnp.minimum(large, REL_BUCKETS - 1))


def _head_norm(o, g):
    return (_rms(o) * g).astype(BF16)


def _head_cols(h):
    return slice(h * HEAD_DIM, (h + 1) * HEAD_DIM)


def _stack_heads(fn, heads):
    return jnp.concatenate([fn(h) for h in range(heads)], axis=0)


def _lane_heads(fn, heads):
    return jnp.concatenate([fn(h) for h in range(heads)], axis=1)


def _store_heads_transposed(o_ref, y_t, g_ref, heads, rows):
    for h in range(heads):
        y = y_t[:, h * rows:(h + 1) * rows].T * g_ref[:, _head_cols(h)]
        o_ref[:, _head_cols(h)] = y.astype(o_ref.dtype)


def _moba_kernel(q_ref, k_ref, v_ref, bias_ref, g_ref, o_ref, kmean_ref, m_ref, l_ref, acc_ref,
                 *, nb, hb, scale):
    i = pl.program_id(2)
    blk = MOBA_BLOCK

    @pl.when(i == 0)
    def _():
        kf = k_ref[...].astype(F32).reshape(nb, blk, hb * HEAD_DIM)
        kmean_ref[...] = jnp.mean(kf, axis=1)

    khi, klo = _split_bf16(kmean_ref[...])

    def gate_of(h):
        q = q_ref[:, _head_cols(h)]
        return _nt_dot(khi[:, _head_cols(h)], q) + _nt_dot(klo[:, _head_cols(h)], q)

    gate = _lane_heads(gate_of, hb)
    blk_id = lax.broadcasted_iota(jnp.int32, gate.shape, 0)
    gate = jnp.where(blk_id < i, gate, NEG_INF)
    sel = jnp.zeros(gate.shape, F32)
    for _ in range(min(MOBA_TOPK, nb)):
        m = jnp.max(gate, axis=0, keepdims=True)
        first = jnp.min(jnp.where((gate == m) & (m > NEG_INF), blk_id, nb), axis=0, keepdims=True)
        pick = blk_id == first
        sel = jnp.where(pick, 1.0, sel)
        gate = jnp.where(pick, NEG_INF, gate)

    def scores(rows, bias_idx):
        return _lane_heads(
            lambda h: _nt_dot(k_ref[rows, _head_cols(h)], q_ref[:, _head_cols(h)]) * scale
            + bias_ref[h, bias_idx], hb)

    def weighted_values(p, rows):
        pb = p.astype(BF16)
        return _lane_heads(
            lambda h: _tn_dot(v_ref[rows, _head_cols(h)], pb[:, h * blk:(h + 1) * blk]), hb)

    key = lax.broadcasted_iota(jnp.int32, (blk, hb * blk), 0)
    qry = lax.broadcasted_iota(jnp.int32, (blk, hb * blk), 1) & (blk - 1)
    own = pl.ds(pl.multiple_of(i * blk, blk), blk)
    s = jnp.where(key <= qry, scores(own, 0), NEG_INF)
    m0 = jnp.max(s, axis=0, keepdims=True)
    p = jnp.exp(s - m0)
    m_ref[...] = m0
    l_ref[...] = jnp.sum(p, axis=0, keepdims=True)
    acc_ref[...] = weighted_values(p, own)

    for n in range(nb - 1):
        @pl.when(n < i)
        def _(n=n):
            rows = slice(n * blk, (n + 1) * blk)
            sn = jnp.where(sel[n:n + 1, :] > 0.0, scores(rows, jnp.minimum(i - n, 2)), NEG_INF)
            m_old = m_ref[...]
            m_new = jnp.maximum(m_old, jnp.max(sn, axis=0, keepdims=True))
            alpha = jnp.exp(m_old - m_new)
            pn = jnp.exp(sn - m_new)
            m_ref[...] = m_new
            l_ref[...] = alpha * l_ref[...] + jnp.sum(pn, axis=0, keepdims=True)
            acc_ref[...] = alpha * acc_ref[...] + weighted_values(pn, rows)

    o_t = acc_ref[...] / l_ref[...]
    y_t = o_t * lax.rsqrt(jnp.mean(o_t * o_t, axis=0, keepdims=True) + NORM_EPS)
    _store_heads_transposed(o_ref, y_t, g_ref, hb, blk)


def _moba(proj, bias, g, heads, q0, k0, v0):
    b, s, _ = proj.shape
    blk = MOBA_BLOCK
    nb = s // blk
    hb = math.gcd(4, heads, q0, k0, v0)
    wide = hb * HEAD_DIM
    bias = jnp.swapaxes(bias, -1, -2)
    seq = lambda c0: pl.BlockSpec((None, s, wide), lambda bi, h, i: (bi, 0, c0 // hb + h))
    return pl.pallas_call(
        functools.partial(_moba_kernel, nb=nb, hb=hb, scale=HEAD_DIM ** -0.5),
        grid=(b, heads // hb, nb),
        in_specs=[
            pl.BlockSpec((None, blk, wide), lambda bi, h, i: (bi, i, q0 // hb + h)),
            seq(k0), seq(v0),
            pl.BlockSpec((hb, 3, blk, blk), lambda bi, h, i: (h, 0, 0, 0)),
            pl.BlockSpec((1, wide), lambda bi, h, i: (0, h)),
        ],
        out_specs=pl.BlockSpec((None, blk, wide), lambda bi, h, i: (bi, i, h)),
        out_shape=jax.ShapeDtypeStruct((b, s, heads * HEAD_DIM), BF16),
        scratch_shapes=[pltpu.VMEM((nb, wide), F32), pltpu.VMEM((1, hb * blk), F32),
                        pltpu.VMEM((1, hb * blk), F32), pltpu.VMEM((HEAD_DIM, hb * blk), F32)],
        compiler_params=_params(("parallel", "parallel", "arbitrary")),
        name="moba_attention",
    )(proj, proj, proj, bias, g.reshape(1, -1))


def _sb_kernel(q_ref, k_ref, v_ref, g_ref, o_ref, acc_ref, *, hb, scale):
    i = pl.program_id(2)
    blk = SB_BLOCK
    key = lax.broadcasted_iota(jnp.int32, (blk, hb * blk), 0)
    qry = lax.broadcasted_iota(jnp.int32, (blk, hb * blk), 1) & (blk - 1)
    rs = lax.broadcasted_iota(jnp.int32, (blk, blk), 0)
    cs = lax.broadcasted_iota(jnp.int32, (blk, blk), 1)
    suffix = jnp.where(cs >= rs, 1.0, 0.0).astype(BF16)

    def block(rows, carry, diagonal):
        z = _lane_heads(
            lambda h: _nt_dot(k_ref[rows, _head_cols(h)], q_ref[:, _head_cols(h)]), hb) * scale
        sp = jnp.maximum(z, 0.0) + jnp.log(1.0 + jnp.exp2(jnp.abs(z) * -LOG2E))
        if diagonal:
            strict = key < qry
            sp = jnp.where(strict, sp, 0.0)
        hi, lo = _split_bf16(sp)
        tot = _dot(suffix, hi) + _dot(suffix, lo) + carry
        a = jnp.exp(z - tot)
        if diagonal:
            a = jnp.where(strict, a, 0.0)
        a = a.astype(BF16)
        av = _lane_heads(
            lambda h: _tn_dot(v_ref[rows, _head_cols(h)], a[:, h * blk:(h + 1) * blk]), hb)
        if diagonal:
            acc_ref[...] = av
        else:
            acc_ref[...] += av
        return tot[0:1, :]

    own = pl.ds(pl.multiple_of(i * blk, blk), blk)
    carry = block(own, jnp.zeros((1, hb * blk), F32), True)

    def body(step, carry):
        rows = pl.ds(pl.multiple_of((i - 1 - step) * blk, blk), blk)
        return block(rows, carry, False)

    lax.fori_loop(0, i, body, carry)
    acc = acc_ref[...]
    y_t = acc * lax.rsqrt(jnp.mean(acc * acc, axis=0, keepdims=True) + NORM_EPS)
    _store_heads_transposed(o_ref, y_t, g_ref, hb, blk)


def _stick_breaking(proj, g, heads, q0, k0, v0):
    b, s, _ = proj.shape
    blk = SB_BLOCK
    hb = math.gcd(4, heads, q0, k0, v0)
    wide = hb * HEAD_DIM
    seq = lambda c0: pl.BlockSpec((None, s, wide), lambda bi, h, i: (bi, 0, c0 // hb + h))
    return pl.pallas_call(
        functools.partial(_sb_kernel, hb=hb, scale=HEAD_DIM ** -0.5),
        grid=(b, heads // hb, s // blk),
        in_specs=[
            pl.BlockSpec((None, blk, wide), lambda bi, h, i: (bi, i, q0 // hb + h)),
            seq(k0), seq(v0),
            pl.BlockSpec((1, wide), lambda bi, h, i: (0, h)),
        ],
        out_specs=pl.BlockSpec((None, blk, wide), lambda bi, h, i: (bi, i, h)),
        out_shape=jax.ShapeDtypeStruct((b, s, heads * HEAD_DIM), BF16),
        scratch_shapes=[pltpu.VMEM((HEAD_DIM, hb * blk), F32)],
        compiler_params=_params(("parallel", "parallel", "parallel")),
        name="stick_breaking_attention",
    )(proj, proj, proj, g.reshape(1, -1))


def _swa_kernel(q_ref, kp_ref, kc_ref, vp_ref, vc_ref, bias_ref, sink_ref, g_ref, o_ref,
                *, group, scale):
    n = pl.program_id(2)
    w = SWA_WINDOW
    kk = jnp.concatenate([kp_ref[...], kc_ref[...]], axis=0)
    vv = jnp.concatenate([vp_ref[...], vc_ref[...]], axis=0)
    key = lax.broadcasted_iota(jnp.int32, (2 * w, group * w), 0)
    qry = lax.broadcasted_iota(jnp.int32, (2 * w, group * w), 1) & (w - 1)
    dist = qry + w - key
    mask = (dist >= 0) & (dist < w) & (key + n * w >= w)
    q = _stack_heads(lambda h: q_ref[:, _head_cols(h)], group)
    s = jnp.where(mask, _nt_dot(kk, q) * scale + bias_ref[...], NEG_INF)
    sink = sink_ref[...]
    m = jnp.maximum(jnp.max(s, axis=0, keepdims=True), sink)
    e = jnp.exp(s - m)
    denom = jnp.sum(e, axis=0, keepdims=True) + jnp.exp(sink - m)
    o_t = _tn_dot(vv, e.astype(BF16)) / denom
    y_t = o_t * lax.rsqrt(jnp.mean(o_t * o_t, axis=0, keepdims=True) + NORM_EPS)
    for h in range(group):
        y = y_t[:, h * w:(h + 1) * w].T * g_ref[:, _head_cols(h)]
        o_ref[:, _head_cols(h)] = y.astype(BF16)


def _swa(proj, bias, sinks, g, q_heads, kv_heads, q0, k0, v0):
    b, s, _ = proj.shape
    w = SWA_WINDOW
    group = q_heads // kv_heads
    bias = bias.reshape(kv_heads, group, w, 2 * w).transpose(0, 3, 1, 2).reshape(
        kv_heads, 2 * w, group * w)
    sink_rows = jnp.broadcast_to(sinks.astype(F32).reshape(kv_heads, 1, group, 1),
                                 (kv_heads, 1, group, w)).reshape(kv_heads, 1, group * w)
    kv = lambda c0, prev: pl.BlockSpec(
        (None, w, HEAD_DIM),
        lambda bi, kh, n: (bi, jnp.maximum(n - 1, 0) if prev else n, c0 + kh))
    return pl.pallas_call(
        functools.partial(_swa_kernel, group=group, scale=HEAD_DIM ** -0.5),
        grid=(b, kv_heads, s // w),
        in_specs=[
            pl.BlockSpec((None, w, group * HEAD_DIM), lambda bi, kh, n: (bi, n, q0 // group + kh)),
            kv(k0, True), kv(k0, False), kv(v0, True), kv(v0, False),
            pl.BlockSpec((None, 2 * w, group * w), lambda bi, kh, n: (kh, 0, 0)),
            pl.BlockSpec((None, 1, group * w), lambda bi, kh, n: (kh, 0, 0)),
            pl.BlockSpec((1, group * HEAD_DIM), lambda bi, kh, n: (0, kh)),
        ],
        out_specs=pl.BlockSpec((None, w, group * HEAD_DIM), lambda bi, kh, n: (bi, n, kh)),
        out_shape=jax.ShapeDtypeStruct((b, s, q_heads * HEAD_DIM), BF16),
        compiler_params=_params(("parallel", "parallel", "parallel")),
        name="swa_sink_attention",
    )(proj, proj, proj, proj, proj, bias, sink_rows, g.reshape(1, -1))


def _toeplitz(v, rows, cols):
    heads, length = v.shape
    t = jnp.tile(v, (1, rows))[:, :rows * (length - 1)].reshape(heads, rows, length - 1)
    return t[:, :, :cols]


def _bias_tiles(rel_bias, moba_heads, swa_heads):
    blk, w = MOBA_BLOCK, SWA_WINDOW
    length = 2 * blk
    k = jnp.arange(length)
    bd = rel_bias[_rel_bucket(k)].astype(F32).T
    mb, sw = bd[:moba_heads], bd[moba_heads:moba_heads + swa_heads]
    own = _toeplitz(mb[:, (-k) % length], blk, blk)
    adj = _toeplitz(mb[:, (blk - k) % length], blk, blk)
    far = jnp.broadcast_to(mb[:, length - 1][:, None, None], own.shape)
    swa = _toeplitz(sw[:, (w - k) % length], w, 2 * w)
    return jnp.stack([own, adj, far], axis=1), swa


def _head_counts(w_in, g_moba, g_sb, g_swa):
    moba_w, sb_w, swa_w = g_moba.shape[1], g_sb.shape[1], g_swa.shape[1]
    kv_w = (w_in.shape[2] - 3 * moba_w - 3 * sb_w - swa_w) // 2
    return moba_w // HEAD_DIM, sb_w // HEAD_DIM, swa_w // HEAD_DIM, kv_w // HEAD_DIM


def _mixing(h, bias_tiles, w_in, w_out, g_moba, g_sb, g_swa, sinks, layer):
    b, s, d = h.shape
    in_w = w_in.shape[2]
    mh, sh, qh, kvh = _head_counts(w_in, g_moba, g_sb, g_swa)
    assert qh % kvh == 0 and (3 * mh + 3 * sh) % (qh // kvh) == 0
    proj = _matmul(h.reshape(b * s, d), w_in, layer, BF16).reshape(b, s, in_w)
    o_a = _moba(proj, bias_tiles[0], g_moba[layer], mh, 0, mh, 2 * mh)
    o_b = _stick_breaking(proj, g_sb[layer], sh, 3 * mh, 3 * mh + sh, 3 * mh + 2 * sh)
    c0 = 3 * mh + 3 * sh
    o_c = _swa(proj, bias_tiles[1], sinks[layer], g_swa[layer], qh, kvh, c0, c0 + qh, c0 + qh + kvh)
    m = b * s
    return _out_proj(o_a.reshape(m, -1), o_b.reshape(m, -1), o_c.reshape(m, -1), w_out, layer)


def _row_gather_pipeline(fetch_rows, wait_rows, compute):
    i = pl.program_id(0)
    n = pl.num_programs(0)

    @pl.when(i == 0)
    def _():
        fetch_rows(0, 0)

    @pl.when(i + 1 < n)
    def _():
        fetch_rows(i + 1, (i + 1) % 2)

    wait_rows(i % 2)
    compute(i % 2)


def _gather_kernel(tok_ref, src_ref, o_ref, buf_ref, sem_ref, *, rows):
    def copy(step, slot, r):
        return pltpu.make_async_copy(src_ref.at[pl.ds(tok_ref[step * rows + r], 1)],
                                     buf_ref.at[slot, pl.ds(r, 1)], sem_ref.at[slot])

    def fetch_rows(step, slot):
        def start(r, c):
            copy(step, slot, r).start()
            return c
        lax.fori_loop(0, rows, start, 0, unroll=8)

    def wait_rows(slot):
        def wait(r, c):
            copy(0, slot, r).wait()
            return c
        lax.fori_loop(0, rows, wait, 0, unroll=8)

    def compute(slot):
        o_ref[...] = buf_ref[slot].astype(o_ref.dtype)

    _row_gather_pipeline(fetch_rows, wait_rows, compute)


def _gather_rows(src, token_of, rows=256):
    p = token_of.shape[0]
    d = src.shape[1]
    return pl.pallas_call(
        functools.partial(_gather_kernel, rows=rows),
        grid_spec=pltpu.PrefetchScalarGridSpec(
            num_scalar_prefetch=1,
            grid=(p // rows,),
            in_specs=[pl.BlockSpec(memory_space=pl.ANY)],
            out_specs=pl.BlockSpec((rows, d), lambda i, tok: (i, 0)),
            scratch_shapes=[pltpu.VMEM((2, rows, d), F32), pltpu.SemaphoreType.DMA((2,))],
        ),
        out_shape=jax.ShapeDtypeStruct((p, d), BF16),
        compiler_params=_params(("arbitrary",)),
        name="moe_gather",
    )(token_of, src)


def _gmm_kernel(tile_ref, jn_ref, exp_ref, otile_ref, ojn_ref, live_ref, a_ref, *rest, dual):
    s = pl.program_id(0)

    @pl.when(live_ref[s] == 0)
    def _():
        rest[-1][...] = jnp.zeros_like(rest[-1])

    @pl.when(live_ref[s] > 0)
    def _():
        a = a_ref[...]
        if dual:
            wg_ref, wu_ref, o_ref = rest
            g = _dot(a, wg_ref[...].astype(BF16))
            u = _dot(a, wu_ref[...].astype(BF16))
            o_ref[...] = (g * jax.nn.sigmoid(g) * u).astype(o_ref.dtype)
        else:
            w_ref, o_ref = rest
            o_ref[...] = _dot(a, w_ref[...].astype(BF16)).astype(o_ref.dtype)


def _grouped_matmul(a, weights, layer, sched, out_dtype, tn):
    p, k = a.shape
    n = weights[0].shape[3]
    tm = MOE_ROW_TILE
    steps = sched[0].shape[0]
    wspec = pl.BlockSpec((None, None, k, tn),
                         lambda s, t, j, e, ot, oj, v: (layer, e[s], 0, j[s]))
    return pl.pallas_call(
        functools.partial(_gmm_kernel, dual=len(weights) == 2),
        grid_spec=pltpu.PrefetchScalarGridSpec(
            num_scalar_prefetch=6,
            grid=(steps,),
            in_specs=[pl.BlockSpec((tm, k), lambda s, t, j, e, ot, oj, v: (t[s], 0))]
            + [wspec] * len(weights),
            out_specs=pl.BlockSpec((tm, tn), lambda s, t, j, e, ot, oj, v: (ot[s], oj[s])),
        ),
        out_shape=jax.ShapeDtypeStruct((p, n), out_dtype),
        compiler_params=_params(("arbitrary",)),
        name="moe_grouped_matmul",
    )(*sched, a, *weights)


def _combine_kernel(pos_ref, src_ref, w_ref, o_ref, buf_ref, sem_ref, *, rows, tokens):
    def copy(step, slot, r, choice):
        return pltpu.make_async_copy(
            src_ref.at[pl.ds(pos_ref[choice * tokens + step * rows + r], 1)],
            buf_ref.at[slot, choice, pl.ds(r, 1)], sem_ref.at[slot])

    def fetch_rows(step, slot):
        def start(r, c):
            for choice in range(TOP_K):
                copy(step, slot, r, choice).start()
            return c
        lax.fori_loop(0, rows, start, 0, unroll=4)

    def wait_rows(slot):
        def wait(r, c):
            for choice in range(TOP_K):
                copy(0, slot, r, choice).wait()
            return c
        lax.fori_loop(0, rows, wait, 0, unroll=4)

    def compute(slot):
        wts = w_ref[...]
        o_ref[...] = wts[:, 0:1] * buf_ref[slot, 0] + wts[:, 1:2] * buf_ref[slot, 1]

    _row_gather_pipeline(fetch_rows, wait_rows, compute)


def _combine(expert_out, pos, wts, rows=256):
    t = pos.shape[0] // TOP_K
    d = expert_out.shape[1]
    return pl.pallas_call(
        functools.partial(_combine_kernel, rows=rows, tokens=t),
        grid_spec=pltpu.PrefetchScalarGridSpec(
            num_scalar_prefetch=1,
            grid=(t // rows,),
            in_specs=[pl.BlockSpec(memory_space=pl.ANY),
                      pl.BlockSpec((rows, LANES), lambda i, pos: (i, 0))],
            out_specs=pl.BlockSpec((rows, d), lambda i, pos: (i, 0)),
            scratch_shapes=[pltpu.VMEM((2, TOP_K, rows, d), F32), pltpu.SemaphoreType.DMA((2,))],
        ),
        out_shape=jax.ShapeDtypeStruct((t, d), F32),
        compiler_params=_params(("arbitrary",)),
        name="moe_combine",
    )(pos, expert_out, wts)


def _route_plan(sel, comb, n_exp):
    t = sel.shape[0]
    tm = MOE_ROW_TILE
    n_tiles = (t * TOP_K) // tm + n_exp
    mask = sel > 0.5
    cnt = jnp.sum(mask, axis=0).astype(jnp.int32)
    tiles_e = (cnt + tm - 1) // tm
    tile_end = jnp.cumsum(tiles_e)
    tile_start = tile_end - tiles_e
    rank = jnp.cumsum(mask, axis=0).astype(jnp.int32) - 1
    pos_te = tile_start[None, :] * tm + rank
    order = jnp.argsort(jnp.logical_not(mask), axis=1, stable=True)[:, :TOP_K].astype(jnp.int32)
    pos = jnp.take_along_axis(pos_te, order, axis=1)
    wts = jnp.take_along_axis(comb, order, axis=1)
    token_of = jnp.zeros((n_tiles * tm,), jnp.int32).at[pos.reshape(-1)].set(
        jnp.repeat(jnp.arange(t, dtype=jnp.int32), TOP_K))
    wts_pad = jnp.zeros((t, LANES), F32).at[:, :TOP_K].set(wts)

    def schedule(col_tiles):
        steps = jnp.arange(n_tiles * col_tiles, dtype=jnp.int32)
        live = tile_end[-1] * col_tiles
        last = jnp.maximum(live - 1, 0)
        steps_c = jnp.minimum(steps, last)
        e_of = jnp.minimum(jnp.searchsorted(tile_end * col_tiles, steps_c, side="right"),
                           n_exp - 1).astype(jnp.int32)
        local = steps_c - tile_start[e_of] * col_tiles
        n_e = jnp.maximum(tiles_e[e_of], 1)
        j_of = (local // n_e).astype(jnp.int32)
        t_of = (tile_start[e_of] + local % n_e).astype(jnp.int32)
        n_dead = jnp.maximum(n_tiles - tile_end[-1], 1)
        dead = jnp.maximum(steps - live, 0)
        is_live = steps < live
        ot_of = jnp.where(is_live, t_of, tile_end[-1] + dead % n_dead).astype(jnp.int32)
        oj_of = jnp.where(is_live, j_of, dead // n_dead).astype(jnp.int32)
        return t_of, j_of, e_of, ot_of, oj_of, is_live.astype(jnp.int32)

    return token_of, schedule, pos.T.reshape(-1), wts_pad


def _col_tile(n, pref):
    while n % pref:
        pref //= 2
    return pref


def _moe(h_f32, comb, sel, w_gate, w_up, w_down, layer_idx):
    t, d = h_f32.shape
    n_exp = w_gate.shape[1]
    d_exp = w_gate.shape[3]
    tn_up, tn_down = _col_tile(d_exp, 256), _col_tile(d, 512)
    token_of, schedule, pos, wts = _route_plan(sel[:, :n_exp], comb[:, :n_exp], n_exp)
    hs = _gather_rows(h_f32, token_of)
    act = _grouped_matmul(hs, (w_gate, w_up), layer_idx, schedule(d_exp // tn_up), BF16, tn_up)
    out = _grouped_matmul(act, (w_down,), layer_idx, schedule(d // tn_down), F32, tn_down)
    return _combine(out, pos, wts)


def kernel(x, c, rel_bias, w_ada, b_ada, g_pre_mix, w_in, g_grp_moba, g_grp_sb, g_grp_swa, swa_sinks, w_out, g_post_mix, g_pre_ffn, w_ff_gate, w_ff_up, w_ff_down, w_router, w_moe_gate, w_moe_up, w_moe_down, g_post_ffn):
    b, s, d = x.shape
    depth = w_ada.shape[0]
    m = b * s
    mod = _ada_mod(c, w_ada, b_ada).reshape(depth, b, ADA_CHUNKS, 1, d)
    chunk = lambda layer, idx: mod[layer, :, idx]
    h = _prenorm(x, g_pre_mix[0], chunk(0, 1), chunk(0, 0))
    mh, _, qh, _ = _head_counts(w_in, g_grp_moba, g_grp_sb, g_grp_swa)
    bias_tiles = _bias_tiles(rel_bias, mh, qh)
    for layer in range(depth):
        shift_f, scale_f, gate_f = chunk(layer, 3), chunk(layer, 4), chunk(layer, 5)
        y = _mixing(h, bias_tiles, w_in, w_out, g_grp_moba, g_grp_sb, g_grp_swa, swa_sinks, layer)
        idx = layer // 2
        if layer % 2 == 0:
            x, h = _post(y, x, chunk(layer, 2), g_post_mix[layer], g_pre_ffn[layer], scale_f, shift_f)
            act = _swiglu_up(h.reshape(m, d), w_ff_gate, w_ff_up, idx)
            y = _matmul_ktiled(act, w_ff_down, idx)
        else:
            x, hf, comb, sel = _post(y, x, chunk(layer, 2), g_post_mix[layer], g_pre_ffn[layer],
                                     scale_f, shift_f, w_router=w_router[idx])
            y = _moe(hf.reshape(m, d), comb.reshape(m, LANES), sel.reshape(m, LANES),
                     w_moe_gate, w_moe_up, w_moe_down, idx)
        if layer + 1 < depth:
            x, h = _post(y, x, gate_f, g_post_ffn[layer], g_pre_mix[layer + 1],
                         chunk(layer + 1, 1), chunk(layer + 1, 0))
        else:
            (x,) = _post(y, x, gate_f, g_post_ffn[layer])
    return x
```

```python
import functools
import math

import jax
import jax.numpy as jnp
from jax import lax
from jax.experimental import pallas as pl
from jax.experimental.pallas import tpu as pltpu

HEAD_DIM = 128
MOBA_BLOCK = 256
MOBA_TOPK = 3
SB_BLOCK = 256
SWA_WINDOW = 128
REL_BUCKETS = 32
REL_MAX_EXACT = 16
REL_MAX_DISTANCE = 128
NORM_EPS = 1e-6
ADA_CHUNKS = 6
TOP_K = 2
LANES = 128
MOE_ROW_TILE = 512
VMEM_LIMIT = 56 * 1024 * 1024

F32 = jnp.float32
BF16 = jnp.bfloat16
NEG_INF = float("-inf")
LOG2E = math.log2(math.e)


def _params(sem, vmem=VMEM_LIMIT):
    return pltpu.CompilerParams(dimension_semantics=sem, vmem_limit_bytes=vmem)


def _nt_dot(a, b):
    return lax.dot_general(a, b, (((1,), (1,)), ((), ())), preferred_element_type=F32)


def _dot(a, b):
    return jnp.dot(a, b, preferred_element_type=F32)


def _tn_dot(a, b):
    return lax.dot_general(a, b, (((0,), (0,)), ((), ())), preferred_element_type=F32)


def _split_bf16(x):
    hi = x.astype(BF16)
    lo = (x - hi.astype(F32)).astype(BF16)
    return hi, lo


def _rms(x):
    return x * lax.rsqrt(jnp.mean(x * x, axis=-1, keepdims=True) + NORM_EPS)


def _ada_kernel(c_ref, w_ref, b_ref, o_ref):
    c = c_ref[...]
    sc = c * jax.nn.sigmoid(c)
    hi, lo = _split_bf16(sc)
    w = w_ref[...].astype(BF16)
    o_ref[...] = _dot(hi, w) + _dot(lo, w) + b_ref[...]


def _ada_mod(c, w_ada, b_ada):
    depth, d, n = w_ada.shape
    b = c.shape[0]
    rows = 8
    c_pad = jnp.zeros((rows, d), F32).at[:b].set(c)
    tn = _col_tile(n, 512)
    out = pl.pallas_call(
        _ada_kernel,
        grid=(depth, n // tn),
        in_specs=[
            pl.BlockSpec((rows, d), lambda l, j: (0, 0)),
            pl.BlockSpec((None, d, tn), lambda l, j: (l, 0, j)),
            pl.BlockSpec((None, 1, tn), lambda l, j: (l, 0, j)),
        ],
        out_specs=pl.BlockSpec((None, rows, tn), lambda l, j: (l, 0, j)),
        out_shape=jax.ShapeDtypeStruct((depth, rows, n), F32),
        compiler_params=_params(("parallel", "parallel")),
        name="ada_mod",
    )(c_pad, w_ada, b_ada.reshape(depth, 1, n))
    return out[:, :b]


def _prenorm_kernel(x_ref, g_ref, sc_ref, sh_ref, h_ref):
    x = x_ref[...]
    h = (_rms(x) * g_ref[...]) * (1.0 + sc_ref[...]) + sh_ref[...]
    h_ref[...] = h.astype(h_ref.dtype)


def _prenorm(x, g, scale, shift, ts=256):
    b, s, d = x.shape
    return pl.pallas_call(
        _prenorm_kernel,
        grid=(b, s // ts),
        in_specs=[
            pl.BlockSpec((None, ts, d), lambda i, j: (i, j, 0)),
            pl.BlockSpec((1, d), lambda i, j: (0, 0)),
            pl.BlockSpec((None, 1, d), lambda i, j: (i, 0, 0)),
            pl.BlockSpec((None, 1, d), lambda i, j: (i, 0, 0)),
        ],
        out_specs=pl.BlockSpec((None, ts, d), lambda i, j: (i, j, 0)),
        out_shape=jax.ShapeDtypeStruct((b, s, d), BF16),
        compiler_params=_params(("parallel", "parallel")),
        name="prenorm",
    )(x, g.reshape(1, d), scale, shift)


def _post_kernel(y_ref, x_ref, gate_ref, gpost_ref, gnext_ref, sc_ref, sh_ref, *rest,
                 emit_h, route):
    if route:
        wr_ref, rest = rest[0], rest[1:]
    xo_ref, rest = rest[0], rest[1:]
    x_new = x_ref[...] + gate_ref[...] * (_rms(y_ref[...]) * gpost_ref[...])
    xo_ref[...] = x_new
    if not emit_h:
        return
    h = (_rms(x_new) * gnext_ref[...]) * (1.0 + sc_ref[...]) + sh_ref[...]
    if not route:
        rest[0][...] = h.astype(BF16)
        return
    hf_ref, comb_ref, sel_ref = rest
    hf_ref[...] = h
    hi, lo = _split_bf16(h)
    w = wr_ref[...]
    whi, wlo = _split_bf16(w)
    logits = _dot(hi, whi) + _dot(lo, whi) + _dot(hi, wlo)
    n_exp = route
    lane = lax.broadcasted_iota(jnp.int32, logits.shape, 1)
    lg = jnp.where(lane < n_exp, logits, NEG_INF)
    m1 = jnp.max(lg, axis=1, keepdims=True)
    i1 = jnp.min(jnp.where(lg == m1, lane, LANES), axis=1, keepdims=True)
    lg2 = jnp.where(lane == i1, NEG_INF, lg)
    m2 = jnp.max(lg2, axis=1, keepdims=True)
    i2 = jnp.min(jnp.where(lg2 == m2, lane, LANES), axis=1, keepdims=True)
    e2 = jnp.exp(m2 - m1)
    w1 = 1.0 / (1.0 + e2)
    w2 = e2 / (1.0 + e2)
    comb_ref[...] = jnp.where(lane == i1, w1, 0.0) + jnp.where(lane == i2, w2, 0.0)
    sel_ref[...] = jnp.where((lane == i1) | (lane == i2), 1.0, 0.0)


def _post(y, x, gate, g_post, g_next=None, scale=None, shift=None, w_router=None, ts=256):
    b, s, d = x.shape
    emit_h = g_next is not None
    n_exp = 0 if w_router is None else w_router.shape[1]
    if not emit_h:
        g_next, scale, shift = g_post, gate, gate
    row = pl.BlockSpec((None, ts, d), lambda i, j: (i, j, 0))
    vec = pl.BlockSpec((1, d), lambda i, j: (0, 0))
    per_b = pl.BlockSpec((None, 1, d), lambda i, j: (i, 0, 0))
    in_specs = [row, row, per_b, vec, vec, per_b, per_b]
    args = [y.reshape(b, s, d), x, gate, g_post.reshape(1, d), g_next.reshape(1, d), scale, shift]
    out_specs = [row]
    out_shape = [jax.ShapeDtypeStruct((b, s, d), F32)]
    if n_exp:
        wr = jnp.zeros((d, LANES), F32).at[:, :n_exp].set(w_router.astype(F32))
        in_specs.append(pl.BlockSpec((d, LANES), lambda i, j: (0, 0)))
        args.append(wr)
        lane_blk = pl.BlockSpec((None, ts, LANES), lambda i, j: (i, j, 0))
        out_specs += [row, lane_blk, lane_blk]
        out_shape += [jax.ShapeDtypeStruct((b, s, d), F32),
                      jax.ShapeDtypeStruct((b, s, LANES), F32),
                      jax.ShapeDtypeStruct((b, s, LANES), F32)]
    elif emit_h:
        out_specs.append(row)
        out_shape.append(jax.ShapeDtypeStruct((b, s, d), BF16))
    return pl.pallas_call(
        functools.partial(_post_kernel, emit_h=emit_h, route=n_exp),
        grid=(b, s // ts),
        in_specs=in_specs,
        out_specs=out_specs,
        out_shape=out_shape,
        compiler_params=_params(("parallel", "parallel")),
        name="post_norm_residual",
    )(*args)


def _mm_kernel(a_ref, w_ref, o_ref):
    o_ref[...] = _dot(a_ref[...], w_ref[...].astype(BF16)).astype(o_ref.dtype)


def _matmul(a, w, layer, out_dtype, tm=1024, tn=512):
    m, k = a.shape
    n = w.shape[2]
    tm, tn = _col_tile(m, tm), _col_tile(n, tn)
    return pl.pallas_call(
        _mm_kernel,
        grid=(n // tn, m // tm),
        in_specs=[
            pl.BlockSpec((tm, k), lambda j, i: (i, 0)),
            pl.BlockSpec((None, k, tn), lambda j, i: (layer, 0, j)),
        ],
        out_specs=pl.BlockSpec((tm, tn), lambda j, i: (i, j)),
        out_shape=jax.ShapeDtypeStruct((m, n), out_dtype),
        compiler_params=_params(("parallel", "parallel")),
        name="matmul",
    )(a, w)


def _out_proj_kernel(oa_ref, ob_ref, oc_ref, w_ref, y_ref, *, cuts):
    c1, c2 = cuts
    w = w_ref[...].astype(BF16)
    y = _dot(oa_ref[...], w[:c1]) + _dot(ob_ref[...], w[c1:c2]) + _dot(oc_ref[...], w[c2:])
    y_ref[...] = y


def _out_proj(oa, ob, oc, w, layer, tm=1024, tn=512):
    m = oa.shape[0]
    k, n = w.shape[1], w.shape[2]
    c1 = oa.shape[1]
    c2 = c1 + ob.shape[1]
    tm, tn = _col_tile(m, tm), _col_tile(n, tn)
    return pl.pallas_call(
        functools.partial(_out_proj_kernel, cuts=(c1, c2)),
        grid=(n // tn, m // tm),
        in_specs=[
            pl.BlockSpec((tm, oa.shape[1]), lambda j, i: (i, 0)),
            pl.BlockSpec((tm, ob.shape[1]), lambda j, i: (i, 0)),
            pl.BlockSpec((tm, oc.shape[1]), lambda j, i: (i, 0)),
            pl.BlockSpec((None, k, tn), lambda j, i: (layer, 0, j)),
        ],
        out_specs=pl.BlockSpec((tm, tn), lambda j, i: (i, j)),
        out_shape=jax.ShapeDtypeStruct((m, n), F32),
        compiler_params=_params(("parallel", "parallel")),
        name="out_proj",
    )(oa, ob, oc, w)


def _swiglu_up_kernel(a_ref, wg_ref, wu_ref, o_ref):
    a = a_ref[...]
    g = _dot(a, wg_ref[...].astype(BF16))
    u = _dot(a, wu_ref[...].astype(BF16))
    o_ref[...] = (g * jax.nn.sigmoid(g) * u).astype(o_ref.dtype)


def _swiglu_up(a, w_gate, w_up, layer, tm=1024, tn=256):
    m, k = a.shape
    n = w_gate.shape[2]
    tm, tn = _col_tile(m, tm), _col_tile(n, tn)
    wspec = pl.BlockSpec((None, k, tn), lambda j, i: (layer, 0, j))
    return pl.pallas_call(
        _swiglu_up_kernel,
        grid=(n // tn, m // tm),
        in_specs=[pl.BlockSpec((tm, k), lambda j, i: (i, 0)), wspec, wspec],
        out_specs=pl.BlockSpec((tm, tn), lambda j, i: (i, j)),
        out_shape=jax.ShapeDtypeStruct((m, n), BF16),
        compiler_params=_params(("parallel", "parallel")),
        name="swiglu_up",
    )(a, w_gate, w_up)


def _mm_acc_kernel(a_ref, w_ref, o_ref):
    kk = pl.program_id(2)
    p = _dot(a_ref[...], w_ref[...].astype(BF16))

    @pl.when(kk == 0)
    def _():
        o_ref[...] = p

    @pl.when(kk > 0)
    def _():
        o_ref[...] += p


def _matmul_ktiled(a, w, layer, tm=2048, tn=1024, tk=1024):
    m, k = a.shape
    n = w.shape[2]
    tm, tn, tk = _col_tile(m, tm), _col_tile(n, tn), _col_tile(k, tk)
    return pl.pallas_call(
        _mm_acc_kernel,
        grid=(n // tn, m // tm, k // tk),
        in_specs=[
            pl.BlockSpec((tm, tk), lambda j, i, kk: (i, kk)),
            pl.BlockSpec((None, tk, tn), lambda j, i, kk: (layer, kk, j)),
        ],
        out_specs=pl.BlockSpec((tm, tn), lambda j, i, kk: (i, j)),
        out_shape=jax.ShapeDtypeStruct((m, n), F32),
        compiler_params=_params(("parallel", "parallel", "arbitrary")),
        name="matmul_ktiled",
    )(a, w)


def _rel_bucket(dist):
    n = jnp.maximum(dist, 0)
    nf = jnp.maximum(n, 1).astype(F32)
    large = REL_MAX_EXACT + (jnp.log(nf / REL_MAX_EXACT) / math.log(REL_MAX_DISTANCE / REL_MAX_EXACT)
                             * (REL_BUCKETS - REL_MAX_EXACT)).astype(jnp.int32)
    return jnp.where(n < REL_MAX_EXACT, n, jnp.minimum(large, REL_BUCKETS - 1))


def _head_norm(o, g):
    return (_rms(o) * g).astype(BF16)


def _head_cols(h):
    return slice(h * HEAD_DIM, (h + 1) * HEAD_DIM)


def _stack_heads(fn, heads):
    return jnp.concatenate([fn(h) for h in range(heads)], axis=0)


def _lane_heads(fn, heads):
    return jnp.concatenate([fn(h) for h in range(heads)], axis=1)


def _store_heads_transposed(o_ref, y_t, g_ref, heads, rows):
    for h in range(heads):
        y = y_t[:, h * rows:(h + 1) * rows].T * g_ref[:, _head_cols(h)]
        o_ref[:, _head_cols(h)] = y.astype(o_ref.dtype)


def _moba_kernel(q_ref, k_ref, v_ref, bias_ref, g_ref, o_ref, kmean_ref, m_ref, l_ref, acc_ref,
                 *, nb, hb, scale):
    i = pl.program_id(2)
    blk = MOBA_BLOCK

    @pl.when(i == 0)
    def _():
        kf = k_ref[...].astype(F32).reshape(nb, blk, hb * HEAD_DIM)
        kmean_ref[...] = jnp.mean(kf, axis=1)

    khi, klo = _split_bf16(kmean_ref[...])

    def gate_of(h):
        q = q_ref[:, _head_cols(h)]
        return _nt_dot(khi[:, _head_cols(h)], q) + _nt_dot(klo[:, _head_cols(h)], q)

    gate = _lane_heads(gate_of, hb)
    blk_id = lax.broadcasted_iota(jnp.int32, gate.shape, 0)
    gate = jnp.where(blk_id < i, gate, NEG_INF)
    sel = jnp.zeros(gate.shape, F32)
    for _ in range(min(MOBA_TOPK, nb)):
        m = jnp.max(gate, axis=0, keepdims=True)
        first = jnp.min(jnp.where((gate == m) & (m > NEG_INF), blk_id, nb), axis=0, keepdims=True)
        pick = blk_id == first
        sel = jnp.where(pick, 1.0, sel)
        gate = jnp.where(pick, NEG_INF, gate)

    def scores(rows, bias_idx):
        return _lane_heads(
            lambda h: _nt_dot(k_ref[rows, _head_cols(h)], q_ref[:, _head_cols(h)]) * scale
            + bias_ref[h, bias_idx], hb)

    def weighted_values(p, rows):
        pb = p.astype(BF16)
        return _lane_heads(
            lambda h: _tn_dot(v_ref[rows, _head_cols(h)], pb[:, h * blk:(h + 1) * blk]), hb)

    key = lax.broadcasted_iota(jnp.int32, (blk, hb * blk), 0)
    qry = lax.broadcasted_iota(jnp.int32, (blk, hb * blk), 1) & (blk - 1)
    own = pl.ds(pl.multiple_of(i * blk, blk), blk)
    s = jnp.where(key <= qry, scores(own, 0), NEG_INF)
    m0 = jnp.max(s, axis=0, keepdims=True)
    p = jnp.exp2(s - m0)
    m_ref[...] = m0
    l_ref[...] = jnp.sum(p, axis=0, keepdims=True)
    acc_ref[...] = weighted_values(p, own)

    for n in range(nb - 1):
        @pl.when(n < i)
        def _(n=n):
            rows = slice(n * blk, (n + 1) * blk)
            sn = jnp.where(sel[n:n + 1, :] > 0.0, scores(rows, jnp.minimum(i - n, 2)), NEG_INF)
            m_old = m_ref[...]
            m_new = jnp.maximum(m_old, jnp.max(sn, axis=0, keepdims=True))
            alpha = jnp.exp2(m_old - m_new)
            pn = jnp.exp2(sn - m_new)
            m_ref[...] = m_new
            l_ref[...] = alpha * l_ref[...] + jnp.sum(pn, axis=0, keepdims=True)
            acc_ref[...] = alpha * acc_ref[...] + weighted_values(pn, rows)

    o_t = acc_ref[...] / l_ref[...]
    y_t = o_t * lax.rsqrt(jnp.mean(o_t * o_t, axis=0, keepdims=True) + NORM_EPS)
    _store_heads_transposed(o_ref, y_t, g_ref, hb, blk)


def _moba(proj, bias, g, heads, q0, k0, v0):
    b, s, _ = proj.shape
    blk = MOBA_BLOCK
    nb = s // blk
    hb = math.gcd(4, heads, q0, k0, v0)
    wide = hb * HEAD_DIM
    bias = jnp.swapaxes(bias, -1, -2) * LOG2E
    seq = lambda c0: pl.BlockSpec((None, s, wide), lambda bi, h, i: (bi, 0, c0 // hb + h))
    return pl.pallas_call(
        functools.partial(_moba_kernel, nb=nb, hb=hb, scale=HEAD_DIM ** -0.5 * LOG2E),
        grid=(b, heads // hb, nb),
        in_specs=[
            pl.BlockSpec((None, blk, wide), lambda bi, h, i: (bi, i, q0 // hb + h)),
            seq(k0), seq(v0),
            pl.BlockSpec((hb, 3, blk, blk), lambda bi, h, i: (h, 0, 0, 0)),
            pl.BlockSpec((1, wide), lambda bi, h, i: (0, h)),
        ],
        out_specs=pl.BlockSpec((None, blk, wide), lambda bi, h, i: (bi, i, h)),
        out_shape=jax.ShapeDtypeStruct((b, s, heads * HEAD_DIM), BF16),
        scratch_shapes=[pltpu.VMEM((nb, wide), F32), pltpu.VMEM((1, hb * blk), F32),
                        pltpu.VMEM((1, hb * blk), F32), pltpu.VMEM((HEAD_DIM, hb * blk), F32)],
        compiler_params=_params(("parallel", "parallel", "arbitrary")),
        name="moba_attention",
    )(proj, proj, proj, bias, g.reshape(1, -1))


def _sb_kernel(q_ref, k_ref, v_ref, g_ref, o_ref, acc_ref, *, hb, scale):
    i = pl.program_id(2)
    blk = SB_BLOCK
    key = lax.broadcasted_iota(jnp.int32, (blk, hb * blk), 0)
    qry = lax.broadcasted_iota(jnp.int32, (blk, hb * blk), 1) & (blk - 1)
    rs = lax.broadcasted_iota(jnp.int32, (blk, blk), 0)
    cs = lax.broadcasted_iota(jnp.int32, (blk, blk), 1)
    suffix = jnp.where(cs >= rs, 1.0, 0.0).astype(BF16)
    suffix2 = jnp.concatenate([suffix, suffix], axis=1)

    def block(rows, carry, diagonal):
        z = _lane_heads(
            lambda h: _nt_dot(k_ref[rows, _head_cols(h)], q_ref[:, _head_cols(h)]), hb) * scale
        sp = jnp.maximum(z, 0.0) + jnp.log(1.0 + jnp.exp2(jnp.abs(z) * -LOG2E))
        if diagonal:
            strict = key < qry
            sp = jnp.where(strict, sp, 0.0)
        hi, lo = _split_bf16(sp)
        tot = _dot(suffix2, jnp.concatenate([hi, lo], axis=0)) + carry
        a = jnp.exp(z - tot)
        if diagonal:
            a = jnp.where(strict, a, 0.0)
        a = a.astype(BF16)
        av = _lane_heads(
            lambda h: _tn_dot(v_ref[rows, _head_cols(h)], a[:, h * blk:(h + 1) * blk]), hb)
        if diagonal:
            acc_ref[...] = av
        else:
            acc_ref[...] += av
        return tot[0:1, :]

    own = pl.ds(pl.multiple_of(i * blk, blk), blk)
    carry = block(own, jnp.zeros((1, hb * blk), F32), True)

    def past(j):
        return pl.ds(pl.multiple_of(j * blk, blk), blk)

    odd = i & 1
    carry = lax.fori_loop(0, odd, lambda _, c: block(past(i - 1), c, False), carry)

    def pair(step, c):
        j = i - odd - 1 - 2 * step
        return block(past(j - 1), block(past(j), c, False), False)

    lax.fori_loop(0, i >> 1, pair, carry)
    acc = acc_ref[...]
    y_t = acc * lax.rsqrt(jnp.mean(acc * acc, axis=0, keepdims=True) + NORM_EPS)
    _store_heads_transposed(o_ref, y_t, g_ref, hb, blk)


def _stick_breaking(proj, g, heads, q0, k0, v0):
    b, s, _ = proj.shape
    blk = SB_BLOCK
    hb = math.gcd(4, heads, q0, k0, v0)
    wide = hb * HEAD_DIM
    seq = lambda c0: pl.BlockSpec((None, s, wide), lambda bi, h, i: (bi, 0, c0 // hb + h))
    return pl.pallas_call(
        functools.partial(_sb_kernel, hb=hb, scale=HEAD_DIM ** -0.5),
        grid=(b, heads // hb, s // blk),
        in_specs=[
            pl.BlockSpec((None, blk, wide), lambda bi, h, i: (bi, i, q0 // hb + h)),
            seq(k0), seq(v0),
            pl.BlockSpec((1, wide), lambda bi, h, i: (0, h)),
        ],
        out_specs=pl.BlockSpec((None, blk, wide), lambda bi, h, i: (bi, i, h)),
        out_shape=jax.ShapeDtypeStruct((b, s, heads * HEAD_DIM), BF16),
        scratch_shapes=[pltpu.VMEM((HEAD_DIM, hb * blk), F32)],
        compiler_params=_params(("parallel", "parallel", "parallel")),
        name="stick_breaking_attention",
    )(proj, proj, proj, g.reshape(1, -1))


def _swa_kernel(q_ref, kp_ref, kc_ref, vp_ref, vc_ref, bias_ref, sink_ref, g_ref, o_ref,
                *, group, scale):
    n = pl.program_id(2)
    w = SWA_WINDOW
    kk = jnp.concatenate([kp_ref[...], kc_ref[...]], axis=0)
    vv = jnp.concatenate([vp_ref[...], vc_ref[...]], axis=0)
    key = lax.broadcasted_iota(jnp.int32, (2 * w, group * w), 0)
    qry = lax.broadcasted_iota(jnp.int32, (2 * w, group * w), 1) & (w - 1)
    dist = qry + w - key
    mask = (dist >= 0) & (dist < w) & (key + n * w >= w)
    q = _stack_heads(lambda h: q_ref[:, _head_cols(h)], group)
    s = jnp.where(mask, _nt_dot(kk, q) * scale + bias_ref[...], NEG_INF)
    sink = sink_ref[...]
    m = jnp.maximum(jnp.max(s, axis=0, keepdims=True), sink)
    e = jnp.exp(s - m)
    denom = jnp.sum(e, axis=0, keepdims=True) + jnp.exp(sink - m)
    o_t = _tn_dot(vv, e.astype(BF16)) / denom
    y_t = o_t * lax.rsqrt(jnp.mean(o_t * o_t, axis=0, keepdims=True) + NORM_EPS)
    for h in range(group):
        y = y_t[:, h * w:(h + 1) * w].T * g_ref[:, _head_cols(h)]
        o_ref[:, _head_cols(h)] = y.astype(BF16)


def _swa(proj, bias, sinks, g, q_heads, kv_heads, q0, k0, v0):
    b, s, _ = proj.shape
    w = SWA_WINDOW
    group = q_heads // kv_heads
    bias = bias.reshape(kv_heads, group, w, 2 * w).transpose(0, 3, 1, 2).reshape(
        kv_heads, 2 * w, group * w)
    sink_rows = jnp.broadcast_to(sinks.astype(F32).reshape(kv_heads, 1, group, 1),
                                 (kv_heads, 1, group, w)).reshape(kv_heads, 1, group * w)
    kv = lambda c0, prev: pl.BlockSpec(
        (None, w, HEAD_DIM),
        lambda bi, kh, n: (bi, jnp.maximum(n - 1, 0) if prev else n, c0 + kh))
    return pl.pallas_call(
        functools.partial(_swa_kernel, group=group, scale=HEAD_DIM ** -0.5),
        grid=(b, kv_heads, s // w),
        in_specs=[
            pl.BlockSpec((None, w, group * HEAD_DIM), lambda bi, kh, n: (bi, n, q0 // group + kh)),
            kv(k0, True), kv(k0, False), kv(v0, True), kv(v0, False),
            pl.BlockSpec((None, 2 * w, group * w), lambda bi, kh, n: (kh, 0, 0)),
            pl.BlockSpec((None, 1, group * w), lambda bi, kh, n: (kh, 0, 0)),
            pl.BlockSpec((1, group * HEAD_DIM), lambda bi, kh, n: (0, kh)),
        ],
        out_specs=pl.BlockSpec((None, w, group * HEAD_DIM), lambda bi, kh, n: (bi, n, kh)),
        out_shape=jax.ShapeDtypeStruct((b, s, q_heads * HEAD_DIM), BF16),
        compiler_params=_params(("parallel", "parallel", "parallel")),
        name="swa_sink_attention",
    )(proj, proj, proj, proj, proj, bias, sink_rows, g.reshape(1, -1))


def _toeplitz(v, rows, cols):
    heads, length = v.shape
    t = jnp.tile(v, (1, rows))[:, :rows * (length - 1)].reshape(heads, rows, length - 1)
    return t[:, :, :cols]


def _bias_tiles(rel_bias, moba_heads, swa_heads):
    blk, w = MOBA_BLOCK, SWA_WINDOW
    length = 2 * blk
    k = jnp.arange(length)
    bd = rel_bias[_rel_bucket(k)].astype(F32).T
    mb, sw = bd[:moba_heads], bd[moba_heads:moba_heads + swa_heads]
    own = _toeplitz(mb[:, (-k) % length], blk, blk)
    adj = _toeplitz(mb[:, (blk - k) % length], blk, blk)
    far = jnp.broadcast_to(mb[:, length - 1][:, None, None], own.shape)
    swa = _toeplitz(sw[:, (w - k) % length], w, 2 * w)
    return jnp.stack([own, adj, far], axis=1), swa


def _head_counts(w_in, g_moba, g_sb, g_swa):
    moba_w, sb_w, swa_w = g_moba.shape[1], g_sb.shape[1], g_swa.shape[1]
    kv_w = (w_in.shape[2] - 3 * moba_w - 3 * sb_w - swa_w) // 2
    return moba_w // HEAD_DIM, sb_w // HEAD_DIM, swa_w // HEAD_DIM, kv_w // HEAD_DIM


def _mixing(h, bias_tiles, w_in, w_out, g_moba, g_sb, g_swa, sinks, layer):
    b, s, d = h.shape
    in_w = w_in.shape[2]
    mh, sh, qh, kvh = _head_counts(w_in, g_moba, g_sb, g_swa)
    assert qh % kvh == 0 and (3 * mh + 3 * sh) % (qh // kvh) == 0
    proj = _matmul(h.reshape(b * s, d), w_in, layer, BF16).reshape(b, s, in_w)
    o_a = _moba(proj, bias_tiles[0], g_moba[layer], mh, 0, mh, 2 * mh)
    o_b = _stick_breaking(proj, g_sb[layer], sh, 3 * mh, 3 * mh + sh, 3 * mh + 2 * sh)
    c0 = 3 * mh + 3 * sh
    o_c = _swa(proj, bias_tiles[1], sinks[layer], g_swa[layer], qh, kvh, c0, c0 + qh, c0 + qh + kvh)
    m = b * s
    return _out_proj(o_a.reshape(m, -1), o_b.reshape(m, -1), o_c.reshape(m, -1), w_out, layer)


def _row_gather_pipeline(fetch_rows, wait_rows, compute):
    i = pl.program_id(0)
    n = pl.num_programs(0)

    @pl.when(i == 0)
    def _():
        fetch_rows(0, 0)

    @pl.when(i + 1 < n)
    def _():
        fetch_rows(i + 1, (i + 1) % 2)

    wait_rows(i % 2)
    compute(i % 2)


def _gather_kernel(tok_ref, src_ref, o_ref, buf_ref, sem_ref, *, rows):
    def copy(step, slot, r):
        return pltpu.make_async_copy(src_ref.at[pl.ds(tok_ref[step * rows + r], 1)],
                                     buf_ref.at[slot, pl.ds(r, 1)], sem_ref.at[slot])

    def fetch_rows(step, slot):
        def start(r2, c):
            for prio in range(2):
                copy(step, slot, 2 * r2 + prio).start(priority=prio)
            return c
        lax.fori_loop(0, rows // 2, start, 0, unroll=4)

    def wait_rows(slot):
        def wait(r, c):
            copy(0, slot, r).wait()
            return c
        lax.fori_loop(0, rows, wait, 0, unroll=8)

    def compute(slot):
        o_ref[...] = buf_ref[slot].astype(o_ref.dtype)

    _row_gather_pipeline(fetch_rows, wait_rows, compute)


def _gather_rows(src, token_of, rows=256):
    p = token_of.shape[0]
    d = src.shape[1]
    return pl.pallas_call(
        functools.partial(_gather_kernel, rows=rows),
        grid_spec=pltpu.PrefetchScalarGridSpec(
            num_scalar_prefetch=1,
            grid=(p // rows,),
            in_specs=[pl.BlockSpec(memory_space=pl.ANY)],
            out_specs=pl.BlockSpec((rows, d), lambda i, tok: (i, 0)),
            scratch_shapes=[pltpu.VMEM((2, rows, d), F32), pltpu.SemaphoreType.DMA((2,))],
        ),
        out_shape=jax.ShapeDtypeStruct((p, d), BF16),
        compiler_params=_params(("arbitrary",)),
        name="moe_gather",
    )(token_of, src)


def _gmm_kernel(tile_ref, jn_ref, exp_ref, otile_ref, ojn_ref, live_ref, a_ref, *rest, dual):
    s = pl.program_id(0)

    @pl.when(live_ref[s] == 0)
    def _():
        rest[-1][...] = jnp.zeros_like(rest[-1])

    @pl.when(live_ref[s] > 0)
    def _():
        a = a_ref[...]
        if dual:
            wg_ref, wu_ref, o_ref = rest
            g = _dot(a, wg_ref[...].astype(BF16))
            u = _dot(a, wu_ref[...].astype(BF16))
            o_ref[...] = (g * jax.nn.sigmoid(g) * u).astype(o_ref.dtype)
        else:
            w_ref, o_ref = rest
            o_ref[...] = _dot(a, w_ref[...].astype(BF16)).astype(o_ref.dtype)


def _grouped_matmul(a, weights, layer, sched, out_dtype, tn):
    p, k = a.shape
    n = weights[0].shape[3]
    tm = MOE_ROW_TILE
    steps = sched[0].shape[0]
    wspec = pl.BlockSpec((None, None, k, tn),
                         lambda s, t, j, e, ot, oj, v: (layer, e[s], 0, j[s]))
    return pl.pallas_call(
        functools.partial(_gmm_kernel, dual=len(weights) == 2),
        grid_spec=pltpu.PrefetchScalarGridSpec(
            num_scalar_prefetch=6,
            grid=(steps,),
            in_specs=[pl.BlockSpec((tm, k), lambda s, t, j, e, ot, oj, v: (t[s], 0))]
            + [wspec] * len(weights),
            out_specs=pl.BlockSpec((tm, tn), lambda s, t, j, e, ot, oj, v: (ot[s], oj[s])),
        ),
        out_shape=jax.ShapeDtypeStruct((p, n), out_dtype),
        compiler_params=_params(("arbitrary",)),
        name="moe_grouped_matmul",
    )(*sched, a, *weights)


def _combine_kernel(pos_ref, src_ref, w_ref, o_ref, buf_ref, sem_ref, *, rows, tokens):
    def copy(step, slot, r, choice):
        return pltpu.make_async_copy(
            src_ref.at[pl.ds(pos_ref[choice * tokens + step * rows + r], 1)],
            buf_ref.at[slot, choice, pl.ds(r, 1)], sem_ref.at[slot])

    def fetch_rows(step, slot):
        def start(r, c):
            for choice in range(TOP_K):
                copy(step, slot, r, choice).start(priority=choice % 2)
            return c
        lax.fori_loop(0, rows, start, 0, unroll=4)

    def wait_rows(slot):
        def wait(r, c):
            for choice in range(TOP_K):
                copy(0, slot, r, choice).wait()
            return c
        lax.fori_loop(0, rows, wait, 0, unroll=4)

    def compute(slot):
        wts = w_ref[...]
        o_ref[...] = wts[:, 0:1] * buf_ref[slot, 0] + wts[:, 1:2] * buf_ref[slot, 1]

    _row_gather_pipeline(fetch_rows, wait_rows, compute)


def _combine(expert_out, pos, wts, rows=256):
    t = pos.shape[0] // TOP_K
    d = expert_out.shape[1]
    return pl.pallas_call(
        functools.partial(_combine_kernel, rows=rows, tokens=t),
        grid_spec=pltpu.PrefetchScalarGridSpec(
            num_scalar_prefetch=1,
            grid=(t // rows,),
            in_specs=[pl.BlockSpec(memory_space=pl.ANY),
                      pl.BlockSpec((rows, LANES), lambda i, pos: (i, 0))],
            out_specs=pl.BlockSpec((rows, d), lambda i, pos: (i, 0)),
            scratch_shapes=[pltpu.VMEM((2, TOP_K, rows, d), F32), pltpu.SemaphoreType.DMA((2,))],
        ),
        out_shape=jax.ShapeDtypeStruct((t, d), F32),
        compiler_params=_params(("arbitrary",)),
        name="moe_combine",
    )(pos, expert_out, wts)


def _route_plan(sel, comb, n_exp):
    t = sel.shape[0]
    tm = MOE_ROW_TILE
    n_tiles = (t * TOP_K) // tm + n_exp
    mask = sel > 0.5
    cnt = jnp.sum(mask, axis=0).astype(jnp.int32)
    tiles_e = (cnt + tm - 1) // tm
    tile_end = jnp.cumsum(tiles_e)
    tile_start = tile_end - tiles_e
    rank = jnp.cumsum(mask, axis=0).astype(jnp.int32) - 1
    pos_te = tile_start[None, :] * tm + rank
    order = jnp.argsort(jnp.logical_not(mask), axis=1, stable=True)[:, :TOP_K].astype(jnp.int32)
    pos = jnp.take_along_axis(pos_te, order, axis=1)
    wts = jnp.take_along_axis(comb, order, axis=1)
    token_of = jnp.zeros((n_tiles * tm,), jnp.int32).at[pos.reshape(-1)].set(
        jnp.repeat(jnp.arange(t, dtype=jnp.int32), TOP_K))
    wts_pad = jnp.zeros((t, LANES), F32).at[:, :TOP_K].set(wts)

    def schedule(col_tiles):
        steps = jnp.arange(n_tiles * col_tiles, dtype=jnp.int32)
        live = tile_end[-1] * col_tiles
        last = jnp.maximum(live - 1, 0)
        steps_c = jnp.minimum(steps, last)
        e_of = jnp.minimum(jnp.searchsorted(tile_end * col_tiles, steps_c, side="right"),
                           n_exp - 1).astype(jnp.int32)
        local = steps_c - tile_start[e_of] * col_tiles
        n_e = jnp.maximum(tiles_e[e_of], 1)
        j_of = (local // n_e).astype(jnp.int32)
        t_of = (tile_start[e_of] + local % n_e).astype(jnp.int32)
        n_dead = jnp.maximum(n_tiles - tile_end[-1], 1)
        dead = jnp.maximum(steps - live, 0)
        is_live = steps < live
        ot_of = jnp.where(is_live, t_of, tile_end[-1] + dead % n_dead).astype(jnp.int32)
        oj_of = jnp.where(is_live, j_of, dead // n_dead).astype(jnp.int32)
        return t_of, j_of, e_of, ot_of, oj_of, is_live.astype(jnp.int32)

    return token_of, schedule, pos.T.reshape(-1), wts_pad


def _col_tile(n, pref):
    while n % pref:
        pref //= 2
    return pref


def _moe(h_f32, comb, sel, w_gate, w_up, w_down, layer_idx):
    t, d = h_f32.shape
    n_exp = w_gate.shape[1]
    d_exp = w_gate.shape[3]
    tn_up, tn_down = _col_tile(d_exp, 512), _col_tile(d, 512)
    token_of, schedule, pos, wts = _route_plan(sel[:, :n_exp], comb[:, :n_exp], n_exp)
    hs = _gather_rows(h_f32, token_of)
    act = _grouped_matmul(hs, (w_gate, w_up), layer_idx, schedule(d_exp // tn_up), BF16, tn_up)
    out = _grouped_matmul(act, (w_down,), layer_idx, schedule(d // tn_down), F32, tn_down)
    return _combine(out, pos, wts)


def kernel(x, c, rel_bias, w_ada, b_ada, g_pre_mix, w_in, g_grp_moba, g_grp_sb, g_grp_swa, swa_sinks, w_out, g_post_mix, g_pre_ffn, w_ff_gate, w_ff_up, w_ff_down, w_router, w_moe_gate, w_moe_up, w_moe_down, g_post_ffn):
    b, s, d = x.shape
    depth = w_ada.shape[0]
    m = b * s
    mod = _ada_mod(c, w_ada, b_ada).reshape(depth, b, ADA_CHUNKS, 1, d)
    chunk = lambda layer, idx: mod[layer, :, idx]
    h = _prenorm(x, g_pre_mix[0], chunk(0, 1), chunk(0, 0))
    mh, _, qh, _ = _head_counts(w_in, g_grp_moba, g_grp_sb, g_grp_swa)
    bias_tiles = _bias_tiles(rel_bias, mh, qh)
    for layer in range(depth):
        shift_f, scale_f, gate_f = chunk(layer, 3), chunk(layer, 4), chunk(layer, 5)
        y = _mixing(h, bias_tiles, w_in, w_out, g_grp_moba, g_grp_sb, g_grp_swa, swa_sinks, layer)
        idx = layer // 2
        if layer % 2 == 0:
            x, h = _post(y, x, chunk(layer, 2), g_post_mix[layer], g_pre_ffn[layer], scale_f, shift_f)
            act = _swiglu_up(h.reshape(m, d), w_ff_gate, w_ff_up, idx)
            y = _matmul_ktiled(act, w_ff_down, idx)
        else:
            x, hf, comb, sel = _post(y, x, chunk(layer, 2), g_post_mix[layer], g_pre_ffn[layer],
                                     scale_f, shift_f, w_router=w_router[idx])
            y = _moe(hf.reshape(m, d), comb.reshape(m, LANES), sel.reshape(m, LANES),
                     w_moe_gate, w_moe_up, w_moe_down, idx)
        if layer + 1 < depth:
            x, h = _post(y, x, gate_f, g_post_ffn[layer], g_pre_mix[layer + 1],
                         chunk(layer + 1, 1), chunk(layer + 1, 0))
        else:
            (x,) = _post(y, x, gate_f, g_post_ffn[layer])
    return x
```

```python
import functools
import math

import jax
import jax.numpy as jnp
from jax import lax
from jax.experimental import pallas as pl
from jax.experimental.pallas import tpu as pltpu

HEAD_DIM = 128
MOBA_BLOCK = 256
MOBA_TOPK = 3
SB_BLOCK = 256
SWA_WINDOW = 128
REL_BUCKETS = 32
REL_MAX_EXACT = 16
REL_MAX_DISTANCE = 128
NORM_EPS = 1e-6
ADA_CHUNKS = 6
TOP_K = 2
LANES = 128
MOE_ROW_TILE = 512
MOE_ROW_QUANT = 128
VMEM_LIMIT = 56 * 1024 * 1024

F32 = jnp.float32
BF16 = jnp.bfloat16
NEG_INF = float("-inf")
LOG2E = math.log2(math.e)


def _params(sem, vmem=VMEM_LIMIT):
    return pltpu.CompilerParams(dimension_semantics=sem, vmem_limit_bytes=vmem)


def _nt_dot(a, b):
    return lax.dot_general(a, b, (((1,), (1,)), ((), ())), preferred_element_type=F32)


def _dot(a, b):
    return jnp.dot(a, b, preferred_element_type=F32)


def _tn_dot(a, b):
    return lax.dot_general(a, b, (((0,), (0,)), ((), ())), preferred_element_type=F32)


def _split_bf16(x):
    hi = x.astype(BF16)
    lo = (x - hi.astype(F32)).astype(BF16)
    return hi, lo


def _rms(x):
    return x * lax.rsqrt(jnp.mean(x * x, axis=-1, keepdims=True) + NORM_EPS)


def _ada_kernel(c_ref, w_ref, b_ref, o_ref):
    c = c_ref[...]
    sc = c * jax.nn.sigmoid(c)
    hi, lo = _split_bf16(sc)
    w = w_ref[...].astype(BF16)
    o_ref[...] = _dot(hi, w) + _dot(lo, w) + b_ref[...]


def _ada_mod(c, w_ada, b_ada):
    depth, d, n = w_ada.shape
    b = c.shape[0]
    rows = 8
    c_pad = jnp.zeros((rows, d), F32).at[:b].set(c)
    tn = _col_tile(n, 512)
    out = pl.pallas_call(
        _ada_kernel,
        grid=(depth, n // tn),
        in_specs=[
            pl.BlockSpec((rows, d), lambda l, j: (0, 0)),
            pl.BlockSpec((None, d, tn), lambda l, j: (l, 0, j)),
            pl.BlockSpec((None, 1, tn), lambda l, j: (l, 0, j)),
        ],
        out_specs=pl.BlockSpec((None, rows, tn), lambda l, j: (l, 0, j)),
        out_shape=jax.ShapeDtypeStruct((depth, rows, n), F32),
        compiler_params=_params(("parallel", "parallel")),
        name="ada_mod",
    )(c_pad, w_ada, b_ada.reshape(depth, 1, n))
    return out[:, :b]


def _prenorm_kernel(x_ref, g_ref, sc_ref, sh_ref, h_ref):
    x = x_ref[...]
    h = (_rms(x) * g_ref[...]) * (1.0 + sc_ref[...]) + sh_ref[...]
    h_ref[...] = h.astype(h_ref.dtype)


def _prenorm(x, g, scale, shift, ts=256):
    b, s, d = x.shape
    return pl.pallas_call(
        _prenorm_kernel,
        grid=(b, s // ts),
        in_specs=[
            pl.BlockSpec((None, ts, d), lambda i, j: (i, j, 0)),
            pl.BlockSpec((1, d), lambda i, j: (0, 0)),
            pl.BlockSpec((None, 1, d), lambda i, j: (i, 0, 0)),
            pl.BlockSpec((None, 1, d), lambda i, j: (i, 0, 0)),
        ],
        out_specs=pl.BlockSpec((None, ts, d), lambda i, j: (i, j, 0)),
        out_shape=jax.ShapeDtypeStruct((b, s, d), BF16),
        compiler_params=_params(("parallel", "parallel")),
        name="prenorm",
    )(x, g.reshape(1, d), scale, shift)


def _post_kernel(y_ref, x_ref, gate_ref, gpost_ref, gnext_ref, sc_ref, sh_ref, *rest,
                 emit_h, route):
    if route:
        wr_ref, rest = rest[0], rest[1:]
    xo_ref, rest = rest[0], rest[1:]
    x_new = x_ref[...] + gate_ref[...] * (_rms(y_ref[...]) * gpost_ref[...])
    xo_ref[...] = x_new
    if not emit_h:
        return
    h = (_rms(x_new) * gnext_ref[...]) * (1.0 + sc_ref[...]) + sh_ref[...]
    if not route:
        rest[0][...] = h.astype(BF16)
        return
    hf_ref, comb_ref, sel_ref = rest
    hf_ref[...] = h
    hi, lo = _split_bf16(h)
    w = wr_ref[...]
    whi, wlo = _split_bf16(w)
    logits = _dot(hi, whi) + _dot(lo, whi) + _dot(hi, wlo)
    n_exp = route
    lane = lax.broadcasted_iota(jnp.int32, logits.shape, 1)
    lg = jnp.where(lane < n_exp, logits, NEG_INF)
    m1 = jnp.max(lg, axis=1, keepdims=True)
    i1 = jnp.min(jnp.where(lg == m1, lane, LANES), axis=1, keepdims=True)
    lg2 = jnp.where(lane == i1, NEG_INF, lg)
    m2 = jnp.max(lg2, axis=1, keepdims=True)
    i2 = jnp.min(jnp.where(lg2 == m2, lane, LANES), axis=1, keepdims=True)
    e2 = jnp.exp(m2 - m1)
    w1 = 1.0 / (1.0 + e2)
    w2 = e2 / (1.0 + e2)
    comb_ref[...] = jnp.where(lane == i1, w1, 0.0) + jnp.where(lane == i2, w2, 0.0)
    sel_ref[...] = jnp.where((lane == i1) | (lane == i2), 1.0, 0.0)


def _post(y, x, gate, g_post, g_next=None, scale=None, shift=None, w_router=None, ts=256):
    b, s, d = x.shape
    emit_h = g_next is not None
    n_exp = 0 if w_router is None else w_router.shape[1]
    if not emit_h:
        g_next, scale, shift = g_post, gate, gate
    row = pl.BlockSpec((None, ts, d), lambda i, j: (i, j, 0))
    vec = pl.BlockSpec((1, d), lambda i, j: (0, 0))
    per_b = pl.BlockSpec((None, 1, d), lambda i, j: (i, 0, 0))
    in_specs = [row, row, per_b, vec, vec, per_b, per_b]
    args = [y.reshape(b, s, d), x, gate, g_post.reshape(1, d), g_next.reshape(1, d), scale, shift]
    out_specs = [row]
    out_shape = [jax.ShapeDtypeStruct((b, s, d), F32)]
    if n_exp:
        wr = jnp.zeros((d, LANES), F32).at[:, :n_exp].set(w_router.astype(F32))
        in_specs.append(pl.BlockSpec((d, LANES), lambda i, j: (0, 0)))
        args.append(wr)
        lane_blk = pl.BlockSpec((None, ts, LANES), lambda i, j: (i, j, 0))
        out_specs += [row, lane_blk, lane_blk]
        out_shape += [jax.ShapeDtypeStruct((b, s, d), F32),
                      jax.ShapeDtypeStruct((b, s, LANES), F32),
                      jax.ShapeDtypeStruct((b, s, LANES), F32)]
    elif emit_h:
        out_specs.append(row)
        out_shape.append(jax.ShapeDtypeStruct((b, s, d), BF16))
    return pl.pallas_call(
        functools.partial(_post_kernel, emit_h=emit_h, route=n_exp),
        grid=(b, s // ts),
        in_specs=in_specs,
        out_specs=out_specs,
        out_shape=out_shape,
        compiler_params=_params(("parallel", "parallel")),
        name="post_norm_residual",
    )(*args)


def _mm_kernel(a_ref, w_ref, o_ref):
    o_ref[...] = _dot(a_ref[...], w_ref[...].astype(BF16)).astype(o_ref.dtype)


def _matmul(a, w, layer, out_dtype, tm=1024, tn=512):
    m, k = a.shape
    n = w.shape[2]
    tm, tn = _col_tile(m, tm), _col_tile(n, tn)
    return pl.pallas_call(
        _mm_kernel,
        grid=(n // tn, m // tm),
        in_specs=[
            pl.BlockSpec((tm, k), lambda j, i: (i, 0)),
            pl.BlockSpec((None, k, tn), lambda j, i: (layer, 0, j)),
        ],
        out_specs=pl.BlockSpec((tm, tn), lambda j, i: (i, j)),
        out_shape=jax.ShapeDtypeStruct((m, n), out_dtype),
        compiler_params=_params(("parallel", "parallel")),
        name="matmul",
    )(a, w)


def _out_proj_kernel(oa_ref, ob_ref, oc_ref, w_ref, y_ref, *, cuts):
    c1, c2 = cuts
    w = w_ref[...].astype(BF16)
    y = _dot(oa_ref[...], w[:c1]) + _dot(ob_ref[...], w[c1:c2]) + _dot(oc_ref[...], w[c2:])
    y_ref[...] = y


def _out_proj(oa, ob, oc, w, layer, tm=1024, tn=512):
    m = oa.shape[0]
    k, n = w.shape[1], w.shape[2]
    c1 = oa.shape[1]
    c2 = c1 + ob.shape[1]
    tm, tn = _col_tile(m, tm), _col_tile(n, tn)
    return pl.pallas_call(
        functools.partial(_out_proj_kernel, cuts=(c1, c2)),
        grid=(n // tn, m // tm),
        in_specs=[
            pl.BlockSpec((tm, oa.shape[1]), lambda j, i: (i, 0)),
            pl.BlockSpec((tm, ob.shape[1]), lambda j, i: (i, 0)),
            pl.BlockSpec((tm, oc.shape[1]), lambda j, i: (i, 0)),
            pl.BlockSpec((None, k, tn), lambda j, i: (layer, 0, j)),
        ],
        out_specs=pl.BlockSpec((tm, tn), lambda j, i: (i, j)),
        out_shape=jax.ShapeDtypeStruct((m, n), F32),
        compiler_params=_params(("parallel", "parallel")),
        name="out_proj",
    )(oa, ob, oc, w)


def _swiglu_up_kernel(a_ref, wg_ref, wu_ref, o_ref):
    a = a_ref[...]
    g = _dot(a, wg_ref[...].astype(BF16))
    u = _dot(a, wu_ref[...].astype(BF16))
    o_ref[...] = (g * jax.nn.sigmoid(g) * u).astype(o_ref.dtype)


def _swiglu_up(a, w_gate, w_up, layer, tm=1024, tn=256):
    m, k = a.shape
    n = w_gate.shape[2]
    tm, tn = _col_tile(m, tm), _col_tile(n, tn)
    wspec = pl.BlockSpec((None, k, tn), lambda j, i: (layer, 0, j))
    return pl.pallas_call(
        _swiglu_up_kernel,
        grid=(n // tn, m // tm),
        in_specs=[pl.BlockSpec((tm, k), lambda j, i: (i, 0)), wspec, wspec],
        out_specs=pl.BlockSpec((tm, tn), lambda j, i: (i, j)),
        out_shape=jax.ShapeDtypeStruct((m, n), BF16),
        compiler_params=_params(("parallel", "parallel")),
        name="swiglu_up",
    )(a, w_gate, w_up)


def _mm_acc_kernel(a_ref, w_ref, o_ref):
    kk = pl.program_id(2)
    p = _dot(a_ref[...], w_ref[...].astype(BF16))

    @pl.when(kk == 0)
    def _():
        o_ref[...] = p

    @pl.when(kk > 0)
    def _():
        o_ref[...] += p


def _matmul_ktiled(a, w, layer, tm=2048, tn=1024, tk=1024):
    m, k = a.shape
    n = w.shape[2]
    tm, tn, tk = _col_tile(m, tm), _col_tile(n, tn), _col_tile(k, tk)
    return pl.pallas_call(
        _mm_acc_kernel,
        grid=(n // tn, m // tm, k // tk),
        in_specs=[
            pl.BlockSpec((tm, tk), lambda j, i, kk: (i, kk)),
            pl.BlockSpec((None, tk, tn), lambda j, i, kk: (layer, kk, j)),
        ],
        out_specs=pl.BlockSpec((tm, tn), lambda j, i, kk: (i, j)),
        out_shape=jax.ShapeDtypeStruct((m, n), F32),
        compiler_params=_params(("parallel", "parallel", "arbitrary")),
        name="matmul_ktiled",
    )(a, w)


def _rel_bucket(dist):
    n = jnp.maximum(dist, 0)
    nf = jnp.maximum(n, 1).astype(F32)
    large = REL_MAX_EXACT + (jnp.log(nf / REL_MAX_EXACT) / math.log(REL_MAX_DISTANCE / REL_MAX_EXACT)
                             * (REL_BUCKETS - REL_MAX_EXACT)).astype(jnp.int32)
    return jnp.where(n < REL_MAX_EXACT, n, jnp.minimum(large, REL_BUCKETS - 1))


def _head_norm(o, g):
    return (_rms(o) * g).astype(BF16)


def _head_cols(h):
    return slice(h * HEAD_DIM, (h + 1) * HEAD_DIM)


def _stack_heads(fn, heads):
    return jnp.concatenate([fn(h) for h in range(heads)], axis=0)


def _lane_heads(fn, heads):
    return jnp.concatenate([fn(h) for h in range(heads)], axis=1)


def _store_heads_transposed(o_ref, y_t, g_ref, heads, rows):
    for h in range(heads):
        y = y_t[:, h * rows:(h + 1) * rows].T * g_ref[:, _head_cols(h)]
        o_ref[:, _head_cols(h)] = y.astype(o_ref.dtype)


def _moba_kernel(q_ref, k_ref, v_ref, bias_ref, g_ref, o_ref, kmean_ref, m_ref, l_ref, acc_ref,
                 *, nb, hb, scale):
    i = pl.program_id(2)
    blk = MOBA_BLOCK

    @pl.when(i == 0)
    def _():
        kf = k_ref[...].astype(F32).reshape(nb, blk, hb * HEAD_DIM)
        kmean_ref[...] = jnp.mean(kf, axis=1)

    khi, klo = _split_bf16(kmean_ref[...])

    def gate_of(h):
        q = q_ref[:, _head_cols(h)]
        return _nt_dot(khi[:, _head_cols(h)], q) + _nt_dot(klo[:, _head_cols(h)], q)

    gate = _lane_heads(gate_of, hb)
    blk_id = lax.broadcasted_iota(jnp.int32, gate.shape, 0)
    gate = jnp.where(blk_id < i, gate, NEG_INF)
    sel = jnp.zeros(gate.shape, F32)
    for _ in range(min(MOBA_TOPK, nb)):
        m = jnp.max(gate, axis=0, keepdims=True)
        first = jnp.min(jnp.where((gate == m) & (m > NEG_INF), blk_id, nb), axis=0, keepdims=True)
        pick = blk_id == first
        sel = jnp.where(pick, 1.0, sel)
        gate = jnp.where(pick, NEG_INF, gate)

    def scores(rows, bias_idx):
        return _lane_heads(
            lambda h: _nt_dot(k_ref[rows, _head_cols(h)], q_ref[:, _head_cols(h)]) * scale
            + bias_ref[h, bias_idx], hb)

    def weighted_values(p, rows):
        pb = p.astype(BF16)
        return _lane_heads(
            lambda h: _tn_dot(v_ref[rows, _head_cols(h)], pb[:, h * blk:(h + 1) * blk]), hb)

    key = lax.broadcasted_iota(jnp.int32, (blk, hb * blk), 0)
    qry = lax.broadcasted_iota(jnp.int32, (blk, hb * blk), 1) & (blk - 1)
    own = pl.ds(pl.multiple_of(i * blk, blk), blk)
    s = jnp.where(key <= qry, scores(own, 0), NEG_INF)
    m0 = jnp.max(s, axis=0, keepdims=True)
    p = jnp.exp2(s - m0)
    m_ref[...] = m0
    l_ref[...] = jnp.sum(p, axis=0, keepdims=True)
    acc_ref[...] = weighted_values(p, own)

    for n in range(nb - 1):
        @pl.when(n < i)
        def _(n=n):
            rows = slice(n * blk, (n + 1) * blk)
            sn = jnp.where(sel[n:n + 1, :] > 0.0, scores(rows, jnp.minimum(i - n, 2)), NEG_INF)
            m_old = m_ref[...]
            m_new = jnp.maximum(m_old, jnp.max(sn, axis=0, keepdims=True))
            alpha = jnp.exp2(m_old - m_new)
            pn = jnp.exp2(sn - m_new)
            m_ref[...] = m_new
            l_ref[...] = alpha * l_ref[...] + jnp.sum(pn, axis=0, keepdims=True)
            acc_ref[...] = alpha * acc_ref[...] + weighted_values(pn, rows)

    o_t = acc_ref[...] / l_ref[...]
    y_t = o_t * lax.rsqrt(jnp.mean(o_t * o_t, axis=0, keepdims=True) + NORM_EPS)
    _store_heads_transposed(o_ref, y_t, g_ref, hb, blk)


def _moba(proj, bias, g, heads, q0, k0, v0):
    b, s, _ = proj.shape
    blk = MOBA_BLOCK
    nb = s // blk
    hb = math.gcd(4, heads, q0, k0, v0)
    wide = hb * HEAD_DIM
    bias = jnp.swapaxes(bias, -1, -2) * LOG2E
    seq = lambda c0: pl.BlockSpec((None, s, wide), lambda bi, h, i: (bi, 0, c0 // hb + h))
    return pl.pallas_call(
        functools.partial(_moba_kernel, nb=nb, hb=hb, scale=HEAD_DIM ** -0.5 * LOG2E),
        grid=(b, heads // hb, nb),
        in_specs=[
            pl.BlockSpec((None, blk, wide), lambda bi, h, i: (bi, i, q0 // hb + h)),
            seq(k0), seq(v0),
            pl.BlockSpec((hb, 3, blk, blk), lambda bi, h, i: (h, 0, 0, 0)),
            pl.BlockSpec((1, wide), lambda bi, h, i: (0, h)),
        ],
        out_specs=pl.BlockSpec((None, blk, wide), lambda bi, h, i: (bi, i, h)),
        out_shape=jax.ShapeDtypeStruct((b, s, heads * HEAD_DIM), BF16),
        scratch_shapes=[pltpu.VMEM((nb, wide), F32), pltpu.VMEM((1, hb * blk), F32),
                        pltpu.VMEM((1, hb * blk), F32), pltpu.VMEM((HEAD_DIM, hb * blk), F32)],
        compiler_params=_params(("parallel", "parallel", "arbitrary")),
        name="moba_attention",
    )(proj, proj, proj, bias, g.reshape(1, -1))


def _sb_kernel(q_ref, k_ref, v_ref, g_ref, o_ref, acc_ref, *, hb, scale):
    i = pl.program_id(2)
    blk = SB_BLOCK
    key = lax.broadcasted_iota(jnp.int32, (blk, hb * blk), 0)
    qry = lax.broadcasted_iota(jnp.int32, (blk, hb * blk), 1) & (blk - 1)
    rs = lax.broadcasted_iota(jnp.int32, (blk, blk), 0)
    cs = lax.broadcasted_iota(jnp.int32, (blk, blk), 1)
    suffix = jnp.where(cs >= rs, 1.0, 0.0).astype(BF16)
    suffix2 = jnp.concatenate([suffix, suffix], axis=1)

    def block(rows, carry, diagonal):
        z = _lane_heads(
            lambda h: _nt_dot(k_ref[rows, _head_cols(h)], q_ref[:, _head_cols(h)]), hb) * scale
        sp = jnp.maximum(z, 0.0) + jnp.log(1.0 + jnp.exp2(jnp.abs(z) * -LOG2E))
        if diagonal:
            strict = key < qry
            sp = jnp.where(strict, sp, 0.0)
        hi, lo = _split_bf16(sp)
        tot = _dot(suffix2, jnp.concatenate([hi, lo], axis=0)) + carry
        a = jnp.exp(z - tot)
        if diagonal:
            a = jnp.where(strict, a, 0.0)
        a = a.astype(BF16)
        av = _lane_heads(
            lambda h: _tn_dot(v_ref[rows, _head_cols(h)], a[:, h * blk:(h + 1) * blk]), hb)
        if diagonal:
            acc_ref[...] = av
        else:
            acc_ref[...] += av
        return tot[0:1, :]

    own = pl.ds(pl.multiple_of(i * blk, blk), blk)
    carry = block(own, jnp.zeros((1, hb * blk), F32), True)

    def past(j):
        return pl.ds(pl.multiple_of(j * blk, blk), blk)

    odd = i & 1
    carry = lax.fori_loop(0, odd, lambda _, c: block(past(i - 1), c, False), carry)

    def pair(step, c):
        j = i - odd - 1 - 2 * step
        return block(past(j - 1), block(past(j), c, False), False)

    lax.fori_loop(0, i >> 1, pair, carry)
    acc = acc_ref[...]
    y_t = acc * lax.rsqrt(jnp.mean(acc * acc, axis=0, keepdims=True) + NORM_EPS)
    _store_heads_transposed(o_ref, y_t, g_ref, hb, blk)


def _stick_breaking(proj, g, heads, q0, k0, v0):
    b, s, _ = proj.shape
    blk = SB_BLOCK
    hb = math.gcd(4, heads, q0, k0, v0)
    wide = hb * HEAD_DIM
    seq = lambda c0: pl.BlockSpec((None, s, wide), lambda bi, h, i: (bi, 0, c0 // hb + h))
    return pl.pallas_call(
        functools.partial(_sb_kernel, hb=hb, scale=HEAD_DIM ** -0.5),
        grid=(b, heads // hb, s // blk),
        in_specs=[
            pl.BlockSpec((None, blk, wide), lambda bi, h, i: (bi, i, q0 // hb + h)),
            seq(k0), seq(v0),
            pl.BlockSpec((1, wide), lambda bi, h, i: (0, h)),
        ],
        out_specs=pl.BlockSpec((None, blk, wide), lambda bi, h, i: (bi, i, h)),
        out_shape=jax.ShapeDtypeStruct((b, s, heads * HEAD_DIM), BF16),
        scratch_shapes=[pltpu.VMEM((HEAD_DIM, hb * blk), F32)],
        compiler_params=_params(("parallel", "parallel", "parallel")),
        name="stick_breaking_attention",
    )(proj, proj, proj, g.reshape(1, -1))


def _swa_kernel(q_ref, kp_ref, kc_ref, vp_ref, vc_ref, bias_ref, sink_ref, g_ref, o_ref,
                *, group, scale):
    n = pl.program_id(2)
    w = SWA_WINDOW
    kk = jnp.concatenate([kp_ref[...], kc_ref[...]], axis=0)
    vv = jnp.concatenate([vp_ref[...], vc_ref[...]], axis=0)
    key = lax.broadcasted_iota(jnp.int32, (2 * w, group * w), 0)
    qry = lax.broadcasted_iota(jnp.int32, (2 * w, group * w), 1) & (w - 1)
    dist = qry + w - key
    mask = (dist >= 0) & (dist < w) & (key + n * w >= w)
    q = _stack_heads(lambda h: q_ref[:, _head_cols(h)], group)
    s = jnp.where(mask, _nt_dot(kk, q) * scale + bias_ref[...], NEG_INF)
    sink = sink_ref[...]
    m = jnp.maximum(jnp.max(s, axis=0, keepdims=True), sink)
    e = jnp.exp(s - m)
    denom = jnp.sum(e, axis=0, keepdims=True) + jnp.exp(sink - m)
    o_t = _tn_dot(vv, e.astype(BF16)) / denom
    y_t = o_t * lax.rsqrt(jnp.mean(o_t * o_t, axis=0, keepdims=True) + NORM_EPS)
    for h in range(group):
        y = y_t[:, h * w:(h + 1) * w].T * g_ref[:, _head_cols(h)]
        o_ref[:, _head_cols(h)] = y.astype(BF16)


def _swa(proj, bias, sinks, g, q_heads, kv_heads, q0, k0, v0):
    b, s, _ = proj.shape
    w = SWA_WINDOW
    group = q_heads // kv_heads
    bias = bias.reshape(kv_heads, group, w, 2 * w).transpose(0, 3, 1, 2).reshape(
        kv_heads, 2 * w, group * w)
    sink_rows = jnp.broadcast_to(sinks.astype(F32).reshape(kv_heads, 1, group, 1),
                                 (kv_heads, 1, group, w)).reshape(kv_heads, 1, group * w)
    kv = lambda c0, prev: pl.BlockSpec(
        (None, w, HEAD_DIM),
        lambda bi, kh, n: (bi, jnp.maximum(n - 1, 0) if prev else n, c0 + kh))
    return pl.pallas_call(
        functools.partial(_swa_kernel, group=group, scale=HEAD_DIM ** -0.5),
        grid=(b, kv_heads, s // w),
        in_specs=[
            pl.BlockSpec((None, w, group * HEAD_DIM), lambda bi, kh, n: (bi, n, q0 // group + kh)),
            kv(k0, True), kv(k0, False), kv(v0, True), kv(v0, False),
            pl.BlockSpec((None, 2 * w, group * w), lambda bi, kh, n: (kh, 0, 0)),
            pl.BlockSpec((None, 1, group * w), lambda bi, kh, n: (kh, 0, 0)),
            pl.BlockSpec((1, group * HEAD_DIM), lambda bi, kh, n: (0, kh)),
        ],
        out_specs=pl.BlockSpec((None, w, group * HEAD_DIM), lambda bi, kh, n: (bi, n, kh)),
        out_shape=jax.ShapeDtypeStruct((b, s, q_heads * HEAD_DIM), BF16),
        compiler_params=_params(("parallel", "parallel", "parallel")),
        name="swa_sink_attention",
    )(proj, proj, proj, proj, proj, bias, sink_rows, g.reshape(1, -1))


def _toeplitz(v, rows, cols):
    heads, length = v.shape
    t = jnp.tile(v, (1, rows))[:, :rows * (length - 1)].reshape(heads, rows, length - 1)
    return t[:, :, :cols]


def _bias_tiles(rel_bias, moba_heads, swa_heads):
    blk, w = MOBA_BLOCK, SWA_WINDOW
    length = 2 * blk
    k = jnp.arange(length)
    bd = rel_bias[_rel_bucket(k)].astype(F32).T
    mb, sw = bd[:moba_heads], bd[moba_heads:moba_heads + swa_heads]
    own = _toeplitz(mb[:, (-k) % length], blk, blk)
    adj = _toeplitz(mb[:, (blk - k) % length], blk, blk)
    far = jnp.broadcast_to(mb[:, length - 1][:, None, None], own.shape)
    swa = _toeplitz(sw[:, (w - k) % length], w, 2 * w)
    return jnp.stack([own, adj, far], axis=1), swa


def _head_counts(w_in, g_moba, g_sb, g_swa):
    moba_w, sb_w, swa_w = g_moba.shape[1], g_sb.shape[1], g_swa.shape[1]
    kv_w = (w_in.shape[2] - 3 * moba_w - 3 * sb_w - swa_w) // 2
    return moba_w // HEAD_DIM, sb_w // HEAD_DIM, swa_w // HEAD_DIM, kv_w // HEAD_DIM


def _mixing(h, bias_tiles, w_in, w_out, g_moba, g_sb, g_swa, sinks, layer):
    b, s, d = h.shape
    in_w = w_in.shape[2]
    mh, sh, qh, kvh = _head_counts(w_in, g_moba, g_sb, g_swa)
    assert qh % kvh == 0 and (3 * mh + 3 * sh) % (qh // kvh) == 0
    proj = _matmul(h.reshape(b * s, d), w_in, layer, BF16).reshape(b, s, in_w)
    o_a = _moba(proj, bias_tiles[0], g_moba[layer], mh, 0, mh, 2 * mh)
    o_b = _stick_breaking(proj, g_sb[layer], sh, 3 * mh, 3 * mh + sh, 3 * mh + 2 * sh)
    c0 = 3 * mh + 3 * sh
    o_c = _swa(proj, bias_tiles[1], sinks[layer], g_swa[layer], qh, kvh, c0, c0 + qh, c0 + qh + kvh)
    m = b * s
    return _out_proj(o_a.reshape(m, -1), o_b.reshape(m, -1), o_c.reshape(m, -1), w_out, layer)


def _row_gather_pipeline(n_live, fetch_rows, wait_rows, compute, idle=None):
    i = pl.program_id(0)

    @pl.when((i == 0) & (n_live > 0))
    def _():
        fetch_rows(0, 0)

    @pl.when(i + 1 < n_live)
    def _():
        fetch_rows(i + 1, (i + 1) % 2)

    @pl.when(i < n_live)
    def _():
        wait_rows(i % 2)
        compute(i % 2)

    if idle is not None:
        pl.when(i >= n_live)(idle)


def _gather_kernel(tok_ref, nlive_ref, src_ref, o_ref, buf_ref, sem_ref, *, rows):
    chunks = src_ref.shape[1]

    def copy(step, slot, group, sub):
        tok = tok_ref[step * rows + group * 8 + sub]
        return pltpu.make_async_copy(src_ref.at[tok], buf_ref.at[slot, group, :, sub, :],
                                     sem_ref.at[slot])

    def fetch_rows(step, slot):
        def start(group, c):
            for sub in range(8):
                copy(step, slot, group, sub).start()
            return c
        lax.fori_loop(0, rows // 8, start, 0)

    def wait_rows(slot):
        def wait(group, c):
            for sub in range(8):
                copy(0, slot, group, sub).wait()
            return c
        lax.fori_loop(0, rows // 8, wait, 0)

    def compute(slot):
        for j in range(chunks):
            o_ref[:, j * LANES:(j + 1) * LANES] = (
                buf_ref[slot, :, j].reshape(rows, LANES).astype(o_ref.dtype))

    def idle():
        o_ref[...] = jnp.zeros_like(o_ref)

    _row_gather_pipeline(nlive_ref[0], fetch_rows, wait_rows, compute, idle)


def _gather_rows(src, token_of, live_rows, rows=256):
    p = token_of.shape[0]
    t, d = src.shape
    chunks = d // LANES
    n_live = (live_rows // rows).astype(jnp.int32).reshape(1)
    return pl.pallas_call(
        functools.partial(_gather_kernel, rows=rows),
        grid_spec=pltpu.PrefetchScalarGridSpec(
            num_scalar_prefetch=2,
            grid=(p // rows,),
            in_specs=[pl.BlockSpec(memory_space=pl.ANY)],
            out_specs=pl.BlockSpec((rows, d), lambda i, tok, nl: (i, 0)),
            scratch_shapes=[pltpu.VMEM((2, rows // 8, chunks, 8, LANES), F32),
                            pltpu.SemaphoreType.DMA((2,))],
        ),
        out_shape=jax.ShapeDtypeStruct((p, d), BF16),
        compiler_params=_params(("arbitrary",)),
        name="moe_gather",
    )(token_of, n_live, src.reshape(t, chunks, LANES))


def _gmm_kernel(tile_ref, jn_ref, exp_ref, otile_ref, ojn_ref, live_ref, a_ref, *rest, dual, quant):
    s = pl.program_id(0)
    o_ref = rest[-1]
    tm = o_ref.shape[0]

    @pl.when(live_ref[s] == 0)
    def _():
        o_ref[...] = jnp.zeros_like(o_ref)

    for rows in range(quant, tm + 1, quant):
        @pl.when(live_ref[s] == rows)
        def _(rows=rows):
            a = a_ref[:rows]
            if dual:
                wg_ref, wu_ref, _ = rest
                g = _dot(a, wg_ref[...].astype(BF16))
                u = _dot(a, wu_ref[...].astype(BF16))
                o_ref[:rows] = (g * jax.nn.sigmoid(g) * u).astype(o_ref.dtype)
            else:
                o_ref[:rows] = _dot(a, rest[0][...].astype(BF16)).astype(o_ref.dtype)
            if rows < tm:
                o_ref[rows:] = jnp.zeros((tm - rows, o_ref.shape[1]), o_ref.dtype)


def _grouped_matmul(a, weights, layer, sched, out_dtype, tn):
    p, k = a.shape
    n = weights[0].shape[3]
    tm = MOE_ROW_TILE
    steps = sched[0].shape[0]
    wspec = pl.BlockSpec((None, None, k, tn),
                         lambda s, t, j, e, ot, oj, v: (layer, e[s], 0, j[s]))
    return pl.pallas_call(
        functools.partial(_gmm_kernel, dual=len(weights) == 2, quant=MOE_ROW_QUANT),
        grid_spec=pltpu.PrefetchScalarGridSpec(
            num_scalar_prefetch=6,
            grid=(steps,),
            in_specs=[pl.BlockSpec((tm, k), lambda s, t, j, e, ot, oj, v: (t[s], 0))]
            + [wspec] * len(weights),
            out_specs=pl.BlockSpec((tm, tn), lambda s, t, j, e, ot, oj, v: (ot[s], oj[s])),
        ),
        out_shape=jax.ShapeDtypeStruct((p, n), out_dtype),
        compiler_params=_params(("arbitrary",)),
        name="moe_grouped_matmul",
    )(*sched, a, *weights)


def _combine_kernel(pos_ref, src_ref, w_ref, o_ref, buf_ref, sem_ref, *, rows, tokens):
    def copy(step, slot, r, choice):
        return pltpu.make_async_copy(
            src_ref.at[pl.ds(pos_ref[choice * tokens + step * rows + r], 1)],
            buf_ref.at[slot, choice, pl.ds(r, 1)], sem_ref.at[slot])

    def fetch_rows(step, slot):
        def start(r, c):
            for choice in range(TOP_K):
                copy(step, slot, r, choice).start(priority=choice % 2)
            return c
        lax.fori_loop(0, rows, start, 0, unroll=4)

    def wait_rows(slot):
        def wait(r, c):
            for choice in range(TOP_K):
                copy(0, slot, r, choice).wait()
            return c
        lax.fori_loop(0, rows, wait, 0, unroll=4)

    def compute(slot):
        wts = w_ref[...]
        o_ref[...] = wts[:, 0:1] * buf_ref[slot, 0] + wts[:, 1:2] * buf_ref[slot, 1]

    _row_gather_pipeline(pl.num_programs(0), fetch_rows, wait_rows, compute)


def _combine(expert_out, pos, wts, rows=256):
    t = pos.shape[0] // TOP_K
    d = expert_out.shape[1]
    return pl.pallas_call(
        functools.partial(_combine_kernel, rows=rows, tokens=t),
        grid_spec=pltpu.PrefetchScalarGridSpec(
            num_scalar_prefetch=1,
            grid=(t // rows,),
            in_specs=[pl.BlockSpec(memory_space=pl.ANY),
                      pl.BlockSpec((rows, LANES), lambda i, pos: (i, 0))],
            out_specs=pl.BlockSpec((rows, d), lambda i, pos: (i, 0)),
            scratch_shapes=[pltpu.VMEM((2, TOP_K, rows, d), F32), pltpu.SemaphoreType.DMA((2,))],
        ),
        out_shape=jax.ShapeDtypeStruct((t, d), F32),
        compiler_params=_params(("arbitrary",)),
        name="moe_combine",
    )(pos, expert_out, wts)


def _route_plan(sel, comb, n_exp):
    t = sel.shape[0]
    tm = MOE_ROW_TILE
    n_tiles = (t * TOP_K) // tm + n_exp
    mask = sel > 0.5
    cnt = jnp.sum(mask, axis=0).astype(jnp.int32)
    tiles_e = (cnt + tm - 1) // tm
    tile_end = jnp.cumsum(tiles_e)
    tile_start = tile_end - tiles_e
    rank = jnp.cumsum(mask, axis=0).astype(jnp.int32) - 1
    pos_te = tile_start[None, :] * tm + rank
    order = jnp.argsort(jnp.logical_not(mask), axis=1, stable=True)[:, :TOP_K].astype(jnp.int32)
    pos = jnp.take_along_axis(pos_te, order, axis=1)
    wts = jnp.take_along_axis(comb, order, axis=1)
    token_of = jnp.zeros((n_tiles * tm,), jnp.int32).at[pos.reshape(-1)].set(
        jnp.repeat(jnp.arange(t, dtype=jnp.int32), TOP_K))
    wts_pad = jnp.zeros((t, LANES), F32).at[:, :TOP_K].set(wts)

    def schedule(col_tiles):
        steps = jnp.arange(n_tiles * col_tiles, dtype=jnp.int32)
        live = tile_end[-1] * col_tiles
        last = jnp.maximum(live - 1, 0)
        steps_c = jnp.minimum(steps, last)
        e_of = jnp.minimum(jnp.searchsorted(tile_end * col_tiles, steps_c, side="right"),
                           n_exp - 1).astype(jnp.int32)
        local = steps_c - tile_start[e_of] * col_tiles
        n_e = jnp.maximum(tiles_e[e_of], 1)
        j_of = (local // n_e).astype(jnp.int32)
        t_of = (tile_start[e_of] + local % n_e).astype(jnp.int32)
        n_dead = jnp.maximum(n_tiles - tile_end[-1], 1)
        dead = jnp.maximum(steps - live, 0)
        is_live = steps < live
        ot_of = jnp.where(is_live, t_of, tile_end[-1] + dead % n_dead).astype(jnp.int32)
        oj_of = jnp.where(is_live, j_of, dead // n_dead).astype(jnp.int32)
        in_tile = jnp.clip(cnt[e_of] - (local % n_e) * tm, 0, tm)
        quant = MOE_ROW_QUANT
        rows_live = jnp.where(is_live, (in_tile + quant - 1) // quant * quant, 0).astype(jnp.int32)
        return t_of, j_of, e_of, ot_of, oj_of, rows_live

    return token_of, tile_end[-1] * tm, schedule, pos.T.reshape(-1), wts_pad


def _col_tile(n, pref):
    while n % pref:
        pref //= 2
    return pref


def _moe(h_f32, comb, sel, w_gate, w_up, w_down, layer_idx):
    t, d = h_f32.shape
    n_exp = w_gate.shape[1]
    d_exp = w_gate.shape[3]
    tn_up, tn_down = _col_tile(d_exp, 512), _col_tile(d, 1024)
    token_of, live_rows, schedule, pos, wts = _route_plan(sel[:, :n_exp], comb[:, :n_exp], n_exp)
    hs = _gather_rows(h_f32, token_of, live_rows)
    act = _grouped_matmul(hs, (w_gate, w_up), layer_idx, schedule(d_exp // tn_up), BF16, tn_up)
    out = _grouped_matmul(act, (w_down,), layer_idx, schedule(d // tn_down), F32, tn_down)
    return _combine(out, pos, wts)


def kernel(x, c, rel_bias, w_ada, b_ada, g_pre_mix, w_in, g_grp_moba, g_grp_sb, g_grp_swa, swa_sinks, w_out, g_post_mix, g_pre_ffn, w_ff_gate, w_ff_up, w_ff_down, w_router, w_moe_gate, w_moe_up, w_moe_down, g_post_ffn):
    b, s, d = x.shape
    depth = w_ada.shape[0]
    m = b * s
    mod = _ada_mod(c, w_ada, b_ada).reshape(depth, b, ADA_CHUNKS, 1, d)
    chunk = lambda layer, idx: mod[layer, :, idx]
    h = _prenorm(x, g_pre_mix[0], chunk(0, 1), chunk(0, 0))
    mh, _, qh, _ = _head_counts(w_in, g_grp_moba, g_grp_sb, g_grp_swa)
    bias_tiles = _bias_tiles(rel_bias, mh, qh)
    for layer in range(depth):
        shift_f, scale_f, gate_f = chunk(layer, 3), chunk(layer, 4), chunk(layer, 5)
        y = _mixing(h, bias_tiles, w_in, w_out, g_grp_moba, g_grp_sb, g_grp_swa, swa_sinks, layer)
        idx = layer // 2
        if layer % 2 == 0:
            x, h = _post(y, x, chunk(layer, 2), g_post_mix[layer], g_pre_ffn[layer], scale_f, shift_f)
            act = _swiglu_up(h.reshape(m, d), w_ff_gate, w_ff_up, idx)
            y = _matmul_ktiled(act, w_ff_down, idx)
        else:
            x, hf, comb, sel = _post(y, x, chunk(layer, 2), g_post_mix[layer], g_pre_ffn[layer],
                                     scale_f, shift_f, w_router=w_router[idx])
            y = _moe(hf.reshape(m, d), comb.reshape(m, LANES), sel.reshape(m, LANES),
                     w_moe_gate, w_moe_up, w_moe_down, idx)
        if layer + 1 < depth:
            x, h = _post(y, x, gate_f, g_post_ffn[layer], g_pre_mix[layer + 1],
                         chunk(layer + 1, 1), chunk(layer + 1, 0))
        else:
            (x,) = _post(y, x, gate_f, g_post_ffn[layer])
    return x
```

```python
import functools
import math

import jax
import jax.numpy as jnp
from jax import lax
from jax.experimental import pallas as pl
from jax.experimental.pallas import tpu as pltpu

HEAD_DIM = 128
MOBA_BLOCK = 256
MOBA_TOPK = 3
SB_BLOCK = 256
SWA_WINDOW = 128
REL_BUCKETS = 32
REL_MAX_EXACT = 16
REL_MAX_DISTANCE = 128
NORM_EPS = 1e-6
ADA_CHUNKS = 6
TOP_K = 2
LANES = 128
MOE_ROW_TILE = 512
MOE_ROW_QUANT = 128
VMEM_LIMIT = 56 * 1024 * 1024

F32 = jnp.float32
BF16 = jnp.bfloat16
NEG_INF = float("-inf")
LOG2E = math.log2(math.e)


def _params(sem, vmem=VMEM_LIMIT):
    return pltpu.CompilerParams(dimension_semantics=sem, vmem_limit_bytes=vmem)


def _nt_dot(a, b):
    return lax.dot_general(a, b, (((1,), (1,)), ((), ())), preferred_element_type=F32)


def _dot(a, b):
    return jnp.dot(a, b, preferred_element_type=F32)


def _tn_dot(a, b):
    return lax.dot_general(a, b, (((0,), (0,)), ((), ())), preferred_element_type=F32)


def _split_bf16(x):
    hi = x.astype(BF16)
    lo = (x - hi.astype(F32)).astype(BF16)
    return hi, lo


def _rms(x):
    return x * lax.rsqrt(jnp.mean(x * x, axis=-1, keepdims=True) + NORM_EPS)


def _ada_kernel(c_ref, w_ref, b_ref, o_ref):
    c = c_ref[...]
    sc = c * jax.nn.sigmoid(c)
    hi, lo = _split_bf16(sc)
    w = w_ref[...].astype(BF16)
    o_ref[...] = _dot(hi, w) + _dot(lo, w) + b_ref[...]


def _ada_mod(c, w_ada, b_ada):
    depth, d, n = w_ada.shape
    b = c.shape[0]
    rows = 8
    c_pad = jnp.zeros((rows, d), F32).at[:b].set(c)
    tn = _col_tile(n, 512)
    out = pl.pallas_call(
        _ada_kernel,
        grid=(depth, n // tn),
        in_specs=[
            pl.BlockSpec((rows, d), lambda l, j: (0, 0)),
            pl.BlockSpec((None, d, tn), lambda l, j: (l, 0, j)),
            pl.BlockSpec((None, 1, tn), lambda l, j: (l, 0, j)),
        ],
        out_specs=pl.BlockSpec((None, rows, tn), lambda l, j: (l, 0, j)),
        out_shape=jax.ShapeDtypeStruct((depth, rows, n), F32),
        compiler_params=_params(("parallel", "parallel")),
        name="ada_mod",
    )(c_pad, w_ada, b_ada.reshape(depth, 1, n))
    return out[:, :b]


def _prenorm_kernel(x_ref, g_ref, sc_ref, sh_ref, h_ref):
    x = x_ref[...]
    h = (_rms(x) * g_ref[...]) * (1.0 + sc_ref[...]) + sh_ref[...]
    h_ref[...] = h.astype(h_ref.dtype)


def _prenorm(x, g, scale, shift, ts=256):
    b, s, d = x.shape
    return pl.pallas_call(
        _prenorm_kernel,
        grid=(b, s // ts),
        in_specs=[
            pl.BlockSpec((None, ts, d), lambda i, j: (i, j, 0)),
            pl.BlockSpec((1, d), lambda i, j: (0, 0)),
            pl.BlockSpec((None, 1, d), lambda i, j: (i, 0, 0)),
            pl.BlockSpec((None, 1, d), lambda i, j: (i, 0, 0)),
        ],
        out_specs=pl.BlockSpec((None, ts, d), lambda i, j: (i, j, 0)),
        out_shape=jax.ShapeDtypeStruct((b, s, d), BF16),
        compiler_params=_params(("parallel", "parallel")),
        name="prenorm",
    )(x, g.reshape(1, d), scale, shift)


def _post_kernel(y_ref, x_ref, gate_ref, gpost_ref, gnext_ref, sc_ref, sh_ref, *rest,
                 emit_h, route):
    if route:
        wr_ref, rest = rest[0], rest[1:]
    xo_ref, rest = rest[0], rest[1:]
    x_new = x_ref[...] + gate_ref[...] * (_rms(y_ref[...]) * gpost_ref[...])
    xo_ref[...] = x_new
    if not emit_h:
        return
    h = (_rms(x_new) * gnext_ref[...]) * (1.0 + sc_ref[...]) + sh_ref[...]
    if not route:
        rest[0][...] = h.astype(BF16)
        return
    hf_ref, comb_ref, sel_ref = rest
    hf_ref[...] = h
    hi, lo = _split_bf16(h)
    w = wr_ref[...]
    whi, wlo = _split_bf16(w)
    logits = _dot(hi, whi) + _dot(lo, whi) + _dot(hi, wlo)
    n_exp = route
    lane = lax.broadcasted_iota(jnp.int32, logits.shape, 1)
    lg = jnp.where(lane < n_exp, logits, NEG_INF)
    m1 = jnp.max(lg, axis=1, keepdims=True)
    i1 = jnp.min(jnp.where(lg == m1, lane, LANES), axis=1, keepdims=True)
    lg2 = jnp.where(lane == i1, NEG_INF, lg)
    m2 = jnp.max(lg2, axis=1, keepdims=True)
    i2 = jnp.min(jnp.where(lg2 == m2, lane, LANES), axis=1, keepdims=True)
    e2 = jnp.exp(m2 - m1)
    w1 = 1.0 / (1.0 + e2)
    w2 = e2 / (1.0 + e2)
    comb_ref[...] = jnp.where(lane == i1, w1, 0.0) + jnp.where(lane == i2, w2, 0.0)
    sel_ref[...] = jnp.where((lane == i1) | (lane == i2), 1.0, 0.0)


def _post(y, x, gate, g_post, g_next=None, scale=None, shift=None, w_router=None, ts=256):
    b, s, d = x.shape
    emit_h = g_next is not None
    n_exp = 0 if w_router is None else w_router.shape[1]
    if not emit_h:
        g_next, scale, shift = g_post, gate, gate
    row = pl.BlockSpec((None, ts, d), lambda i, j: (i, j, 0))
    vec = pl.BlockSpec((1, d), lambda i, j: (0, 0))
    per_b = pl.BlockSpec((None, 1, d), lambda i, j: (i, 0, 0))
    in_specs = [row, row, per_b, vec, vec, per_b, per_b]
    args = [y.reshape(b, s, d), x, gate, g_post.reshape(1, d), g_next.reshape(1, d), scale, shift]
    out_specs = [row]
    out_shape = [jax.ShapeDtypeStruct((b, s, d), F32)]
    if n_exp:
        wr = jnp.zeros((d, LANES), F32).at[:, :n_exp].set(w_router.astype(F32))
        in_specs.append(pl.BlockSpec((d, LANES), lambda i, j: (0, 0)))
        args.append(wr)
        lane_blk = pl.BlockSpec((None, ts, LANES), lambda i, j: (i, j, 0))
        out_specs += [row, lane_blk, lane_blk]
        out_shape += [jax.ShapeDtypeStruct((b, s, d), F32),
                      jax.ShapeDtypeStruct((b, s, LANES), F32),
                      jax.ShapeDtypeStruct((b, s, LANES), F32)]
    elif emit_h:
        out_specs.append(row)
        out_shape.append(jax.ShapeDtypeStruct((b, s, d), BF16))
    return pl.pallas_call(
        functools.partial(_post_kernel, emit_h=emit_h, route=n_exp),
        grid=(b, s // ts),
        in_specs=in_specs,
        out_specs=out_specs,
        out_shape=out_shape,
        compiler_params=_params(("parallel", "parallel")),
        name="post_norm_residual",
    )(*args)


def _mm_kernel(a_ref, w_ref, o_ref):
    o_ref[...] = _dot(a_ref[...], w_ref[...].astype(BF16)).astype(o_ref.dtype)


def _matmul(a, w, layer, out_dtype, tm=1024, tn=512):
    m, k = a.shape
    n = w.shape[2]
    tm, tn = _col_tile(m, tm), _col_tile(n, tn)
    return pl.pallas_call(
        _mm_kernel,
        grid=(n // tn, m // tm),
        in_specs=[
            pl.BlockSpec((tm, k), lambda j, i: (i, 0)),
            pl.BlockSpec((None, k, tn), lambda j, i: (layer, 0, j)),
        ],
        out_specs=pl.BlockSpec((tm, tn), lambda j, i: (i, j)),
        out_shape=jax.ShapeDtypeStruct((m, n), out_dtype),
        compiler_params=_params(("parallel", "parallel")),
        name="matmul",
    )(a, w)


def _out_proj_kernel(oa_ref, ob_ref, oc_ref, w_ref, y_ref, *, cuts):
    c1, c2 = cuts
    w = w_ref[...].astype(BF16)
    y = _dot(oa_ref[...], w[:c1]) + _dot(ob_ref[...], w[c1:c2]) + _dot(oc_ref[...], w[c2:])
    y_ref[...] = y


def _out_proj(oa, ob, oc, w, layer, tm=1024, tn=512):
    m = oa.shape[0]
    k, n = w.shape[1], w.shape[2]
    c1 = oa.shape[1]
    c2 = c1 + ob.shape[1]
    tm, tn = _col_tile(m, tm), _col_tile(n, tn)
    return pl.pallas_call(
        functools.partial(_out_proj_kernel, cuts=(c1, c2)),
        grid=(n // tn, m // tm),
        in_specs=[
            pl.BlockSpec((tm, oa.shape[1]), lambda j, i: (i, 0)),
            pl.BlockSpec((tm, ob.shape[1]), lambda j, i: (i, 0)),
            pl.BlockSpec((tm, oc.shape[1]), lambda j, i: (i, 0)),
            pl.BlockSpec((None, k, tn), lambda j, i: (layer, 0, j)),
        ],
        out_specs=pl.BlockSpec((tm, tn), lambda j, i: (i, j)),
        out_shape=jax.ShapeDtypeStruct((m, n), F32),
        compiler_params=_params(("parallel", "parallel")),
        name="out_proj",
    )(oa, ob, oc, w)


def _swiglu_up_kernel(a_ref, wg_ref, wu_ref, o_ref):
    a = a_ref[...]
    g = _dot(a, wg_ref[...].astype(BF16))
    u = _dot(a, wu_ref[...].astype(BF16))
    o_ref[...] = (g * jax.nn.sigmoid(g) * u).astype(o_ref.dtype)


def _swiglu_up(a, w_gate, w_up, layer, tm=1024, tn=256):
    m, k = a.shape
    n = w_gate.shape[2]
    tm, tn = _col_tile(m, tm), _col_tile(n, tn)
    wspec = pl.BlockSpec((None, k, tn), lambda j, i: (layer, 0, j))
    return pl.pallas_call(
        _swiglu_up_kernel,
        grid=(n // tn, m // tm),
        in_specs=[pl.BlockSpec((tm, k), lambda j, i: (i, 0)), wspec, wspec],
        out_specs=pl.BlockSpec((tm, tn), lambda j, i: (i, j)),
        out_shape=jax.ShapeDtypeStruct((m, n), BF16),
        compiler_params=_params(("parallel", "parallel")),
        name="swiglu_up",
    )(a, w_gate, w_up)


def _mm_acc_kernel(a_ref, w_ref, o_ref):
    kk = pl.program_id(2)
    p = _dot(a_ref[...], w_ref[...].astype(BF16))

    @pl.when(kk == 0)
    def _():
        o_ref[...] = p

    @pl.when(kk > 0)
    def _():
        o_ref[...] += p


def _matmul_ktiled(a, w, layer, tm=2048, tn=1024, tk=1024):
    m, k = a.shape
    n = w.shape[2]
    tm, tn, tk = _col_tile(m, tm), _col_tile(n, tn), _col_tile(k, tk)
    return pl.pallas_call(
        _mm_acc_kernel,
        grid=(n // tn, m // tm, k // tk),
        in_specs=[
            pl.BlockSpec((tm, tk), lambda j, i, kk: (i, kk)),
            pl.BlockSpec((None, tk, tn), lambda j, i, kk: (layer, kk, j)),
        ],
        out_specs=pl.BlockSpec((tm, tn), lambda j, i, kk: (i, j)),
        out_shape=jax.ShapeDtypeStruct((m, n), F32),
        compiler_params=_params(("parallel", "parallel", "arbitrary")),
        name="matmul_ktiled",
    )(a, w)


def _rel_bucket(dist):
    n = jnp.maximum(dist, 0)
    nf = jnp.maximum(n, 1).astype(F32)
    large = REL_MAX_EXACT + (jnp.log(nf / REL_MAX_EXACT) / math.log(REL_MAX_DISTANCE / REL_MAX_EXACT)
                             * (REL_BUCKETS - REL_MAX_EXACT)).astype(jnp.int32)
    return jnp.where(n < REL_MAX_EXACT, n, jnp.minimum(large, REL_BUCKETS - 1))


def _head_norm(o, g):
    return (_rms(o) * g).astype(BF16)


def _head_cols(h):
    return slice(h * HEAD_DIM, (h + 1) * HEAD_DIM)


def _stack_heads(fn, heads):
    return jnp.concatenate([fn(h) for h in range(heads)], axis=0)


def _lane_heads(fn, heads):
    return jnp.concatenate([fn(h) for h in range(heads)], axis=1)


def _store_heads_transposed(o_ref, y_t, g_ref, heads, rows):
    for h in range(heads):
        y = y_t[:, h * rows:(h + 1) * rows].T * g_ref[:, _head_cols(h)]
        o_ref[:, _head_cols(h)] = y.astype(o_ref.dtype)


def _moba_kernel(q_ref, k_ref, v_ref, bias_ref, g_ref, o_ref, kmean_ref, m_ref, l_ref, acc_ref,
                 *, nb, hb, scale):
    i = pl.program_id(2)
    blk = MOBA_BLOCK

    @pl.when(i == 0)
    def _():
        kf = k_ref[...].astype(F32).reshape(nb, blk, hb * HEAD_DIM)
        kmean_ref[...] = jnp.mean(kf, axis=1)

    khi, klo = _split_bf16(kmean_ref[...])

    def gate_of(h):
        q = q_ref[:, _head_cols(h)]
        return _nt_dot(khi[:, _head_cols(h)], q) + _nt_dot(klo[:, _head_cols(h)], q)

    gate = _lane_heads(gate_of, hb)
    blk_id = lax.broadcasted_iota(jnp.int32, gate.shape, 0)
    gate = jnp.where(blk_id < i, gate, NEG_INF)
    sel = jnp.zeros(gate.shape, F32)
    for _ in range(min(MOBA_TOPK, nb)):
        m = jnp.max(gate, axis=0, keepdims=True)
        first = jnp.min(jnp.where((gate == m) & (m > NEG_INF), blk_id, nb), axis=0, keepdims=True)
        pick = blk_id == first
        sel = jnp.where(pick, 1.0, sel)
        gate = jnp.where(pick, NEG_INF, gate)

    def scores(rows, bias_idx):
        return _lane_heads(
            lambda h: _nt_dot(k_ref[rows, _head_cols(h)], q_ref[:, _head_cols(h)]) * scale
            + bias_ref[h, bias_idx], hb)

    def weighted_values(p, rows):
        pb = p.astype(BF16)
        return _lane_heads(
            lambda h: _tn_dot(v_ref[rows, _head_cols(h)], pb[:, h * blk:(h + 1) * blk]), hb)

    key = lax.broadcasted_iota(jnp.int32, (blk, hb * blk), 0)
    qry = lax.broadcasted_iota(jnp.int32, (blk, hb * blk), 1) & (blk - 1)
    own = pl.ds(pl.multiple_of(i * blk, blk), blk)
    s = jnp.where(key <= qry, scores(own, 0), NEG_INF)
    m0 = jnp.max(s, axis=0, keepdims=True)
    p = jnp.exp2(s - m0)
    m_ref[...] = m0
    l_ref[...] = jnp.sum(p, axis=0, keepdims=True)
    acc_ref[...] = weighted_values(p, own)

    for n in range(nb - 1):
        @pl.when(n < i)
        def _(n=n):
            rows = slice(n * blk, (n + 1) * blk)
            sn = jnp.where(sel[n:n + 1, :] > 0.0, scores(rows, jnp.minimum(i - n, 2)), NEG_INF)
            m_old = m_ref[...]
            m_new = jnp.maximum(m_old, jnp.max(sn, axis=0, keepdims=True))
            alpha = jnp.exp2(m_old - m_new)
            pn = jnp.exp2(sn - m_new)
            m_ref[...] = m_new
            l_ref[...] = alpha * l_ref[...] + jnp.sum(pn, axis=0, keepdims=True)
            acc_ref[...] = alpha * acc_ref[...] + weighted_values(pn, rows)

    o_t = acc_ref[...] / l_ref[...]
    y_t = o_t * lax.rsqrt(jnp.mean(o_t * o_t, axis=0, keepdims=True) + NORM_EPS)
    _store_heads_transposed(o_ref, y_t, g_ref, hb, blk)


def _moba(proj, bias, g, heads, q0, k0, v0):
    b, s, _ = proj.shape
    blk = MOBA_BLOCK
    nb = s // blk
    hb = math.gcd(4, heads, q0, k0, v0)
    wide = hb * HEAD_DIM
    bias = jnp.swapaxes(bias, -1, -2) * LOG2E
    seq = lambda c0: pl.BlockSpec((None, s, wide), lambda bi, h, i: (bi, 0, c0 // hb + h))
    return pl.pallas_call(
        functools.partial(_moba_kernel, nb=nb, hb=hb, scale=HEAD_DIM ** -0.5 * LOG2E),
        grid=(b, heads // hb, nb),
        in_specs=[
            pl.BlockSpec((None, blk, wide), lambda bi, h, i: (bi, i, q0 // hb + h)),
            seq(k0), seq(v0),
            pl.BlockSpec((hb, 3, blk, blk), lambda bi, h, i: (h, 0, 0, 0)),
            pl.BlockSpec((1, wide), lambda bi, h, i: (0, h)),
        ],
        out_specs=pl.BlockSpec((None, blk, wide), lambda bi, h, i: (bi, i, h)),
        out_shape=jax.ShapeDtypeStruct((b, s, heads * HEAD_DIM), BF16),
        scratch_shapes=[pltpu.VMEM((nb, wide), F32), pltpu.VMEM((1, hb * blk), F32),
                        pltpu.VMEM((1, hb * blk), F32), pltpu.VMEM((HEAD_DIM, hb * blk), F32)],
        compiler_params=_params(("parallel", "parallel", "arbitrary")),
        name="moba_attention",
    )(proj, proj, proj, bias, g.reshape(1, -1))


def _sb_kernel(q_ref, k_ref, v_ref, g_ref, o_ref, acc_ref, *, hb, scale):
    i = pl.program_id(2)
    blk = SB_BLOCK
    key = lax.broadcasted_iota(jnp.int32, (blk, hb * blk), 0)
    qry = lax.broadcasted_iota(jnp.int32, (blk, hb * blk), 1) & (blk - 1)
    rs = lax.broadcasted_iota(jnp.int32, (blk, blk), 0)
    cs = lax.broadcasted_iota(jnp.int32, (blk, blk), 1)
    suffix = jnp.where(cs >= rs, 1.0, 0.0).astype(BF16)

    def block(rows, carry, diagonal):
        z = _lane_heads(
            lambda h: _nt_dot(k_ref[rows, _head_cols(h)], q_ref[:, _head_cols(h)]), hb) * scale
        sp = jnp.maximum(z, 0.0) + jnp.log(1.0 + jnp.exp2(jnp.abs(z) * -LOG2E))
        if diagonal:
            strict = key < qry
            sp = jnp.where(strict, sp, 0.0)
        tot = _dot(suffix, sp.astype(BF16)) + carry
        a = jnp.exp(z - tot)
        if diagonal:
            a = jnp.where(strict, a, 0.0)
        a = a.astype(BF16)
        av = _lane_heads(
            lambda h: _tn_dot(v_ref[rows, _head_cols(h)], a[:, h * blk:(h + 1) * blk]), hb)
        if diagonal:
            acc_ref[...] = av
        else:
            acc_ref[...] += av
        return tot[0:1, :]

    own = pl.ds(pl.multiple_of(i * blk, blk), blk)
    carry = block(own, jnp.zeros((1, hb * blk), F32), True)

    def past(j):
        return pl.ds(pl.multiple_of(j * blk, blk), blk)

    odd = i & 1
    carry = lax.fori_loop(0, odd, lambda _, c: block(past(i - 1), c, False), carry)

    def pair(step, c):
        j = i - odd - 1 - 2 * step
        return block(past(j - 1), block(past(j), c, False), False)

    lax.fori_loop(0, i >> 1, pair, carry)
    acc = acc_ref[...]
    y_t = acc * lax.rsqrt(jnp.mean(acc * acc, axis=0, keepdims=True) + NORM_EPS)
    _store_heads_transposed(o_ref, y_t, g_ref, hb, blk)


def _stick_breaking(proj, g, heads, q0, k0, v0):
    b, s, _ = proj.shape
    blk = SB_BLOCK
    hb = math.gcd(4, heads, q0, k0, v0)
    wide = hb * HEAD_DIM
    seq = lambda c0: pl.BlockSpec((None, s, wide), lambda bi, h, i: (bi, 0, c0 // hb + h))
    return pl.pallas_call(
        functools.partial(_sb_kernel, hb=hb, scale=HEAD_DIM ** -0.5),
        grid=(b, heads // hb, s // blk),
        in_specs=[
            pl.BlockSpec((None, blk, wide), lambda bi, h, i: (bi, i, q0 // hb + h)),
            seq(k0), seq(v0),
            pl.BlockSpec((1, wide), lambda bi, h, i: (0, h)),
        ],
        out_specs=pl.BlockSpec((None, blk, wide), lambda bi, h, i: (bi, i, h)),
        out_shape=jax.ShapeDtypeStruct((b, s, heads * HEAD_DIM), BF16),
        scratch_shapes=[pltpu.VMEM((HEAD_DIM, hb * blk), F32)],
        compiler_params=_params(("parallel", "parallel", "parallel")),
        name="stick_breaking_attention",
    )(proj, proj, proj, g.reshape(1, -1))


def _swa_kernel(q_ref, kp_ref, kc_ref, vp_ref, vc_ref, bias_ref, sink_ref, g_ref, o_ref,
                *, group, scale):
    n = pl.program_id(2)
    w = SWA_WINDOW
    kk = jnp.concatenate([kp_ref[...], kc_ref[...]], axis=0)
    vv = jnp.concatenate([vp_ref[...], vc_ref[...]], axis=0)
    key = lax.broadcasted_iota(jnp.int32, (2 * w, group * w), 0)
    qry = lax.broadcasted_iota(jnp.int32, (2 * w, group * w), 1) & (w - 1)
    dist = qry + w - key
    mask = (dist >= 0) & (dist < w) & (key + n * w >= w)
    q = _stack_heads(lambda h: q_ref[:, _head_cols(h)], group)
    s = jnp.where(mask, _nt_dot(kk, q) * scale + bias_ref[...], NEG_INF)
    sink = sink_ref[...]
    m = jnp.maximum(jnp.max(s, axis=0, keepdims=True), sink)
    e = jnp.exp(s - m)
    denom = jnp.sum(e, axis=0, keepdims=True) + jnp.exp(sink - m)
    o_t = _tn_dot(vv, e.astype(BF16)) / denom
    y_t = o_t * lax.rsqrt(jnp.mean(o_t * o_t, axis=0, keepdims=True) + NORM_EPS)
    for h in range(group):
        y = y_t[:, h * w:(h + 1) * w].T * g_ref[:, _head_cols(h)]
        o_ref[:, _head_cols(h)] = y.astype(BF16)


def _swa(proj, bias, sinks, g, q_heads, kv_heads, q0, k0, v0):
    b, s, _ = proj.shape
    w = SWA_WINDOW
    group = q_heads // kv_heads
    bias = bias.reshape(kv_heads, group, w, 2 * w).transpose(0, 3, 1, 2).reshape(
        kv_heads, 2 * w, group * w)
    sink_rows = jnp.broadcast_to(sinks.astype(F32).reshape(kv_heads, 1, group, 1),
                                 (kv_heads, 1, group, w)).reshape(kv_heads, 1, group * w)
    kv = lambda c0, prev: pl.BlockSpec(
        (None, w, HEAD_DIM),
        lambda bi, kh, n: (bi, jnp.maximum(n - 1, 0) if prev else n, c0 + kh))
    return pl.pallas_call(
        functools.partial(_swa_kernel, group=group, scale=HEAD_DIM ** -0.5),
        grid=(b, kv_heads, s // w),
        in_specs=[
            pl.BlockSpec((None, w, group * HEAD_DIM), lambda bi, kh, n: (bi, n, q0 // group + kh)),
            kv(k0, True), kv(k0, False), kv(v0, True), kv(v0, False),
            pl.BlockSpec((None, 2 * w, group * w), lambda bi, kh, n: (kh, 0, 0)),
            pl.BlockSpec((None, 1, group * w), lambda bi, kh, n: (kh, 0, 0)),
            pl.BlockSpec((1, group * HEAD_DIM), lambda bi, kh, n: (0, kh)),
        ],
        out_specs=pl.BlockSpec((None, w, group * HEAD_DIM), lambda bi, kh, n: (bi, n, kh)),
        out_shape=jax.ShapeDtypeStruct((b, s, q_heads * HEAD_DIM), BF16),
        compiler_params=_params(("parallel", "parallel", "parallel")),
        name="swa_sink_attention",
    )(proj, proj, proj, proj, proj, bias, sink_rows, g.reshape(1, -1))


def _toeplitz(v, rows, cols):
    heads, length = v.shape
    t = jnp.tile(v, (1, rows))[:, :rows * (length - 1)].reshape(heads, rows, length - 1)
    return t[:, :, :cols]


def _bias_tiles(rel_bias, moba_heads, swa_heads):
    blk, w = MOBA_BLOCK, SWA_WINDOW
    length = 2 * blk
    k = jnp.arange(length)
    bd = rel_bias[_rel_bucket(k)].astype(F32).T
    mb, sw = bd[:moba_heads], bd[moba_heads:moba_heads + swa_heads]
    own = _toeplitz(mb[:, (-k) % length], blk, blk)
    adj = _toeplitz(mb[:, (blk - k) % length], blk, blk)
    far = jnp.broadcast_to(mb[:, length - 1][:, None, None], own.shape)
    swa = _toeplitz(sw[:, (w - k) % length], w, 2 * w)
    return jnp.stack([own, adj, far], axis=1), swa


def _head_counts(w_in, g_moba, g_sb, g_swa):
    moba_w, sb_w, swa_w = g_moba.shape[1], g_sb.shape[1], g_swa.shape[1]
    kv_w = (w_in.shape[2] - 3 * moba_w - 3 * sb_w - swa_w) // 2
    return moba_w // HEAD_DIM, sb_w // HEAD_DIM, swa_w // HEAD_DIM, kv_w // HEAD_DIM


def _mixing(h, bias_tiles, w_in, w_out, g_moba, g_sb, g_swa, sinks, layer):
    b, s, d = h.shape
    in_w = w_in.shape[2]
    mh, sh, qh, kvh = _head_counts(w_in, g_moba, g_sb, g_swa)
    assert qh % kvh == 0 and (3 * mh + 3 * sh) % (qh // kvh) == 0
    proj = _matmul(h.reshape(b * s, d), w_in, layer, BF16).reshape(b, s, in_w)
    o_a = _moba(proj, bias_tiles[0], g_moba[layer], mh, 0, mh, 2 * mh)
    o_b = _stick_breaking(proj, g_sb[layer], sh, 3 * mh, 3 * mh + sh, 3 * mh + 2 * sh)
    c0 = 3 * mh + 3 * sh
    o_c = _swa(proj, bias_tiles[1], sinks[layer], g_swa[layer], qh, kvh, c0, c0 + qh, c0 + qh + kvh)
    m = b * s
    return _out_proj(o_a.reshape(m, -1), o_b.reshape(m, -1), o_c.reshape(m, -1), w_out, layer)


def _row_gather_pipeline(n_live, fetch_rows, wait_rows, compute, idle=None):
    i = pl.program_id(0)

    @pl.when((i == 0) & (n_live > 0))
    def _():
        fetch_rows(0, 0)

    @pl.when(i + 1 < n_live)
    def _():
        fetch_rows(i + 1, (i + 1) % 2)

    @pl.when(i < n_live)
    def _():
        wait_rows(i % 2)
        compute(i % 2)

    if idle is not None:
        pl.when(i >= n_live)(idle)


def _gather_kernel(tok_ref, nlive_ref, src_ref, o_ref, buf_ref, sem_ref, *, rows):
    chunks = src_ref.shape[1]

    def copy(step, slot, group, sub):
        tok = tok_ref[step * rows + group * 8 + sub]
        return pltpu.make_async_copy(src_ref.at[tok], buf_ref.at[slot, group, :, sub, :],
                                     sem_ref.at[slot])

    def fetch_rows(step, slot):
        def start(group, c):
            for sub in range(8):
                copy(step, slot, group, sub).start()
            return c
        lax.fori_loop(0, rows // 8, start, 0)

    def wait_rows(slot):
        def wait(group, c):
            for sub in range(8):
                copy(0, slot, group, sub).wait()
            return c
        lax.fori_loop(0, rows // 8, wait, 0)

    def compute(slot):
        for j in range(chunks):
            o_ref[:, j * LANES:(j + 1) * LANES] = (
                buf_ref[slot, :, j].reshape(rows, LANES).astype(o_ref.dtype))

    def idle():
        o_ref[...] = jnp.zeros_like(o_ref)

    _row_gather_pipeline(nlive_ref[0], fetch_rows, wait_rows, compute, idle)


def _gather_rows(src, token_of, live_rows, rows=256):
    p = token_of.shape[0]
    t, d = src.shape
    chunks = d // LANES
    n_live = (live_rows // rows).astype(jnp.int32).reshape(1)
    return pl.pallas_call(
        functools.partial(_gather_kernel, rows=rows),
        grid_spec=pltpu.PrefetchScalarGridSpec(
            num_scalar_prefetch=2,
            grid=(p // rows,),
            in_specs=[pl.BlockSpec(memory_space=pl.ANY)],
            out_specs=pl.BlockSpec((rows, d), lambda i, tok, nl: (i, 0)),
            scratch_shapes=[pltpu.VMEM((2, rows // 8, chunks, 8, LANES), F32),
                            pltpu.SemaphoreType.DMA((2,))],
        ),
        out_shape=jax.ShapeDtypeStruct((p, d), BF16),
        compiler_params=_params(("arbitrary",)),
        name="moe_gather",
    )(token_of, n_live, src.reshape(t, chunks, LANES))


def _gmm_kernel(t_ref, ot_ref, oj_ref, live_ref, first_ref, slot_ref, e_ref, j_ref, nok_ref,
                ne_ref, nj_ref, a_ref, *rest, n_w, layer, tn, quant):
    w_hbm, o_ref, wbuf_ref, sem_ref = rest[:n_w], rest[n_w], rest[n_w + 1], rest[n_w + 2]
    s = pl.program_id(0)
    slot = slot_ref[s]
    tm = o_ref.shape[0]

    def weight_copies(e, j, to_slot):
        cols = pl.ds(pl.multiple_of(j * tn, tn), tn)
        return [pltpu.make_async_copy(w.at[layer, e, :, cols], wbuf_ref.at[to_slot, i],
                                      sem_ref.at[to_slot]) for i, w in enumerate(w_hbm)]

    @pl.when(s == 0)
    def _():
        for c in weight_copies(e_ref[0], j_ref[0], 0):
            c.start()

    @pl.when((first_ref[s] > 0) & (nok_ref[s] > 0))
    def _():
        for c in weight_copies(ne_ref[s], nj_ref[s], 1 - slot):
            c.start()

    @pl.when(first_ref[s] > 0)
    def _():
        for c in weight_copies(e_ref[s], j_ref[s], slot):
            c.wait()

    @pl.when(live_ref[s] == 0)
    def _():
        o_ref[...] = jnp.zeros_like(o_ref)

    for rows in range(quant, tm + 1, quant):
        @pl.when(live_ref[s] == rows)
        def _(rows=rows):
            a = a_ref[:rows]
            if n_w == 2:
                g = _dot(a, wbuf_ref[slot, 0].astype(BF16))
                u = _dot(a, wbuf_ref[slot, 1].astype(BF16))
                o_ref[:rows] = (g * jax.nn.sigmoid(g) * u).astype(o_ref.dtype)
            else:
                o_ref[:rows] = _dot(a, wbuf_ref[slot, 0].astype(BF16)).astype(o_ref.dtype)
            if rows < tm:
                o_ref[rows:] = jnp.zeros((tm - rows, o_ref.shape[1]), o_ref.dtype)


def _grouped_matmul(a, weights, layer, sched, out_dtype, tn):
    p, k = a.shape
    n = weights[0].shape[3]
    tm = MOE_ROW_TILE
    n_w = len(weights)
    steps = sched[0].shape[0]
    n_sched = len(sched)
    return pl.pallas_call(
        functools.partial(_gmm_kernel, n_w=n_w, layer=layer, tn=tn, quant=MOE_ROW_QUANT),
        grid_spec=pltpu.PrefetchScalarGridSpec(
            num_scalar_prefetch=n_sched,
            grid=(steps,),
            in_specs=[pl.BlockSpec((tm, k), lambda s, t, *_: (t[s], 0))]
            + [pl.BlockSpec(memory_space=pl.ANY)] * n_w,
            out_specs=pl.BlockSpec((tm, tn), lambda s, t, ot, oj, *_: (ot[s], oj[s])),
            scratch_shapes=[pltpu.VMEM((2, n_w, k, tn), F32), pltpu.SemaphoreType.DMA((2,))],
        ),
        out_shape=jax.ShapeDtypeStruct((p, n), out_dtype),
        compiler_params=_params(("arbitrary",)),
        name="moe_grouped_matmul",
    )(*sched, a, *weights)


def _combine_kernel(pos_ref, src_ref, w_ref, x_ref, gate_ref, gpost_ref, gnext_ref, sc_ref, sh_ref,
                    xo_ref, *rest, rows, tokens, emit_h):
    buf_ref, sem_ref = rest[-2:]

    def copy(step, slot, r, choice):
        return pltpu.make_async_copy(
            src_ref.at[pl.ds(pos_ref[choice * tokens + step * rows + r], 1)],
            buf_ref.at[slot, choice, pl.ds(r, 1)], sem_ref.at[slot])

    def fetch_rows(step, slot):
        def start(r, c):
            for choice in range(TOP_K):
                copy(step, slot, r, choice).start(priority=choice % 2)
            return c
        lax.fori_loop(0, rows, start, 0, unroll=4)

    def wait_rows(slot):
        def wait(r, c):
            for choice in range(TOP_K):
                copy(0, slot, r, choice).wait()
            return c
        lax.fori_loop(0, rows, wait, 0, unroll=4)

    def compute(slot):
        wts = w_ref[...]
        y = wts[:, 0:1] * buf_ref[slot, 0] + wts[:, 1:2] * buf_ref[slot, 1]
        x_new = x_ref[...] + gate_ref[...] * (_rms(y) * gpost_ref[...])
        xo_ref[...] = x_new
        if emit_h:
            h = (_rms(x_new) * gnext_ref[...]) * (1.0 + sc_ref[...]) + sh_ref[...]
            rest[0][...] = h.astype(BF16)

    _row_gather_pipeline(pl.num_programs(0), fetch_rows, wait_rows, compute)


def _combine_post(expert_out, pos, wts, x, gate, g_post, g_next=None, scale=None, shift=None,
                  rows=256):
    b, s, d = x.shape
    t = b * s
    per_b = s // rows
    emit_h = g_next is not None
    if not emit_h:
        g_next, scale, shift = g_post, gate, gate
    row = pl.BlockSpec((rows, d), lambda i, pos: (i, 0))
    vec = pl.BlockSpec((1, d), lambda i, pos: (0, 0))
    per_batch = pl.BlockSpec((None, 1, d), lambda i, pos: (i // per_b, 0, 0))
    out_specs = [row] + ([row] if emit_h else [])
    out_shape = [jax.ShapeDtypeStruct((t, d), F32)] + (
        [jax.ShapeDtypeStruct((t, d), BF16)] if emit_h else [])
    outs = pl.pallas_call(
        functools.partial(_combine_kernel, rows=rows, tokens=t, emit_h=emit_h),
        grid_spec=pltpu.PrefetchScalarGridSpec(
            num_scalar_prefetch=1,
            grid=(t // rows,),
            in_specs=[pl.BlockSpec(memory_space=pl.ANY),
                      pl.BlockSpec((rows, LANES), lambda i, pos: (i, 0)),
                      row, per_batch, vec, vec, per_batch, per_batch],
            out_specs=out_specs,
            scratch_shapes=[pltpu.VMEM((2, TOP_K, rows, d), F32), pltpu.SemaphoreType.DMA((2,))],
        ),
        out_shape=out_shape,
        compiler_params=_params(("arbitrary",)),
        name="moe_combine_post",
    )(pos, expert_out, wts, x.reshape(t, d), gate, g_post.reshape(1, d), g_next.reshape(1, d),
      scale, shift)
    return [o.reshape(b, s, d) for o in outs]


def _route_plan(sel, comb, n_exp):
    t = sel.shape[0]
    tm = MOE_ROW_TILE
    n_tiles = (t * TOP_K) // tm + n_exp
    mask = sel > 0.5
    cnt = jnp.sum(mask, axis=0).astype(jnp.int32)
    tiles_e = (cnt + tm - 1) // tm
    tile_end = jnp.cumsum(tiles_e)
    tile_start = tile_end - tiles_e
    rank = jnp.cumsum(mask, axis=0).astype(jnp.int32) - 1
    pos_te = tile_start[None, :] * tm + rank
    order = jnp.argsort(jnp.logical_not(mask), axis=1, stable=True)[:, :TOP_K].astype(jnp.int32)
    pos = jnp.take_along_axis(pos_te, order, axis=1)
    wts = jnp.take_along_axis(comb, order, axis=1)
    token_of = jnp.zeros((n_tiles * tm,), jnp.int32).at[pos.reshape(-1)].set(
        jnp.repeat(jnp.arange(t, dtype=jnp.int32), TOP_K))
    wts_pad = jnp.zeros((t, LANES), F32).at[:, :TOP_K].set(wts)

    def schedule(col_tiles):
        steps = jnp.arange(n_tiles * col_tiles, dtype=jnp.int32)
        live = tile_end[-1] * col_tiles
        last = jnp.maximum(live - 1, 0)
        steps_c = jnp.minimum(steps, last)
        e_of = jnp.minimum(jnp.searchsorted(tile_end * col_tiles, steps_c, side="right"),
                           n_exp - 1).astype(jnp.int32)
        local = steps_c - tile_start[e_of] * col_tiles
        n_e = jnp.maximum(tiles_e[e_of], 1)
        j_of = (local // n_e).astype(jnp.int32)
        t_of = (tile_start[e_of] + local % n_e).astype(jnp.int32)
        n_dead = jnp.maximum(n_tiles - tile_end[-1], 1)
        dead = jnp.maximum(steps - live, 0)
        is_live = steps < live
        ot_of = jnp.where(is_live, t_of, tile_end[-1] + dead % n_dead).astype(jnp.int32)
        oj_of = jnp.where(is_live, j_of, dead // n_dead).astype(jnp.int32)
        in_tile = jnp.clip(cnt[e_of] - (local % n_e) * tm, 0, tm)
        quant = MOE_ROW_QUANT
        rows_live = jnp.where(is_live, (in_tile + quant - 1) // quant * quant, 0).astype(jnp.int32)
        first = is_live & (local % n_e == 0)
        slot = jnp.where(is_live, (jnp.cumsum(first) - 1) % 2, 0).astype(jnp.int32)
        nxt = steps + n_e
        nxt_ok = first & (nxt < live)
        nxt_c = jnp.minimum(nxt, last)
        as_i32 = lambda v: v.astype(jnp.int32)
        return (t_of, ot_of, oj_of, rows_live, as_i32(first), slot, e_of, j_of, as_i32(nxt_ok),
                e_of[nxt_c], j_of[nxt_c])

    return token_of, tile_end[-1] * tm, schedule, pos.T.reshape(-1), wts_pad


def _col_tile(n, pref):
    while n % pref:
        pref //= 2
    return pref


def _moe(h_f32, comb, sel, w_gate, w_up, w_down, layer_idx):
    t, d = h_f32.shape
    n_exp = w_gate.shape[1]
    d_exp = w_gate.shape[3]
    tn_up, tn_down = _col_tile(d_exp, 512), _col_tile(d, 1024)
    token_of, live_rows, schedule, pos, wts = _route_plan(sel[:, :n_exp], comb[:, :n_exp], n_exp)
    hs = _gather_rows(h_f32, token_of, live_rows)
    act = _grouped_matmul(hs, (w_gate, w_up), layer_idx, schedule(d_exp // tn_up), BF16, tn_up)
    out = _grouped_matmul(act, (w_down,), layer_idx, schedule(d // tn_down), F32, tn_down)
    return out, pos, wts


def kernel(x, c, rel_bias, w_ada, b_ada, g_pre_mix, w_in, g_grp_moba, g_grp_sb, g_grp_swa, swa_sinks, w_out, g_post_mix, g_pre_ffn, w_ff_gate, w_ff_up, w_ff_down, w_router, w_moe_gate, w_moe_up, w_moe_down, g_post_ffn):
    b, s, d = x.shape
    depth = w_ada.shape[0]
    m = b * s
    mod = _ada_mod(c, w_ada, b_ada).reshape(depth, b, ADA_CHUNKS, 1, d)
    chunk = lambda layer, idx: mod[layer, :, idx]
    h = _prenorm(x, g_pre_mix[0], chunk(0, 1), chunk(0, 0))
    mh, _, qh, _ = _head_counts(w_in, g_grp_moba, g_grp_sb, g_grp_swa)
    bias_tiles = _bias_tiles(rel_bias, mh, qh)
    for layer in range(depth):
        shift_f, scale_f, gate_f = chunk(layer, 3), chunk(layer, 4), chunk(layer, 5)
        y = _mixing(h, bias_tiles, w_in, w_out, g_grp_moba, g_grp_sb, g_grp_swa, swa_sinks, layer)
        idx = layer // 2
        dense = layer % 2 == 0
        nxt = ()
        if layer + 1 < depth:
            nxt = (g_pre_mix[layer + 1], chunk(layer + 1, 1), chunk(layer + 1, 0))
        if dense:
            x, h = _post(y, x, chunk(layer, 2), g_post_mix[layer], g_pre_ffn[layer], scale_f, shift_f)
            act = _swiglu_up(h.reshape(m, d), w_ff_gate, w_ff_up, idx)
            y = _matmul_ktiled(act, w_ff_down, idx)
            outs = _post(y, x, gate_f, g_post_ffn[layer], *nxt)
        else:
            x, hf, comb, sel = _post(y, x, chunk(layer, 2), g_post_mix[layer], g_pre_ffn[layer],
                                     scale_f, shift_f, w_router=w_router[idx])
            expert_out, pos, wts = _moe(hf.reshape(m, d), comb.reshape(m, LANES),
                                        sel.reshape(m, LANES), w_moe_gate, w_moe_up, w_moe_down, idx)
            outs = _combine_post(expert_out, pos, wts, x, gate_f, g_post_ffn[layer], *nxt)
        x = outs[0]
        if nxt:
            h = outs[1]
    return x
```

```python
import functools
import math

import jax
import jax.numpy as jnp
from jax import lax
from jax.experimental import pallas as pl
from jax.experimental.pallas import tpu as pltpu

HEAD_DIM = 128
MOBA_BLOCK = 256
MOBA_TOPK = 3
SB_BLOCK = 256
SWA_WINDOW = 128
REL_BUCKETS = 32
REL_MAX_EXACT = 16
REL_MAX_DISTANCE = 128
NORM_EPS = 1e-6
ADA_CHUNKS = 6
TOP_K = 2
LANES = 128
MOE_ROW_TILE = 512
MOE_ROW_QUANT = 128
VMEM_LIMIT = 56 * 1024 * 1024

F32 = jnp.float32
BF16 = jnp.bfloat16
NEG_INF = float("-inf")
LOG2E = math.log2(math.e)


def _params(sem, vmem=VMEM_LIMIT):
    return pltpu.CompilerParams(dimension_semantics=sem, vmem_limit_bytes=vmem)


def _nt_dot(a, b):
    return lax.dot_general(a, b, (((1,), (1,)), ((), ())), preferred_element_type=F32)


def _dot(a, b):
    return jnp.dot(a, b, preferred_element_type=F32)


def _tn_dot(a, b):
    return lax.dot_general(a, b, (((0,), (0,)), ((), ())), preferred_element_type=F32)


def _split_bf16(x):
    hi = x.astype(BF16)
    lo = (x - hi.astype(F32)).astype(BF16)
    return hi, lo


def _rms(x):
    return x * lax.rsqrt(jnp.mean(x * x, axis=-1, keepdims=True) + NORM_EPS)


def _ada_kernel(c_ref, w_ref, b_ref, o_ref):
    c = c_ref[...]
    sc = c * jax.nn.sigmoid(c)
    hi, lo = _split_bf16(sc)
    w = w_ref[...].astype(BF16)
    o_ref[...] = _dot(hi, w) + _dot(lo, w) + b_ref[...]


def _ada_mod(c, w_ada, b_ada):
    depth, d, n = w_ada.shape
    b = c.shape[0]
    rows = 8
    c_pad = jnp.zeros((rows, d), F32).at[:b].set(c)
    tn = _col_tile(n, 512)
    out = pl.pallas_call(
        _ada_kernel,
        grid=(depth, n // tn),
        in_specs=[
            pl.BlockSpec((rows, d), lambda l, j: (0, 0)),
            pl.BlockSpec((None, d, tn), lambda l, j: (l, 0, j)),
            pl.BlockSpec((None, 1, tn), lambda l, j: (l, 0, j)),
        ],
        out_specs=pl.BlockSpec((None, rows, tn), lambda l, j: (l, 0, j)),
        out_shape=jax.ShapeDtypeStruct((depth, rows, n), F32),
        compiler_params=_params(("parallel", "parallel")),
        name="ada_mod",
    )(c_pad, w_ada, b_ada.reshape(depth, 1, n))
    return out[:, :b]


def _prenorm_kernel(x_ref, g_ref, sc_ref, sh_ref, h_ref):
    x = x_ref[...]
    h = (_rms(x) * g_ref[...]) * (1.0 + sc_ref[...]) + sh_ref[...]
    h_ref[...] = h.astype(h_ref.dtype)


def _prenorm(x, g, scale, shift, ts=256):
    b, s, d = x.shape
    return pl.pallas_call(
        _prenorm_kernel,
        grid=(b, s // ts),
        in_specs=[
            pl.BlockSpec((None, ts, d), lambda i, j: (i, j, 0)),
            pl.BlockSpec((1, d), lambda i, j: (0, 0)),
            pl.BlockSpec((None, 1, d), lambda i, j: (i, 0, 0)),
            pl.BlockSpec((None, 1, d), lambda i, j: (i, 0, 0)),
        ],
        out_specs=pl.BlockSpec((None, ts, d), lambda i, j: (i, j, 0)),
        out_shape=jax.ShapeDtypeStruct((b, s, d), BF16),
        compiler_params=_params(("parallel", "parallel")),
        name="prenorm",
    )(x, g.reshape(1, d), scale, shift)


def _post_kernel(y_ref, x_ref, gate_ref, gpost_ref, gnext_ref, sc_ref, sh_ref, *rest,
                 emit_h, route):
    if route:
        wr_ref, rest = rest[0], rest[1:]
    xo_ref, rest = rest[0], rest[1:]
    x_new = x_ref[...] + gate_ref[...] * (_rms(y_ref[...]) * gpost_ref[...])
    xo_ref[...] = x_new
    if not emit_h:
        return
    h = (_rms(x_new) * gnext_ref[...]) * (1.0 + sc_ref[...]) + sh_ref[...]
    if not route:
        rest[0][...] = h.astype(BF16)
        return
    hf_ref, comb_ref, sel_ref = rest
    hf_ref[...] = h
    hi, lo = _split_bf16(h)
    w = wr_ref[...]
    whi, wlo = _split_bf16(w)
    logits = _dot(hi, whi) + _dot(lo, whi) + _dot(hi, wlo)
    n_exp = route
    lane = lax.broadcasted_iota(jnp.int32, logits.shape, 1)
    lg = jnp.where(lane < n_exp, logits, NEG_INF)
    m1 = jnp.max(lg, axis=1, keepdims=True)
    i1 = jnp.min(jnp.where(lg == m1, lane, LANES), axis=1, keepdims=True)
    lg2 = jnp.where(lane == i1, NEG_INF, lg)
    m2 = jnp.max(lg2, axis=1, keepdims=True)
    i2 = jnp.min(jnp.where(lg2 == m2, lane, LANES), axis=1, keepdims=True)
    e2 = jnp.exp(m2 - m1)
    w1 = 1.0 / (1.0 + e2)
    w2 = e2 / (1.0 + e2)
    comb_ref[...] = jnp.where(lane == i1, w1, 0.0) + jnp.where(lane == i2, w2, 0.0)
    sel_ref[...] = jnp.where((lane == i1) | (lane == i2), 1.0, 0.0)


def _post(y, x, gate, g_post, g_next=None, scale=None, shift=None, w_router=None, ts=256):
    b, s, d = x.shape
    emit_h = g_next is not None
    n_exp = 0 if w_router is None else w_router.shape[1]
    if not emit_h:
        g_next, scale, shift = g_post, gate, gate
    row = pl.BlockSpec((None, ts, d), lambda i, j: (i, j, 0))
    vec = pl.BlockSpec((1, d), lambda i, j: (0, 0))
    per_b = pl.BlockSpec((None, 1, d), lambda i, j: (i, 0, 0))
    in_specs = [row, row, per_b, vec, vec, per_b, per_b]
    args = [y.reshape(b, s, d), x, gate, g_post.reshape(1, d), g_next.reshape(1, d), scale, shift]
    out_specs = [row]
    out_shape = [jax.ShapeDtypeStruct((b, s, d), F32)]
    if n_exp:
        wr = jnp.zeros((d, LANES), F32).at[:, :n_exp].set(w_router.astype(F32))
        in_specs.append(pl.BlockSpec((d, LANES), lambda i, j: (0, 0)))
        args.append(wr)
        lane_blk = pl.BlockSpec((None, ts, LANES), lambda i, j: (i, j, 0))
        out_specs += [row, lane_blk, lane_blk]
        out_shape += [jax.ShapeDtypeStruct((b, s, d), F32),
                      jax.ShapeDtypeStruct((b, s, LANES), F32),
                      jax.ShapeDtypeStruct((b, s, LANES), F32)]
    elif emit_h:
        out_specs.append(row)
        out_shape.append(jax.ShapeDtypeStruct((b, s, d), BF16))
    return pl.pallas_call(
        functools.partial(_post_kernel, emit_h=emit_h, route=n_exp),
        grid=(b, s // ts),
        in_specs=in_specs,
        out_specs=out_specs,
        out_shape=out_shape,
        compiler_params=_params(("parallel", "parallel")),
        name="post_norm_residual",
    )(*args)


def _mm_kernel(a_ref, w_ref, o_ref):
    o_ref[...] = _dot(a_ref[...], w_ref[...].astype(BF16)).astype(o_ref.dtype)


def _matmul(a, w, layer, out_dtype, tm=1024, tn=512):
    m, k = a.shape
    n = w.shape[2]
    tm, tn = _col_tile(m, tm), _col_tile(n, tn)
    return pl.pallas_call(
        _mm_kernel,
        grid=(n // tn, m // tm),
        in_specs=[
            pl.BlockSpec((tm, k), lambda j, i: (i, 0)),
            pl.BlockSpec((None, k, tn), lambda j, i: (layer, 0, j)),
        ],
        out_specs=pl.BlockSpec((tm, tn), lambda j, i: (i, j)),
        out_shape=jax.ShapeDtypeStruct((m, n), out_dtype),
        compiler_params=_params(("parallel", "parallel")),
        name="matmul",
    )(a, w)


def _out_proj_kernel(oa_ref, ob_ref, oc_ref, w_ref, y_ref, *, cuts):
    c1, c2 = cuts
    w = w_ref[...].astype(BF16)
    y = _dot(oa_ref[...], w[:c1]) + _dot(ob_ref[...], w[c1:c2]) + _dot(oc_ref[...], w[c2:])
    y_ref[...] = y


def _out_proj(oa, ob, oc, w, layer, tm=1024, tn=512):
    m = oa.shape[0]
    k, n = w.shape[1], w.shape[2]
    c1 = oa.shape[1]
    c2 = c1 + ob.shape[1]
    tm, tn = _col_tile(m, tm), _col_tile(n, tn)
    return pl.pallas_call(
        functools.partial(_out_proj_kernel, cuts=(c1, c2)),
        grid=(n // tn, m // tm),
        in_specs=[
            pl.BlockSpec((tm, oa.shape[1]), lambda j, i: (i, 0)),
            pl.BlockSpec((tm, ob.shape[1]), lambda j, i: (i, 0)),
            pl.BlockSpec((tm, oc.shape[1]), lambda j, i: (i, 0)),
            pl.BlockSpec((None, k, tn), lambda j, i: (layer, 0, j)),
        ],
        out_specs=pl.BlockSpec((tm, tn), lambda j, i: (i, j)),
        out_shape=jax.ShapeDtypeStruct((m, n), F32),
        compiler_params=_params(("parallel", "parallel")),
        name="out_proj",
    )(oa, ob, oc, w)


def _swiglu_up_kernel(a_ref, wg_ref, wu_ref, o_ref):
    a = a_ref[...]
    g = _dot(a, wg_ref[...].astype(BF16))
    u = _dot(a, wu_ref[...].astype(BF16))
    o_ref[...] = (g * jax.nn.sigmoid(g) * u).astype(o_ref.dtype)


def _swiglu_up(a, w_gate, w_up, layer, tm=1024, tn=256):
    m, k = a.shape
    n = w_gate.shape[2]
    tm, tn = _col_tile(m, tm), _col_tile(n, tn)
    wspec = pl.BlockSpec((None, k, tn), lambda j, i: (layer, 0, j))
    return pl.pallas_call(
        _swiglu_up_kernel,
        grid=(n // tn, m // tm),
        in_specs=[pl.BlockSpec((tm, k), lambda j, i: (i, 0)), wspec, wspec],
        out_specs=pl.BlockSpec((tm, tn), lambda j, i: (i, j)),
        out_shape=jax.ShapeDtypeStruct((m, n), BF16),
        compiler_params=_params(("parallel", "parallel")),
        name="swiglu_up",
    )(a, w_gate, w_up)


def _mm_acc_kernel(a_ref, w_ref, o_ref):
    @pl.when(pl.program_id(2) == 0)
    def _():
        o_ref[...] = jnp.zeros_like(o_ref)

    o_ref[...] += _dot(a_ref[...], w_ref[...].astype(BF16))


def _matmul_ktiled(a, w, layer, tm=2048, tn=1024, tk=1024):
    m, k = a.shape
    n = w.shape[2]
    tm, tn, tk = _col_tile(m, tm), _col_tile(n, tn), _col_tile(k, tk)
    return pl.pallas_call(
        _mm_acc_kernel,
        grid=(n // tn, m // tm, k // tk),
        in_specs=[
            pl.BlockSpec((tm, tk), lambda j, i, kk: (i, kk)),
            pl.BlockSpec((None, tk, tn), lambda j, i, kk: (layer, kk, j)),
        ],
        out_specs=pl.BlockSpec((tm, tn), lambda j, i, kk: (i, j)),
        out_shape=jax.ShapeDtypeStruct((m, n), F32),
        compiler_params=_params(("parallel", "parallel", "arbitrary")),
        name="matmul_ktiled",
    )(a, w)


def _rel_bucket(dist):
    n = jnp.maximum(dist, 0)
    nf = jnp.maximum(n, 1).astype(F32)
    large = REL_MAX_EXACT + (jnp.log(nf / REL_MAX_EXACT) / math.log(REL_MAX_DISTANCE / REL_MAX_EXACT)
                             * (REL_BUCKETS - REL_MAX_EXACT)).astype(jnp.int32)
    return jnp.where(n < REL_MAX_EXACT, n, jnp.minimum(large, REL_BUCKETS - 1))


def _head_norm(o, g):
    return (_rms(o) * g).astype(BF16)


def _head_cols(h):
    return slice(h * HEAD_DIM, (h + 1) * HEAD_DIM)


def _stack_heads(fn, heads):
    return jnp.concatenate([fn(h) for h in range(heads)], axis=0)


def _lane_heads(fn, heads):
    return jnp.concatenate([fn(h) for h in range(heads)], axis=1)


def _store_heads_transposed(o_ref, y_t, g_ref, heads, rows):
    for h in range(heads):
        y = y_t[:, h * rows:(h + 1) * rows].T * g_ref[:, _head_cols(h)]
        o_ref[:, _head_cols(h)] = y.astype(o_ref.dtype)


def _moba_kernel(q_ref, k_ref, v_ref, bias_ref, g_ref, o_ref, kmean_ref, m_ref, l_ref, acc_ref,
                 *, nb, hb, scale):
    i = pl.program_id(2)
    blk = MOBA_BLOCK

    @pl.when(i == 0)
    def _():
        kf = k_ref[...].astype(F32).reshape(nb, blk, hb * HEAD_DIM)
        kmean_ref[...] = jnp.mean(kf, axis=1)

    khi, klo = _split_bf16(kmean_ref[...])

    def gate_of(h):
        q = q_ref[:, _head_cols(h)]
        return _nt_dot(khi[:, _head_cols(h)], q) + _nt_dot(klo[:, _head_cols(h)], q)

    gate = _lane_heads(gate_of, hb)
    blk_id = lax.broadcasted_iota(jnp.int32, gate.shape, 0)
    gate = jnp.where(blk_id < i, gate, NEG_INF)
    sel = jnp.zeros(gate.shape, F32)
    for _ in range(min(MOBA_TOPK, nb)):
        m = jnp.max(gate, axis=0, keepdims=True)
        first = jnp.min(jnp.where((gate == m) & (m > NEG_INF), blk_id, nb), axis=0, keepdims=True)
        pick = blk_id == first
        sel = jnp.where(pick, 1.0, sel)
        gate = jnp.where(pick, NEG_INF, gate)

    def scores(rows, bias_idx):
        return _lane_heads(
            lambda h: _nt_dot(k_ref[rows, _head_cols(h)], q_ref[:, _head_cols(h)]) * scale
            + bias_ref[h, bias_idx], hb)

    def weighted_values(p, rows):
        pb = p.astype(BF16)
        return _lane_heads(
            lambda h: _tn_dot(v_ref[rows, _head_cols(h)], pb[:, h * blk:(h + 1) * blk]), hb)

    key = lax.broadcasted_iota(jnp.int32, (blk, hb * blk), 0)
    qry = lax.broadcasted_iota(jnp.int32, (blk, hb * blk), 1) & (blk - 1)
    own = pl.ds(pl.multiple_of(i * blk, blk), blk)
    s = jnp.where(key <= qry, scores(own, 0), NEG_INF)
    m0 = jnp.max(s, axis=0, keepdims=True)
    p = jnp.exp2(s - m0)
    m_ref[...] = m0
    l_ref[...] = jnp.sum(p, axis=0, keepdims=True)
    acc_ref[...] = weighted_values(p, own)

    def past_blocks(n0, count):
        rows = slice(n0 * blk, (n0 + count) * blk)

        def head_scores(h):
            raw = _nt_dot(k_ref[rows, _head_cols(h)], q_ref[:, _head_cols(h)]) * scale
            bias = [bias_ref[h, jnp.minimum(i - (n0 + t), 2)] for t in range(count)]
            return raw + jnp.concatenate(bias, axis=0)

        keep = jnp.concatenate(
            [jnp.broadcast_to(sel[n0 + t:n0 + t + 1, :], (blk, hb * blk)) for t in range(count)],
            axis=0)
        sn = jnp.where(keep > 0.0, _lane_heads(head_scores, hb), NEG_INF)
        m_old = m_ref[...]
        m_new = jnp.maximum(m_old, jnp.max(sn, axis=0, keepdims=True))
        alpha = jnp.exp2(m_old - m_new)
        pn = jnp.exp2(sn - m_new)
        m_ref[...] = m_new
        l_ref[...] = alpha * l_ref[...] + jnp.sum(pn, axis=0, keepdims=True)
        acc_ref[...] = alpha * acc_ref[...] + weighted_values(pn, rows)

    for n in range(0, nb - 1, 2):
        if n + 1 < nb - 1:
            pl.when(n + 1 < i)(functools.partial(past_blocks, n, 2))
        pl.when(n == i - 1)(functools.partial(past_blocks, n, 1))

    o_t = acc_ref[...] / l_ref[...]
    y_t = o_t * lax.rsqrt(jnp.mean(o_t * o_t, axis=0, keepdims=True) + NORM_EPS)
    _store_heads_transposed(o_ref, y_t, g_ref, hb, blk)


def _moba(proj, bias, g, heads, q0, k0, v0):
    b, s, _ = proj.shape
    blk = MOBA_BLOCK
    nb = s // blk
    hb = math.gcd(4, heads, q0, k0, v0)
    wide = hb * HEAD_DIM
    bias = jnp.swapaxes(bias, -1, -2) * LOG2E
    seq = lambda c0: pl.BlockSpec((None, s, wide), lambda bi, h, i: (bi, 0, c0 // hb + h))
    return pl.pallas_call(
        functools.partial(_moba_kernel, nb=nb, hb=hb, scale=HEAD_DIM ** -0.5 * LOG2E),
        grid=(b, heads // hb, nb),
        in_specs=[
            pl.BlockSpec((None, blk, wide), lambda bi, h, i: (bi, i, q0 // hb + h)),
            seq(k0), seq(v0),
            pl.BlockSpec((hb, 3, blk, blk), lambda bi, h, i: (h, 0, 0, 0)),
            pl.BlockSpec((1, wide), lambda bi, h, i: (0, h)),
        ],
        out_specs=pl.BlockSpec((None, blk, wide), lambda bi, h, i: (bi, i, h)),
        out_shape=jax.ShapeDtypeStruct((b, s, heads * HEAD_DIM), BF16),
        scratch_shapes=[pltpu.VMEM((nb, wide), F32), pltpu.VMEM((1, hb * blk), F32),
                        pltpu.VMEM((1, hb * blk), F32), pltpu.VMEM((HEAD_DIM, hb * blk), F32)],
        compiler_params=_params(("parallel", "parallel", "arbitrary")),
        name="moba_attention",
    )(proj, proj, proj, bias, g.reshape(1, -1))


def _sb_kernel(q_ref, k_ref, v_ref, g_ref, o_ref, acc_ref, *, hb, scale):
    i = pl.program_id(2)
    blk = SB_BLOCK
    key = lax.broadcasted_iota(jnp.int32, (blk, hb * blk), 0)
    qry = lax.broadcasted_iota(jnp.int32, (blk, hb * blk), 1) & (blk - 1)
    rs = lax.broadcasted_iota(jnp.int32, (blk, blk), 0)
    cs = lax.broadcasted_iota(jnp.int32, (blk, blk), 1)
    suffix = jnp.where(cs >= rs, 1.0, 0.0).astype(BF16)

    def block(rows, carry, diagonal):
        z = _lane_heads(
            lambda h: _nt_dot(k_ref[rows, _head_cols(h)], q_ref[:, _head_cols(h)]), hb) * scale
        zb = z.astype(BF16)
        sp = jnp.maximum(zb, 0.0) + jnp.log(1.0 + jnp.exp2(jnp.abs(zb) * -LOG2E))
        if diagonal:
            strict = key < qry
            sp = jnp.where(strict, sp, 0.0)
        tot = _dot(suffix, sp.astype(BF16)) + carry
        a = jnp.exp(z - tot)
        if diagonal:
            a = jnp.where(strict, a, 0.0)
        a = a.astype(BF16)
        av = _lane_heads(
            lambda h: _tn_dot(v_ref[rows, _head_cols(h)], a[:, h * blk:(h + 1) * blk]), hb)
        if diagonal:
            acc_ref[...] = av
        else:
            acc_ref[...] += av
        return tot[0:1, :]

    own = pl.ds(pl.multiple_of(i * blk, blk), blk)
    carry = block(own, jnp.zeros((1, hb * blk), F32), True)

    def past(j):
        return pl.ds(pl.multiple_of(j * blk, blk), blk)

    group = 4
    rem = i % group
    carry = lax.fori_loop(0, rem, lambda t, c: block(past(i - 1 - t), c, False), carry)

    def several(step, c):
        j = i - rem - 1 - group * step
        for t in range(group):
            c = block(past(j - t), c, False)
        return c

    lax.fori_loop(0, i // group, several, carry)
    acc = acc_ref[...]
    y_t = acc * lax.rsqrt(jnp.mean(acc * acc, axis=0, keepdims=True) + NORM_EPS)
    _store_heads_transposed(o_ref, y_t, g_ref, hb, blk)


def _stick_breaking(proj, g, heads, q0, k0, v0):
    b, s, _ = proj.shape
    blk = SB_BLOCK
    hb = math.gcd(4, heads, q0, k0, v0)
    wide = hb * HEAD_DIM
    seq = lambda c0: pl.BlockSpec((None, s, wide), lambda bi, h, i: (bi, 0, c0 // hb + h))
    return pl.pallas_call(
        functools.partial(_sb_kernel, hb=hb, scale=HEAD_DIM ** -0.5),
        grid=(b, heads // hb, s // blk),
        in_specs=[
            pl.BlockSpec((None, blk, wide), lambda bi, h, i: (bi, i, q0 // hb + h)),
            seq(k0), seq(v0),
            pl.BlockSpec((1, wide), lambda bi, h, i: (0, h)),
        ],
        out_specs=pl.BlockSpec((None, blk, wide), lambda bi, h, i: (bi, i, h)),
        out_shape=jax.ShapeDtypeStruct((b, s, heads * HEAD_DIM), BF16),
        scratch_shapes=[pltpu.VMEM((HEAD_DIM, hb * blk), F32)],
        compiler_params=_params(("parallel", "parallel", "parallel")),
        name="stick_breaking_attention",
    )(proj, proj, proj, g.reshape(1, -1))


def _swa_kernel(q_ref, kp_ref, kc_ref, vp_ref, vc_ref, bias_ref, sink_ref, g_ref, o_ref,
                *, group, scale):
    n = pl.program_id(2)
    w = SWA_WINDOW
    kk = jnp.concatenate([kp_ref[...], kc_ref[...]], axis=0)
    vv = jnp.concatenate([vp_ref[...], vc_ref[...]], axis=0)
    key = lax.broadcasted_iota(jnp.int32, (2 * w, group * w), 0)
    qry = lax.broadcasted_iota(jnp.int32, (2 * w, group * w), 1) & (w - 1)
    dist = qry + w - key
    mask = (dist >= 0) & (dist < w) & (key + n * w >= w)
    q = _stack_heads(lambda h: q_ref[:, _head_cols(h)], group)
    s = jnp.where(mask, _nt_dot(kk, q) * scale + bias_ref[...], NEG_INF)
    sink = sink_ref[...]
    m = jnp.maximum(jnp.max(s, axis=0, keepdims=True), sink)
    e = jnp.exp(s - m)
    denom = jnp.sum(e, axis=0, keepdims=True) + jnp.exp(sink - m)
    o_t = _tn_dot(vv, e.astype(BF16)) / denom
    y_t = o_t * lax.rsqrt(jnp.mean(o_t * o_t, axis=0, keepdims=True) + NORM_EPS)
    for h in range(group):
        y = y_t[:, h * w:(h + 1) * w].T * g_ref[:, _head_cols(h)]
        o_ref[:, _head_cols(h)] = y.astype(BF16)


def _swa(proj, bias, sinks, g, q_heads, kv_heads, q0, k0, v0):
    b, s, _ = proj.shape
    w = SWA_WINDOW
    group = q_heads // kv_heads
    bias = bias.reshape(kv_heads, group, w, 2 * w).transpose(0, 3, 1, 2).reshape(
        kv_heads, 2 * w, group * w)
    sink_rows = jnp.broadcast_to(sinks.astype(F32).reshape(kv_heads, 1, group, 1),
                                 (kv_heads, 1, group, w)).reshape(kv_heads, 1, group * w)
    kv = lambda c0, prev: pl.BlockSpec(
        (None, w, HEAD_DIM),
        lambda bi, kh, n: (bi, jnp.maximum(n - 1, 0) if prev else n, c0 + kh))
    return pl.pallas_call(
        functools.partial(_swa_kernel, group=group, scale=HEAD_DIM ** -0.5),
        grid=(b, kv_heads, s // w),
        in_specs=[
            pl.BlockSpec((None, w, group * HEAD_DIM), lambda bi, kh, n: (bi, n, q0 // group + kh)),
            kv(k0, True), kv(k0, False), kv(v0, True), kv(v0, False),
            pl.BlockSpec((None, 2 * w, group * w), lambda bi, kh, n: (kh, 0, 0)),
            pl.BlockSpec((None, 1, group * w), lambda bi, kh, n: (kh, 0, 0)),
            pl.BlockSpec((1, group * HEAD_DIM), lambda bi, kh, n: (0, kh)),
        ],
        out_specs=pl.BlockSpec((None, w, group * HEAD_DIM), lambda bi, kh, n: (bi, n, kh)),
        out_shape=jax.ShapeDtypeStruct((b, s, q_heads * HEAD_DIM), BF16),
        compiler_params=_params(("parallel", "parallel", "parallel")),
        name="swa_sink_attention",
    )(proj, proj, proj, proj, proj, bias, sink_rows, g.reshape(1, -1))


def _toeplitz(v, rows, cols):
    heads, length = v.shape
    t = jnp.tile(v, (1, rows))[:, :rows * (length - 1)].reshape(heads, rows, length - 1)
    return t[:, :, :cols]


def _bias_tiles(rel_bias, moba_heads, swa_heads):
    blk, w = MOBA_BLOCK, SWA_WINDOW
    length = 2 * blk
    k = jnp.arange(length)
    bd = rel_bias[_rel_bucket(k)].astype(F32).T
    mb, sw = bd[:moba_heads], bd[moba_heads:moba_heads + swa_heads]
    own = _toeplitz(mb[:, (-k) % length], blk, blk)
    adj = _toeplitz(mb[:, (blk - k) % length], blk, blk)
    far = jnp.broadcast_to(mb[:, length - 1][:, None, None], own.shape)
    swa = _toeplitz(sw[:, (w - k) % length], w, 2 * w)
    return jnp.stack([own, adj, far], axis=1), swa


def _head_counts(w_in, g_moba, g_sb, g_swa):
    moba_w, sb_w, swa_w = g_moba.shape[1], g_sb.shape[1], g_swa.shape[1]
    kv_w = (w_in.shape[2] - 3 * moba_w - 3 * sb_w - swa_w) // 2
    return moba_w // HEAD_DIM, sb_w // HEAD_DIM, swa_w // HEAD_DIM, kv_w // HEAD_DIM


def _mixing(h, bias_tiles, w_in, w_out, g_moba, g_sb, g_swa, sinks, layer):
    b, s, d = h.shape
    in_w = w_in.shape[2]
    mh, sh, qh, kvh = _head_counts(w_in, g_moba, g_sb, g_swa)
    assert qh % kvh == 0 and (3 * mh + 3 * sh) % (qh // kvh) == 0
    proj = _matmul(h.reshape(b * s, d), w_in, layer, BF16).reshape(b, s, in_w)
    o_a = _moba(proj, bias_tiles[0], g_moba[layer], mh, 0, mh, 2 * mh)
    o_b = _stick_breaking(proj, g_sb[layer], sh, 3 * mh, 3 * mh + sh, 3 * mh + 2 * sh)
    c0 = 3 * mh + 3 * sh
    o_c = _swa(proj, bias_tiles[1], sinks[layer], g_swa[layer], qh, kvh, c0, c0 + qh, c0 + qh + kvh)
    m = b * s
    return _out_proj(o_a.reshape(m, -1), o_b.reshape(m, -1), o_c.reshape(m, -1), w_out, layer)


def _row_gather_pipeline(n_live, fetch_rows, wait_rows, compute, idle=None):
    i = pl.program_id(0)

    @pl.when((i == 0) & (n_live > 0))
    def _():
        fetch_rows(0, 0)

    @pl.when(i + 1 < n_live)
    def _():
        fetch_rows(i + 1, (i + 1) % 2)

    @pl.when(i < n_live)
    def _():
        wait_rows(i % 2)
        compute(i % 2)

    if idle is not None:
        pl.when(i >= n_live)(idle)


def _gather_kernel(tok_ref, nlive_ref, src_ref, o_ref, buf_ref, sem_ref, *, rows):
    chunks = src_ref.shape[1]

    def copy(step, slot, group, sub):
        tok = tok_ref[step * rows + group * 8 + sub]
        return pltpu.make_async_copy(src_ref.at[tok], buf_ref.at[slot, group, :, sub, :],
                                     sem_ref.at[slot])

    def fetch_rows(step, slot):
        def start(group, c):
            for sub in range(8):
                copy(step, slot, group, sub).start()
            return c
        lax.fori_loop(0, rows // 8, start, 0)

    def wait_rows(slot):
        def wait(group, c):
            for sub in range(8):
                copy(0, slot, group, sub).wait()
            return c
        lax.fori_loop(0, rows // 8, wait, 0)

    def compute(slot):
        for j in range(chunks):
            o_ref[:, j * LANES:(j + 1) * LANES] = (
                buf_ref[slot, :, j].reshape(rows, LANES).astype(o_ref.dtype))

    def idle():
        o_ref[...] = jnp.zeros_like(o_ref)

    _row_gather_pipeline(nlive_ref[0], fetch_rows, wait_rows, compute, idle)


def _gather_rows(src, token_of, live_rows, rows=256):
    p = token_of.shape[0]
    t, d = src.shape
    chunks = d // LANES
    n_live = (live_rows // rows).astype(jnp.int32).reshape(1)
    return pl.pallas_call(
        functools.partial(_gather_kernel, rows=rows),
        grid_spec=pltpu.PrefetchScalarGridSpec(
            num_scalar_prefetch=2,
            grid=(p // rows,),
            in_specs=[pl.BlockSpec(memory_space=pl.ANY)],
            out_specs=pl.BlockSpec((rows, d), lambda i, tok, nl: (i, 0)),
            scratch_shapes=[pltpu.VMEM((2, rows // 8, chunks, 8, LANES), F32),
                            pltpu.SemaphoreType.DMA((2,))],
        ),
        out_shape=jax.ShapeDtypeStruct((p, d), BF16),
        compiler_params=_params(("arbitrary",)),
        name="moe_gather",
    )(token_of, n_live, src.reshape(t, chunks, LANES))


def _gmm_kernel(t_ref, ot_ref, oj_ref, live_ref, first_ref, slot_ref, e_ref, j_ref, nok_ref,
                ne_ref, nj_ref, a_ref, *rest, n_w, layer, tn, quant):
    w_hbm, o_ref, wbuf_ref, sem_ref = rest[:n_w], rest[n_w], rest[n_w + 1], rest[n_w + 2]
    s = pl.program_id(0)
    slot = slot_ref[s]
    tm = o_ref.shape[0]

    def weight_copies(e, j, to_slot):
        cols = pl.ds(pl.multiple_of(j * tn, tn), tn)
        return [pltpu.make_async_copy(w.at[layer, e, :, cols], wbuf_ref.at[to_slot, i],
                                      sem_ref.at[to_slot]) for i, w in enumerate(w_hbm)]

    @pl.when(s == 0)
    def _():
        for c in weight_copies(e_ref[0], j_ref[0], 0):
            c.start()

    @pl.when((first_ref[s] > 0) & (nok_ref[s] > 0))
    def _():
        for c in weight_copies(ne_ref[s], nj_ref[s], 1 - slot):
            c.start()

    @pl.when(first_ref[s] > 0)
    def _():
        for c in weight_copies(e_ref[s], j_ref[s], slot):
            c.wait()

    @pl.when(live_ref[s] == 0)
    def _():
        o_ref[...] = jnp.zeros_like(o_ref)

    for rows in range(quant, tm + 1, quant):
        @pl.when(live_ref[s] == rows)
        def _(rows=rows):
            a = a_ref[:rows]
            if n_w == 2:
                g = _dot(a, wbuf_ref[slot, 0].astype(BF16))
                u = _dot(a, wbuf_ref[slot, 1].astype(BF16))
                o_ref[:rows] = (g * jax.nn.sigmoid(g) * u).astype(o_ref.dtype)
            else:
                o_ref[:rows] = _dot(a, wbuf_ref[slot, 0].astype(BF16)).astype(o_ref.dtype)
            if rows < tm:
                o_ref[rows:] = jnp.zeros((tm - rows, o_ref.shape[1]), o_ref.dtype)


def _grouped_matmul(a, weights, layer, sched, out_dtype, tn):
    p, k = a.shape
    n = weights[0].shape[3]
    tm = MOE_ROW_TILE
    n_w = len(weights)
    steps = sched[0].shape[0]
    n_sched = len(sched)
    return pl.pallas_call(
        functools.partial(_gmm_kernel, n_w=n_w, layer=layer, tn=tn, quant=MOE_ROW_QUANT),
        grid_spec=pltpu.PrefetchScalarGridSpec(
            num_scalar_prefetch=n_sched,
            grid=(steps,),
            in_specs=[pl.BlockSpec((tm, k), lambda s, t, *_: (t[s], 0))]
            + [pl.BlockSpec(memory_space=pl.ANY)] * n_w,
            out_specs=pl.BlockSpec((tm, tn), lambda s, t, ot, oj, *_: (ot[s], oj[s])),
            scratch_shapes=[pltpu.VMEM((2, n_w, k, tn), F32), pltpu.SemaphoreType.DMA((2,))],
        ),
        out_shape=jax.ShapeDtypeStruct((p, n), out_dtype),
        compiler_params=_params(("arbitrary",)),
        name="moe_grouped_matmul",
    )(*sched, a, *weights)


def _combine_kernel(pos_ref, src_ref, w_ref, x_ref, gate_ref, gpost_ref, gnext_ref, sc_ref, sh_ref,
                    xo_ref, *rest, rows, tokens, emit_h):
    buf_ref, sem_ref = rest[-2:]

    def copy(step, slot, r, choice):
        return pltpu.make_async_copy(
            src_ref.at[pl.ds(pos_ref[choice * tokens + step * rows + r], 1)],
            buf_ref.at[slot, choice, pl.ds(r, 1)], sem_ref.at[slot])

    def fetch_rows(step, slot):
        def start(r, c):
            for choice in range(TOP_K):
                copy(step, slot, r, choice).start(priority=choice % 2)
            return c
        lax.fori_loop(0, rows, start, 0, unroll=4)

    def wait_rows(slot):
        def wait(r, c):
            for choice in range(TOP_K):
                copy(0, slot, r, choice).wait()
            return c
        lax.fori_loop(0, rows, wait, 0, unroll=4)

    def compute(slot):
        wts = w_ref[...]
        y = wts[:, 0:1] * buf_ref[slot, 0] + wts[:, 1:2] * buf_ref[slot, 1]
        x_new = x_ref[...] + gate_ref[...] * (_rms(y) * gpost_ref[...])
        xo_ref[...] = x_new
        if emit_h:
            h = (_rms(x_new) * gnext_ref[...]) * (1.0 + sc_ref[...]) + sh_ref[...]
            rest[0][...] = h.astype(BF16)

    _row_gather_pipeline(pl.num_programs(0), fetch_rows, wait_rows, compute)


def _combine_post(expert_out, pos, wts, x, gate, g_post, g_next=None, scale=None, shift=None,
                  rows=256):
    b, s, d = x.shape
    t = b * s
    per_b = s // rows
    emit_h = g_next is not None
    if not emit_h:
        g_next, scale, shift = g_post, gate, gate
    row = pl.BlockSpec((rows, d), lambda i, pos: (i, 0))
    vec = pl.BlockSpec((1, d), lambda i, pos: (0, 0))
    per_batch = pl.BlockSpec((None, 1, d), lambda i, pos: (i // per_b, 0, 0))
    out_specs = [row] + ([row] if emit_h else [])
    out_shape = [jax.ShapeDtypeStruct((t, d), F32)] + (
        [jax.ShapeDtypeStruct((t, d), BF16)] if emit_h else [])
    outs = pl.pallas_call(
        functools.partial(_combine_kernel, rows=rows, tokens=t, emit_h=emit_h),
        grid_spec=pltpu.PrefetchScalarGridSpec(
            num_scalar_prefetch=1,
            grid=(t // rows,),
            in_specs=[pl.BlockSpec(memory_space=pl.ANY),
                      pl.BlockSpec((rows, LANES), lambda i, pos: (i, 0)),
                      row, per_batch, vec, vec, per_batch, per_batch],
            out_specs=out_specs,
            scratch_shapes=[pltpu.VMEM((2, TOP_K, rows, d), F32), pltpu.SemaphoreType.DMA((2,))],
        ),
        out_shape=out_shape,
        compiler_params=_params(("arbitrary",)),
        name="moe_combine_post",
    )(pos, expert_out, wts, x.reshape(t, d), gate, g_post.reshape(1, d), g_next.reshape(1, d),
      scale, shift)
    return [o.reshape(b, s, d) for o in outs]


def _route_plan(sel, comb, n_exp):
    t = sel.shape[0]
    tm = MOE_ROW_TILE
    n_tiles = (t * TOP_K) // tm + n_exp
    mask = sel > 0.5
    cnt = jnp.sum(mask, axis=0).astype(jnp.int32)
    tiles_e = (cnt + tm - 1) // tm
    tile_end = jnp.cumsum(tiles_e)
    tile_start = tile_end - tiles_e
    rank = jnp.cumsum(mask, axis=0).astype(jnp.int32) - 1
    pos_te = tile_start[None, :] * tm + rank
    order = jnp.argsort(jnp.logical_not(mask), axis=1, stable=True)[:, :TOP_K].astype(jnp.int32)
    pos = jnp.take_along_axis(pos_te, order, axis=1)
    wts = jnp.take_along_axis(comb, order, axis=1)
    token_of = jnp.zeros((n_tiles * tm,), jnp.int32).at[pos.reshape(-1)].set(
        jnp.repeat(jnp.arange(t, dtype=jnp.int32), TOP_K))
    wts_pad = jnp.zeros((t, LANES), F32).at[:, :TOP_K].set(wts)

    def schedule(col_tiles):
        steps = jnp.arange(n_tiles * col_tiles, dtype=jnp.int32)
        live = tile_end[-1] * col_tiles
        last = jnp.maximum(live - 1, 0)
        steps_c = jnp.minimum(steps, last)
        e_of = jnp.minimum(jnp.sum(steps_c[:, None] >= (tile_end * col_tiles)[None, :], axis=1),
                           n_exp - 1).astype(jnp.int32)
        local = steps_c - tile_start[e_of] * col_tiles
        n_e = jnp.maximum(tiles_e[e_of], 1)
        j_of = (local // n_e).astype(jnp.int32)
        t_of = (tile_start[e_of] + local % n_e).astype(jnp.int32)
        n_dead = jnp.maximum(n_tiles - tile_end[-1], 1)
        dead = jnp.maximum(steps - live, 0)
        is_live = steps < live
        ot_of = jnp.where(is_live, t_of, tile_end[-1] + dead % n_dead).astype(jnp.int32)
        oj_of = jnp.where(is_live, j_of, dead // n_dead).astype(jnp.int32)
        in_tile = jnp.clip(cnt[e_of] - (local % n_e) * tm, 0, tm)
        quant = MOE_ROW_QUANT
        rows_live = jnp.where(is_live, (in_tile + quant - 1) // quant * quant, 0).astype(jnp.int32)
        first = is_live & (local % n_e == 0)
        slot = jnp.where(is_live, (jnp.cumsum(first) - 1) % 2, 0).astype(jnp.int32)
        nxt = steps + n_e
        nxt_ok = first & (nxt < live)
        nxt_c = jnp.minimum(nxt, last)
        as_i32 = lambda v: v.astype(jnp.int32)
        return (t_of, ot_of, oj_of, rows_live, as_i32(first), slot, e_of, j_of, as_i32(nxt_ok),
                e_of[nxt_c], j_of[nxt_c])

    return token_of, tile_end[-1] * tm, schedule, pos.T.reshape(-1), wts_pad


def _col_tile(n, pref):
    while n % pref:
        pref //= 2
    return pref


def _moe(h_f32, comb, sel, w_gate, w_up, w_down, layer_idx):
    t, d = h_f32.shape
    n_exp = w_gate.shape[1]
    d_exp = w_gate.shape[3]
    tn_up, tn_down = _col_tile(d_exp, 512), _col_tile(d, 1024)
    token_of, live_rows, schedule, pos, wts = _route_plan(sel[:, :n_exp], comb[:, :n_exp], n_exp)
    hs = _gather_rows(h_f32, token_of, live_rows)
    act = _grouped_matmul(hs, (w_gate, w_up), layer_idx, schedule(d_exp // tn_up), BF16, tn_up)
    out = _grouped_matmul(act, (w_down,), layer_idx, schedule(d // tn_down), F32, tn_down)
    return out, pos, wts


def kernel(x, c, rel_bias, w_ada, b_ada, g_pre_mix, w_in, g_grp_moba, g_grp_sb, g_grp_swa, swa_sinks, w_out, g_post_mix, g_pre_ffn, w_ff_gate, w_ff_up, w_ff_down, w_router, w_moe_gate, w_moe_up, w_moe_down, g_post_ffn):
    b, s, d = x.shape
    depth = w_ada.shape[0]
    m = b * s
    mod = _ada_mod(c, w_ada, b_ada).reshape(depth, b, ADA_CHUNKS, 1, d)
    chunk = lambda layer, idx: mod[layer, :, idx]
    h = _prenorm(x, g_pre_mix[0], chunk(0, 1), chunk(0, 0))
    mh, _, qh, _ = _head_counts(w_in, g_grp_moba, g_grp_sb, g_grp_swa)
    bias_tiles = _bias_tiles(rel_bias, mh, qh)
    for layer in range(depth):
        shift_f, scale_f, gate_f = chunk(layer, 3), chunk(layer, 4), chunk(layer, 5)
        y = _mixing(h, bias_tiles, w_in, w_out, g_grp_moba, g_grp_sb, g_grp_swa, swa_sinks, layer)
        idx = layer // 2
        dense = layer % 2 == 0
        nxt = ()
        if layer + 1 < depth:
            nxt = (g_pre_mix[layer + 1], chunk(layer + 1, 1), chunk(layer + 1, 0))
        if dense:
            x, h = _post(y, x, chunk(layer, 2), g_post_mix[layer], g_pre_ffn[layer], scale_f, shift_f)
            act = _swiglu_up(h.reshape(m, d), w_ff_gate, w_ff_up, idx)
            y = _matmul_ktiled(act, w_ff_down, idx)
            outs = _post(y, x, gate_f, g_post_ffn[layer], *nxt)
        else:
            x, hf, comb, sel = _post(y, x, chunk(layer, 2), g_post_mix[layer], g_pre_ffn[layer],
                                     scale_f, shift_f, w_router=w_router[idx])
            expert_out, pos, wts = _moe(hf.reshape(m, d), comb.reshape(m, LANES),
                                        sel.reshape(m, LANES), w_moe_gate, w_moe_up, w_moe_down, idx)
            outs = _combine_post(expert_out, pos, wts, x, gate_f, g_post_ffn[layer], *nxt)
        x = outs[0]
        if nxt:
            h = outs[1]
    return x
```

```python
import functools
import math

import jax
import jax.numpy as jnp
from jax import lax
from jax.experimental import pallas as pl
from jax.experimental.pallas import tpu as pltpu

HEAD_DIM = 128
MOBA_BLOCK = 256
MOBA_TOPK = 3
SB_BLOCK = 256
SWA_WINDOW = 128
REL_BUCKETS = 32
REL_MAX_EXACT = 16
REL_MAX_DISTANCE = 128
NORM_EPS = 1e-6
ADA_CHUNKS = 6
TOP_K = 2
LANES = 128
MOE_ROW_TILE = 512
MOE_ROW_QUANT = 128
VMEM_LIMIT = 56 * 1024 * 1024

F32 = jnp.float32
BF16 = jnp.bfloat16
NEG_INF = float("-inf")
LOG2E = math.log2(math.e)


def _params(sem, vmem=VMEM_LIMIT):
    return pltpu.CompilerParams(dimension_semantics=sem, vmem_limit_bytes=vmem)


def _nt_dot(a, b):
    return lax.dot_general(a, b, (((1,), (1,)), ((), ())), preferred_element_type=F32)


def _dot(a, b):
    return jnp.dot(a, b, preferred_element_type=F32)


def _tn_dot(a, b):
    return lax.dot_general(a, b, (((0,), (0,)), ((), ())), preferred_element_type=F32)


def _split_bf16(x):
    hi = x.astype(BF16)
    lo = (x - hi.astype(F32)).astype(BF16)
    return hi, lo


def _rms(x):
    return x * lax.rsqrt(jnp.mean(x * x, axis=-1, keepdims=True) + NORM_EPS)


def _ada_kernel(c_ref, w_ref, b_ref, o_ref):
    c = c_ref[...]
    sc = c * jax.nn.sigmoid(c)
    hi, lo = _split_bf16(sc)
    w = w_ref[...].astype(BF16)
    o_ref[...] = _dot(hi, w) + _dot(lo, w) + b_ref[...]


def _ada_mod(c, w_ada, b_ada):
    depth, d, n = w_ada.shape
    b = c.shape[0]
    rows = 8
    c_pad = jnp.zeros((rows, d), F32).at[:b].set(c)
    tn = _col_tile(n, 512)
    out = pl.pallas_call(
        _ada_kernel,
        grid=(depth, n // tn),
        in_specs=[
            pl.BlockSpec((rows, d), lambda l, j: (0, 0)),
            pl.BlockSpec((None, d, tn), lambda l, j: (l, 0, j)),
            pl.BlockSpec((None, 1, tn), lambda l, j: (l, 0, j)),
        ],
        out_specs=pl.BlockSpec((None, rows, tn), lambda l, j: (l, 0, j)),
        out_shape=jax.ShapeDtypeStruct((depth, rows, n), F32),
        compiler_params=_params(("parallel", "parallel")),
        name="ada_mod",
    )(c_pad, w_ada, b_ada.reshape(depth, 1, n))
    return out[:, :b]


def _prenorm_kernel(x_ref, g_ref, sc_ref, sh_ref, h_ref):
    x = x_ref[...]
    h = (_rms(x) * g_ref[...]) * (1.0 + sc_ref[...]) + sh_ref[...]
    h_ref[...] = h.astype(h_ref.dtype)


def _prenorm(x, g, scale, shift, ts=256):
    b, s, d = x.shape
    return pl.pallas_call(
        _prenorm_kernel,
        grid=(b, s // ts),
        in_specs=[
            pl.BlockSpec((None, ts, d), lambda i, j: (i, j, 0)),
            pl.BlockSpec((1, d), lambda i, j: (0, 0)),
            pl.BlockSpec((None, 1, d), lambda i, j: (i, 0, 0)),
            pl.BlockSpec((None, 1, d), lambda i, j: (i, 0, 0)),
        ],
        out_specs=pl.BlockSpec((None, ts, d), lambda i, j: (i, j, 0)),
        out_shape=jax.ShapeDtypeStruct((b, s, d), BF16),
        compiler_params=_params(("parallel", "parallel")),
        name="prenorm",
    )(x, g.reshape(1, d), scale, shift)


def _post_kernel(y_ref, x_ref, gate_ref, gpost_ref, gnext_ref, sc_ref, sh_ref, *rest,
                 emit_h, route):
    if route:
        wr_ref, rest = rest[0], rest[1:]
    xo_ref, rest = rest[0], rest[1:]
    x_new = x_ref[...] + gate_ref[...] * (_rms(y_ref[...]) * gpost_ref[...])
    xo_ref[...] = x_new
    if not emit_h:
        return
    h = (_rms(x_new) * gnext_ref[...]) * (1.0 + sc_ref[...]) + sh_ref[...]
    if not route:
        rest[0][...] = h.astype(BF16)
        return
    hf_ref, comb_ref, sel_ref = rest
    hf_ref[...] = h
    hi, lo = _split_bf16(h)
    w = wr_ref[...]
    whi, wlo = _split_bf16(w)
    logits = _dot(hi, whi) + _dot(lo, whi) + _dot(hi, wlo)
    n_exp = route
    lane = lax.broadcasted_iota(jnp.int32, logits.shape, 1)
    lg = jnp.where(lane < n_exp, logits, NEG_INF)
    m1 = jnp.max(lg, axis=1, keepdims=True)
    i1 = jnp.min(jnp.where(lg == m1, lane, LANES), axis=1, keepdims=True)
    lg2 = jnp.where(lane == i1, NEG_INF, lg)
    m2 = jnp.max(lg2, axis=1, keepdims=True)
    i2 = jnp.min(jnp.where(lg2 == m2, lane, LANES), axis=1, keepdims=True)
    e2 = jnp.exp(m2 - m1)
    w1 = 1.0 / (1.0 + e2)
    w2 = e2 / (1.0 + e2)
    comb_ref[...] = jnp.where(lane == i1, w1, 0.0) + jnp.where(lane == i2, w2, 0.0)
    sel_ref[...] = jnp.where((lane == i1) | (lane == i2), 1.0, 0.0)


def _post(y, x, gate, g_post, g_next=None, scale=None, shift=None, w_router=None, ts=256):
    b, s, d = x.shape
    emit_h = g_next is not None
    n_exp = 0 if w_router is None else w_router.shape[1]
    if not emit_h:
        g_next, scale, shift = g_post, gate, gate
    row = pl.BlockSpec((None, ts, d), lambda i, j: (i, j, 0))
    vec = pl.BlockSpec((1, d), lambda i, j: (0, 0))
    per_b = pl.BlockSpec((None, 1, d), lambda i, j: (i, 0, 0))
    in_specs = [row, row, per_b, vec, vec, per_b, per_b]
    args = [y.reshape(b, s, d), x, gate, g_post.reshape(1, d), g_next.reshape(1, d), scale, shift]
    out_specs = [row]
    out_shape = [jax.ShapeDtypeStruct((b, s, d), F32)]
    if n_exp:
        wr = jnp.zeros((d, LANES), F32).at[:, :n_exp].set(w_router.astype(F32))
        in_specs.append(pl.BlockSpec((d, LANES), lambda i, j: (0, 0)))
        args.append(wr)
        lane_blk = pl.BlockSpec((None, ts, LANES), lambda i, j: (i, j, 0))
        out_specs += [row, lane_blk, lane_blk]
        out_shape += [jax.ShapeDtypeStruct((b, s, d), F32),
                      jax.ShapeDtypeStruct((b, s, LANES), F32),
                      jax.ShapeDtypeStruct((b, s, LANES), F32)]
    elif emit_h:
        out_specs.append(row)
        out_shape.append(jax.ShapeDtypeStruct((b, s, d), BF16))
    return pl.pallas_call(
        functools.partial(_post_kernel, emit_h=emit_h, route=n_exp),
        grid=(b, s // ts),
        in_specs=in_specs,
        out_specs=out_specs,
        out_shape=out_shape,
        compiler_params=_params(("parallel", "parallel")),
        name="post_norm_residual",
    )(*args)


def _mm_kernel(a_ref, w_ref, o_ref):
    o_ref[...] = _dot(a_ref[...], w_ref[...].astype(BF16)).astype(o_ref.dtype)


def _matmul(a, w, layer, out_dtype, tm=1024, tn=512):
    m, k = a.shape
    n = w.shape[2]
    tm, tn = _col_tile(m, tm), _col_tile(n, tn)
    return pl.pallas_call(
        _mm_kernel,
        grid=(n // tn, m // tm),
        in_specs=[
            pl.BlockSpec((tm, k), lambda j, i: (i, 0)),
            pl.BlockSpec((None, k, tn), lambda j, i: (layer, 0, j)),
        ],
        out_specs=pl.BlockSpec((tm, tn), lambda j, i: (i, j)),
        out_shape=jax.ShapeDtypeStruct((m, n), out_dtype),
        compiler_params=_params(("parallel", "parallel")),
        name="matmul",
    )(a, w)


def _out_proj_kernel(oa_ref, ob_ref, oc_ref, w_ref, y_ref, *, cuts):
    c1, c2 = cuts
    w = w_ref[...].astype(BF16)
    y = _dot(oa_ref[...], w[:c1]) + _dot(ob_ref[...], w[c1:c2]) + _dot(oc_ref[...], w[c2:])
    y_ref[...] = y


def _out_proj(oa, ob, oc, w, layer, tm=1024, tn=512):
    m = oa.shape[0]
    k, n = w.shape[1], w.shape[2]
    c1 = oa.shape[1]
    c2 = c1 + ob.shape[1]
    tm, tn = _col_tile(m, tm), _col_tile(n, tn)
    return pl.pallas_call(
        functools.partial(_out_proj_kernel, cuts=(c1, c2)),
        grid=(n // tn, m // tm),
        in_specs=[
            pl.BlockSpec((tm, oa.shape[1]), lambda j, i: (i, 0)),
            pl.BlockSpec((tm, ob.shape[1]), lambda j, i: (i, 0)),
            pl.BlockSpec((tm, oc.shape[1]), lambda j, i: (i, 0)),
            pl.BlockSpec((None, k, tn), lambda j, i: (layer, 0, j)),
        ],
        out_specs=pl.BlockSpec((tm, tn), lambda j, i: (i, j)),
        out_shape=jax.ShapeDtypeStruct((m, n), F32),
        compiler_params=_params(("parallel", "parallel")),
        name="out_proj",
    )(oa, ob, oc, w)


def _swiglu_up_kernel(a_ref, wg_ref, wu_ref, o_ref):
    a = a_ref[...]
    g = _dot(a, wg_ref[...].astype(BF16))
    u = _dot(a, wu_ref[...].astype(BF16))
    o_ref[...] = (g * jax.nn.sigmoid(g) * u).astype(o_ref.dtype)


def _swiglu_up(a, w_gate, w_up, layer, tm=1024, tn=256):
    m, k = a.shape
    n = w_gate.shape[2]
    tm, tn = _col_tile(m, tm), _col_tile(n, tn)
    wspec = pl.BlockSpec((None, k, tn), lambda j, i: (layer, 0, j))
    return pl.pallas_call(
        _swiglu_up_kernel,
        grid=(n // tn, m // tm),
        in_specs=[pl.BlockSpec((tm, k), lambda j, i: (i, 0)), wspec, wspec],
        out_specs=pl.BlockSpec((tm, tn), lambda j, i: (i, j)),
        out_shape=jax.ShapeDtypeStruct((m, n), BF16),
        compiler_params=_params(("parallel", "parallel")),
        name="swiglu_up",
    )(a, w_gate, w_up)


def _mm_acc_kernel(a_ref, w_ref, o_ref):
    @pl.when(pl.program_id(2) == 0)
    def _():
        o_ref[...] = jnp.zeros_like(o_ref)

    o_ref[...] += _dot(a_ref[...], w_ref[...].astype(BF16))


def _matmul_ktiled(a, w, layer, tm=2048, tn=1024, tk=1024):
    m, k = a.shape
    n = w.shape[2]
    tm, tn, tk = _col_tile(m, tm), _col_tile(n, tn), _col_tile(k, tk)
    return pl.pallas_call(
        _mm_acc_kernel,
        grid=(n // tn, m // tm, k // tk),
        in_specs=[
            pl.BlockSpec((tm, tk), lambda j, i, kk: (i, kk)),
            pl.BlockSpec((None, tk, tn), lambda j, i, kk: (layer, kk, j)),
        ],
        out_specs=pl.BlockSpec((tm, tn), lambda j, i, kk: (i, j)),
        out_shape=jax.ShapeDtypeStruct((m, n), F32),
        compiler_params=_params(("parallel", "parallel", "arbitrary")),
        name="matmul_ktiled",
    )(a, w)


def _rel_bucket(dist):
    n = jnp.maximum(dist, 0)
    nf = jnp.maximum(n, 1).astype(F32)
    large = REL_MAX_EXACT + (jnp.log(nf / REL_MAX_EXACT) / math.log(REL_MAX_DISTANCE / REL_MAX_EXACT)
                             * (REL_BUCKETS - REL_MAX_EXACT)).astype(jnp.int32)
    return jnp.where(n < REL_MAX_EXACT, n, jnp.minimum(large, REL_BUCKETS - 1))


def _head_norm(o, g):
    return (_rms(o) * g).astype(BF16)


def _head_cols(h):
    return slice(h * HEAD_DIM, (h + 1) * HEAD_DIM)


def _stack_heads(fn, heads):
    return jnp.concatenate([fn(h) for h in range(heads)], axis=0)


def _lane_heads(fn, heads):
    return jnp.concatenate([fn(h) for h in range(heads)], axis=1)


def _store_heads_transposed(o_ref, y_t, g_ref, heads, rows):
    for h in range(heads):
        y = y_t[:, h * rows:(h + 1) * rows].T * g_ref[:, _head_cols(h)]
        o_ref[:, _head_cols(h)] = y.astype(o_ref.dtype)


def _moba_kernel(q_ref, k_ref, v_ref, bias_ref, g_ref, o_ref, kmean_ref, m_ref, l_ref, acc_ref,
                 *, nb, hb, scale):
    i = pl.program_id(2)
    blk = MOBA_BLOCK

    @pl.when(i == 0)
    def _():
        kf = k_ref[...].astype(F32).reshape(nb, blk, hb * HEAD_DIM)
        kmean_ref[...] = jnp.mean(kf, axis=1)

    khi, klo = _split_bf16(kmean_ref[...])

    def gate_of(h):
        q = q_ref[:, _head_cols(h)]
        return _nt_dot(khi[:, _head_cols(h)], q) + _nt_dot(klo[:, _head_cols(h)], q)

    gate = _lane_heads(gate_of, hb)
    blk_id = lax.broadcasted_iota(jnp.int32, gate.shape, 0)
    gate = jnp.where(blk_id < i, gate, NEG_INF)
    sel = jnp.zeros(gate.shape, F32)
    for _ in range(min(MOBA_TOPK, nb)):
        m = jnp.max(gate, axis=0, keepdims=True)
        first = jnp.min(jnp.where((gate == m) & (m > NEG_INF), blk_id, nb), axis=0, keepdims=True)
        pick = blk_id == first
        sel = jnp.where(pick, 1.0, sel)
        gate = jnp.where(pick, NEG_INF, gate)

    def scores(rows, bias_idx):
        return _lane_heads(
            lambda h: _nt_dot(k_ref[rows, _head_cols(h)], q_ref[:, _head_cols(h)]) * scale
            + bias_ref[h, bias_idx], hb)

    def weighted_values(p, rows):
        pb = p.astype(BF16)
        return _lane_heads(
            lambda h: _tn_dot(v_ref[rows, _head_cols(h)], pb[:, h * blk:(h + 1) * blk]), hb)

    key = lax.broadcasted_iota(jnp.int32, (blk, hb * blk), 0)
    qry = lax.broadcasted_iota(jnp.int32, (blk, hb * blk), 1) & (blk - 1)
    own = pl.ds(pl.multiple_of(i * blk, blk), blk)
    s = jnp.where(key <= qry, scores(own, 0), NEG_INF)
    m0 = jnp.max(s, axis=0, keepdims=True)
    p = jnp.exp2(s - m0)
    m_ref[...] = m0
    l_ref[...] = jnp.sum(p, axis=0, keepdims=True)
    acc_ref[...] = weighted_values(p, own)

    def past_blocks(n0, count):
        rows = slice(n0 * blk, (n0 + count) * blk)

        def head_scores(h):
            raw = _nt_dot(k_ref[rows, _head_cols(h)], q_ref[:, _head_cols(h)]) * scale
            bias = [bias_ref[h, jnp.minimum(i - (n0 + t), 2)] for t in range(count)]
            return raw + jnp.concatenate(bias, axis=0)

        keep = jnp.concatenate(
            [jnp.broadcast_to(sel[n0 + t:n0 + t + 1, :], (blk, hb * blk)) for t in range(count)],
            axis=0)
        sn = jnp.where(keep > 0.0, _lane_heads(head_scores, hb), NEG_INF)
        m_old = m_ref[...]
        m_new = jnp.maximum(m_old, jnp.max(sn, axis=0, keepdims=True))
        alpha = jnp.exp2(m_old - m_new)
        pn = jnp.exp2(sn - m_new)
        m_ref[...] = m_new
        l_ref[...] = alpha * l_ref[...] + jnp.sum(pn, axis=0, keepdims=True)
        acc_ref[...] = alpha * acc_ref[...] + weighted_values(pn, rows)

    for n in range(0, nb - 1, 2):
        if n + 1 < nb - 1:
            pl.when(n + 1 < i)(functools.partial(past_blocks, n, 2))
        pl.when(n == i - 1)(functools.partial(past_blocks, n, 1))

    o_t = acc_ref[...] / l_ref[...]
    y_t = o_t * lax.rsqrt(jnp.mean(o_t * o_t, axis=0, keepdims=True) + NORM_EPS)
    _store_heads_transposed(o_ref, y_t, g_ref, hb, blk)


def _moba(proj, bias, g, heads, q0, k0, v0):
    b, s, _ = proj.shape
    blk = MOBA_BLOCK
    nb = s // blk
    hb = math.gcd(8, heads, q0, k0, v0)
    wide = hb * HEAD_DIM
    bias = jnp.swapaxes(bias, -1, -2) * LOG2E
    seq = lambda c0: pl.BlockSpec((None, s, wide), lambda bi, h, i: (bi, 0, c0 // hb + h))
    return pl.pallas_call(
        functools.partial(_moba_kernel, nb=nb, hb=hb, scale=HEAD_DIM ** -0.5 * LOG2E),
        grid=(b, heads // hb, nb),
        in_specs=[
            pl.BlockSpec((None, blk, wide), lambda bi, h, i: (bi, i, q0 // hb + h)),
            seq(k0), seq(v0),
            pl.BlockSpec((hb, 3, blk, blk), lambda bi, h, i: (h, 0, 0, 0)),
            pl.BlockSpec((1, wide), lambda bi, h, i: (0, h)),
        ],
        out_specs=pl.BlockSpec((None, blk, wide), lambda bi, h, i: (bi, i, h)),
        out_shape=jax.ShapeDtypeStruct((b, s, heads * HEAD_DIM), BF16),
        scratch_shapes=[pltpu.VMEM((nb, wide), F32), pltpu.VMEM((1, hb * blk), F32),
                        pltpu.VMEM((1, hb * blk), F32), pltpu.VMEM((HEAD_DIM, hb * blk), F32)],
        compiler_params=_params(("parallel", "parallel", "arbitrary")),
        name="moba_attention",
    )(proj, proj, proj, bias, g.reshape(1, -1))


def _sb_kernel(q_ref, k_ref, v_ref, g_ref, o_ref, acc_ref, *, hb, scale):
    i = pl.program_id(2)
    blk = SB_BLOCK
    key = lax.broadcasted_iota(jnp.int32, (blk, hb * blk), 0)
    qry = lax.broadcasted_iota(jnp.int32, (blk, hb * blk), 1) & (blk - 1)
    rs = lax.broadcasted_iota(jnp.int32, (blk, blk), 0)
    cs = lax.broadcasted_iota(jnp.int32, (blk, blk), 1)
    suffix = jnp.where(cs >= rs, 1.0, 0.0).astype(BF16)

    def block(rows, carry, diagonal):
        z = _lane_heads(
            lambda h: _nt_dot(k_ref[rows, _head_cols(h)], q_ref[:, _head_cols(h)]), hb) * scale
        zb = z.astype(BF16)
        sp = jnp.maximum(zb, 0.0) + jnp.log(1.0 + jnp.exp2(jnp.abs(zb) * -LOG2E))
        if diagonal:
            strict = key < qry
            sp = jnp.where(strict, sp, 0.0)
        tot = _dot(suffix, sp.astype(BF16)) + carry
        a = jnp.exp(z - tot)
        if diagonal:
            a = jnp.where(strict, a, 0.0)
        a = a.astype(BF16)
        av = _lane_heads(
            lambda h: _tn_dot(v_ref[rows, _head_cols(h)], a[:, h * blk:(h + 1) * blk]), hb)
        if diagonal:
            acc_ref[...] = av
        else:
            acc_ref[...] += av
        return tot[0:1, :]

    own = pl.ds(pl.multiple_of(i * blk, blk), blk)
    carry = block(own, jnp.zeros((1, hb * blk), F32), True)

    def past(j):
        return pl.ds(pl.multiple_of(j * blk, blk), blk)

    group = 4
    rem = i % group
    carry = lax.fori_loop(0, rem, lambda t, c: block(past(i - 1 - t), c, False), carry)

    def several(step, c):
        j = i - rem - 1 - group * step
        for t in range(group):
            c = block(past(j - t), c, False)
        return c

    lax.fori_loop(0, i // group, several, carry)
    acc = acc_ref[...]
    y_t = acc * lax.rsqrt(jnp.mean(acc * acc, axis=0, keepdims=True) + NORM_EPS)
    _store_heads_transposed(o_ref, y_t, g_ref, hb, blk)


def _stick_breaking(proj, g, heads, q0, k0, v0):
    b, s, _ = proj.shape
    blk = SB_BLOCK
    hb = math.gcd(8, heads, q0, k0, v0)
    wide = hb * HEAD_DIM
    seq = lambda c0: pl.BlockSpec((None, s, wide), lambda bi, h, i: (bi, 0, c0 // hb + h))
    return pl.pallas_call(
        functools.partial(_sb_kernel, hb=hb, scale=HEAD_DIM ** -0.5),
        grid=(b, heads // hb, s // blk),
        in_specs=[
            pl.BlockSpec((None, blk, wide), lambda bi, h, i: (bi, i, q0 // hb + h)),
            seq(k0), seq(v0),
            pl.BlockSpec((1, wide), lambda bi, h, i: (0, h)),
        ],
        out_specs=pl.BlockSpec((None, blk, wide), lambda bi, h, i: (bi, i, h)),
        out_shape=jax.ShapeDtypeStruct((b, s, heads * HEAD_DIM), BF16),
        scratch_shapes=[pltpu.VMEM((HEAD_DIM, hb * blk), F32)],
        compiler_params=_params(("parallel", "parallel", "parallel")),
        name="stick_breaking_attention",
    )(proj, proj, proj, g.reshape(1, -1))


def _swa_kernel(q_ref, kp_ref, kc_ref, vp_ref, vc_ref, bias_ref, sink_ref, g_ref, o_ref,
                *, group, scale):
    n = pl.program_id(1)
    w = SWA_WINDOW
    kv_heads = kc_ref.shape[1] // HEAD_DIM
    key = lax.broadcasted_iota(jnp.int32, (2 * w, group * w), 0)
    qry = lax.broadcasted_iota(jnp.int32, (2 * w, group * w), 1) & (w - 1)
    dist = qry + w - key
    mask = (dist >= 0) & (dist < w) & (key + n * w >= w)
    for kh in range(kv_heads):
        kcols = _head_cols(kh)
        kk = jnp.concatenate([kp_ref[:, kcols], kc_ref[:, kcols]], axis=0)
        vv = jnp.concatenate([vp_ref[:, kcols], vc_ref[:, kcols]], axis=0)
        q = _stack_heads(lambda h: q_ref[:, _head_cols(kh * group + h)], group)
        s = jnp.where(mask, _nt_dot(kk, q) * scale + bias_ref[kh], NEG_INF)
        sink = sink_ref[kh]
        m = jnp.maximum(jnp.max(s, axis=0, keepdims=True), sink)
        e = jnp.exp(s - m)
        denom = jnp.sum(e, axis=0, keepdims=True) + jnp.exp(sink - m)
        o_t = _tn_dot(vv, e.astype(BF16)) / denom
        y_t = o_t * lax.rsqrt(jnp.mean(o_t * o_t, axis=0, keepdims=True) + NORM_EPS)
        for h in range(group):
            cols = _head_cols(kh * group + h)
            o_ref[:, cols] = (y_t[:, h * w:(h + 1) * w].T * g_ref[:, cols]).astype(BF16)


def _swa(proj, bias, sinks, g, q_heads, kv_heads, q0, k0, v0):
    b, s, _ = proj.shape
    w = SWA_WINDOW
    group = q_heads // kv_heads
    assert q0 % q_heads == 0 and k0 % kv_heads == 0 and v0 % kv_heads == 0
    bias = bias.reshape(kv_heads, group, w, 2 * w).transpose(0, 3, 1, 2).reshape(
        kv_heads, 2 * w, group * w)
    sink_rows = jnp.broadcast_to(sinks.astype(F32).reshape(kv_heads, 1, group, 1),
                                 (kv_heads, 1, group, w)).reshape(kv_heads, 1, group * w)
    kv = lambda c0, prev: pl.BlockSpec(
        (None, w, kv_heads * HEAD_DIM),
        lambda bi, n: (bi, jnp.maximum(n - 1, 0) if prev else n, c0 // kv_heads))
    whole = lambda shape: pl.BlockSpec(shape, lambda bi, n: (0,) * len(shape))
    return pl.pallas_call(
        functools.partial(_swa_kernel, group=group, scale=HEAD_DIM ** -0.5),
        grid=(b, s // w),
        in_specs=[
            pl.BlockSpec((None, w, q_heads * HEAD_DIM), lambda bi, n: (bi, n, q0 // q_heads)),
            kv(k0, True), kv(k0, False), kv(v0, True), kv(v0, False),
            whole((kv_heads, 2 * w, group * w)), whole((kv_heads, 1, group * w)),
            whole((1, q_heads * HEAD_DIM)),
        ],
        out_specs=pl.BlockSpec((None, w, q_heads * HEAD_DIM), lambda bi, n: (bi, n, 0)),
        out_shape=jax.ShapeDtypeStruct((b, s, q_heads * HEAD_DIM), BF16),
        compiler_params=_params(("parallel", "parallel")),
        name="swa_sink_attention",
    )(proj, proj, proj, proj, proj, bias, sink_rows, g.reshape(1, -1))


def _toeplitz(v, rows, cols):
    heads, length = v.shape
    t = jnp.tile(v, (1, rows))[:, :rows * (length - 1)].reshape(heads, rows, length - 1)
    return t[:, :, :cols]


def _bias_tiles(rel_bias, moba_heads, swa_heads):
    blk, w = MOBA_BLOCK, SWA_WINDOW
    length = 2 * blk
    k = jnp.arange(length)
    bd = rel_bias[_rel_bucket(k)].astype(F32).T
    mb, sw = bd[:moba_heads], bd[moba_heads:moba_heads + swa_heads]
    own = _toeplitz(mb[:, (-k) % length], blk, blk)
    adj = _toeplitz(mb[:, (blk - k) % length], blk, blk)
    far = jnp.broadcast_to(mb[:, length - 1][:, None, None], own.shape)
    swa = _toeplitz(sw[:, (w - k) % length], w, 2 * w)
    return jnp.stack([own, adj, far], axis=1), swa


def _head_counts(w_in, g_moba, g_sb, g_swa):
    moba_w, sb_w, swa_w = g_moba.shape[1], g_sb.shape[1], g_swa.shape[1]
    kv_w = (w_in.shape[2] - 3 * moba_w - 3 * sb_w - swa_w) // 2
    return moba_w // HEAD_DIM, sb_w // HEAD_DIM, swa_w // HEAD_DIM, kv_w // HEAD_DIM


def _mixing(h, bias_tiles, w_in, w_out, g_moba, g_sb, g_swa, sinks, layer):
    b, s, d = h.shape
    in_w = w_in.shape[2]
    mh, sh, qh, kvh = _head_counts(w_in, g_moba, g_sb, g_swa)
    assert qh % kvh == 0 and (3 * mh + 3 * sh) % (qh // kvh) == 0
    proj = _matmul(h.reshape(b * s, d), w_in, layer, BF16).reshape(b, s, in_w)
    o_a = _moba(proj, bias_tiles[0], g_moba[layer], mh, 0, mh, 2 * mh)
    o_b = _stick_breaking(proj, g_sb[layer], sh, 3 * mh, 3 * mh + sh, 3 * mh + 2 * sh)
    c0 = 3 * mh + 3 * sh
    o_c = _swa(proj, bias_tiles[1], sinks[layer], g_swa[layer], qh, kvh, c0, c0 + qh, c0 + qh + kvh)
    m = b * s
    return _out_proj(o_a.reshape(m, -1), o_b.reshape(m, -1), o_c.reshape(m, -1), w_out, layer)


def _row_gather_pipeline(n_live, fetch_rows, wait_rows, compute, idle=None):
    i = pl.program_id(0)

    @pl.when((i == 0) & (n_live > 0))
    def _():
        fetch_rows(0, 0)

    @pl.when(i + 1 < n_live)
    def _():
        fetch_rows(i + 1, (i + 1) % 2)

    @pl.when(i < n_live)
    def _():
        wait_rows(i % 2)
        compute(i % 2)

    if idle is not None:
        pl.when(i >= n_live)(idle)


def _gather_kernel(tok_ref, nlive_ref, src_ref, o_ref, buf_ref, sem_ref, *, rows):
    chunks = src_ref.shape[1]

    def copy(step, slot, group, sub):
        tok = tok_ref[step * rows + group * 8 + sub]
        return pltpu.make_async_copy(src_ref.at[tok], buf_ref.at[slot, group, :, sub, :],
                                     sem_ref.at[slot])

    def fetch_rows(step, slot):
        def start(group, c):
            for sub in range(8):
                copy(step, slot, group, sub).start()
            return c
        lax.fori_loop(0, rows // 8, start, 0)

    def wait_rows(slot):
        def wait(group, c):
            for sub in range(8):
                copy(0, slot, group, sub).wait()
            return c
        lax.fori_loop(0, rows // 8, wait, 0)

    def compute(slot):
        for j in range(chunks):
            o_ref[:, j * LANES:(j + 1) * LANES] = (
                buf_ref[slot, :, j].reshape(rows, LANES).astype(o_ref.dtype))

    def idle():
        o_ref[...] = jnp.zeros_like(o_ref)

    _row_gather_pipeline(nlive_ref[0], fetch_rows, wait_rows, compute, idle)


def _gather_rows(src, token_of, live_rows, rows=256):
    p = token_of.shape[0]
    t, d = src.shape
    chunks = d // LANES
    n_live = (live_rows // rows).astype(jnp.int32).reshape(1)
    return pl.pallas_call(
        functools.partial(_gather_kernel, rows=rows),
        grid_spec=pltpu.PrefetchScalarGridSpec(
            num_scalar_prefetch=2,
            grid=(p // rows,),
            in_specs=[pl.BlockSpec(memory_space=pl.ANY)],
            out_specs=pl.BlockSpec((rows, d), lambda i, tok, nl: (i, 0)),
            scratch_shapes=[pltpu.VMEM((2, rows // 8, chunks, 8, LANES), F32),
                            pltpu.SemaphoreType.DMA((2,))],
        ),
        out_shape=jax.ShapeDtypeStruct((p, d), BF16),
        compiler_params=_params(("arbitrary",)),
        name="moe_gather",
    )(token_of, n_live, src.reshape(t, chunks, LANES))


def _gmm_kernel(t_ref, ot_ref, oj_ref, live_ref, first_ref, slot_ref, e_ref, j_ref, nok_ref,
                ne_ref, nj_ref, a_ref, *rest, n_w, layer, tn, quant):
    w_hbm, o_ref, wbuf_ref, sem_ref = rest[:n_w], rest[n_w], rest[n_w + 1], rest[n_w + 2]
    s = pl.program_id(0)
    slot = slot_ref[s]
    tm = o_ref.shape[0]

    def weight_copies(e, j, to_slot):
        cols = pl.ds(pl.multiple_of(j * tn, tn), tn)
        return [pltpu.make_async_copy(w.at[layer, e, :, cols], wbuf_ref.at[to_slot, i],
                                      sem_ref.at[to_slot]) for i, w in enumerate(w_hbm)]

    @pl.when(s == 0)
    def _():
        for c in weight_copies(e_ref[0], j_ref[0], 0):
            c.start()

    @pl.when((first_ref[s] > 0) & (nok_ref[s] > 0))
    def _():
        for c in weight_copies(ne_ref[s], nj_ref[s], 1 - slot):
            c.start()

    @pl.when(first_ref[s] > 0)
    def _():
        for c in weight_copies(e_ref[s], j_ref[s], slot):
            c.wait()

    @pl.when(live_ref[s] == 0)
    def _():
        o_ref[...] = jnp.zeros_like(o_ref)

    for rows in range(quant, tm + 1, quant):
        @pl.when(live_ref[s] == rows)
        def _(rows=rows):
            a = a_ref[:rows]
            if n_w == 2:
                g = _dot(a, wbuf_ref[slot, 0].astype(BF16))
                u = _dot(a, wbuf_ref[slot, 1].astype(BF16))
                o_ref[:rows] = (g * jax.nn.sigmoid(g) * u).astype(o_ref.dtype)
            else:
                o_ref[:rows] = _dot(a, wbuf_ref[slot, 0].astype(BF16)).astype(o_ref.dtype)
            if rows < tm:
                o_ref[rows:] = jnp.zeros((tm - rows, o_ref.shape[1]), o_ref.dtype)


def _grouped_matmul(a, weights, layer, sched, out_dtype, tn):
    p, k = a.shape
    n = weights[0].shape[3]
    tm = MOE_ROW_TILE
    n_w = len(weights)
    steps = sched[0].shape[0]
    n_sched = len(sched)
    return pl.pallas_call(
        functools.partial(_gmm_kernel, n_w=n_w, layer=layer, tn=tn, quant=MOE_ROW_QUANT),
        grid_spec=pltpu.PrefetchScalarGridSpec(
            num_scalar_prefetch=n_sched,
            grid=(steps,),
            in_specs=[pl.BlockSpec((tm, k), lambda s, t, *_: (t[s], 0))]
            + [pl.BlockSpec(memory_space=pl.ANY)] * n_w,
            out_specs=pl.BlockSpec((tm, tn), lambda s, t, ot, oj, *_: (ot[s], oj[s])),
            scratch_shapes=[pltpu.VMEM((2, n_w, k, tn), F32), pltpu.SemaphoreType.DMA((2,))],
        ),
        out_shape=jax.ShapeDtypeStruct((p, n), out_dtype),
        compiler_params=_params(("arbitrary",)),
        name="moe_grouped_matmul",
    )(*sched, a, *weights)


def _combine_kernel(pos_ref, src_ref, w_ref, x_ref, gate_ref, gpost_ref, gnext_ref, sc_ref, sh_ref,
                    xo_ref, *rest, rows, tokens, emit_h):
    buf_ref, sem_ref = rest[-2:]

    def copy(step, slot, r, choice):
        return pltpu.make_async_copy(
            src_ref.at[pl.ds(pos_ref[choice * tokens + step * rows + r], 1)],
            buf_ref.at[slot, choice, pl.ds(r, 1)], sem_ref.at[slot])

    def fetch_rows(step, slot):
        def start(r, c):
            for choice in range(TOP_K):
                copy(step, slot, r, choice).start(priority=choice % 2)
            return c
        lax.fori_loop(0, rows, start, 0, unroll=4)

    def wait_rows(slot):
        def wait(r, c):
            for choice in range(TOP_K):
                copy(0, slot, r, choice).wait()
            return c
        lax.fori_loop(0, rows, wait, 0, unroll=4)

    def compute(slot):
        wts = w_ref[...]
        y = wts[:, 0:1] * buf_ref[slot, 0] + wts[:, 1:2] * buf_ref[slot, 1]
        x_new = x_ref[...] + gate_ref[...] * (_rms(y) * gpost_ref[...])
        xo_ref[...] = x_new
        if emit_h:
            h = (_rms(x_new) * gnext_ref[...]) * (1.0 + sc_ref[...]) + sh_ref[...]
            rest[0][...] = h.astype(BF16)

    _row_gather_pipeline(pl.num_programs(0), fetch_rows, wait_rows, compute)


def _combine_post(expert_out, pos, wts, x, gate, g_post, g_next=None, scale=None, shift=None,
                  rows=256):
    b, s, d = x.shape
    t = b * s
    per_b = s // rows
    emit_h = g_next is not None
    if not emit_h:
        g_next, scale, shift = g_post, gate, gate
    row = pl.BlockSpec((rows, d), lambda i, pos: (i, 0))
    vec = pl.BlockSpec((1, d), lambda i, pos: (0, 0))
    per_batch = pl.BlockSpec((None, 1, d), lambda i, pos: (i // per_b, 0, 0))
    out_specs = [row] + ([row] if emit_h else [])
    out_shape = [jax.ShapeDtypeStruct((t, d), F32)] + (
        [jax.ShapeDtypeStruct((t, d), BF16)] if emit_h else [])
    outs = pl.pallas_call(
        functools.partial(_combine_kernel, rows=rows, tokens=t, emit_h=emit_h),
        grid_spec=pltpu.PrefetchScalarGridSpec(
            num_scalar_prefetch=1,
            grid=(t // rows,),
            in_specs=[pl.BlockSpec(memory_space=pl.ANY),
                      pl.BlockSpec((rows, LANES), lambda i, pos: (i, 0)),
                      row, per_batch, vec, vec, per_batch, per_batch],
            out_specs=out_specs,
            scratch_shapes=[pltpu.VMEM((2, TOP_K, rows, d), F32), pltpu.SemaphoreType.DMA((2,))],
        ),
        out_shape=out_shape,
        compiler_params=_params(("arbitrary",)),
        name="moe_combine_post",
    )(pos, expert_out, wts, x.reshape(t, d), gate, g_post.reshape(1, d), g_next.reshape(1, d),
      scale, shift)
    return [o.reshape(b, s, d) for o in outs]


def _route_plan(sel, comb, n_exp):
    t = sel.shape[0]
    tm = MOE_ROW_TILE
    n_tiles = (t * TOP_K) // tm + n_exp
    mask = sel > 0.5
    cnt = jnp.sum(mask, axis=0).astype(jnp.int32)
    tiles_e = (cnt + tm - 1) // tm
    tile_end = jnp.cumsum(tiles_e)
    tile_start = tile_end - tiles_e
    rank = jnp.cumsum(mask, axis=0).astype(jnp.int32) - 1
    pos_te = tile_start[None, :] * tm + rank
    order = jnp.argsort(jnp.logical_not(mask), axis=1, stable=True)[:, :TOP_K].astype(jnp.int32)
    pos = jnp.take_along_axis(pos_te, order, axis=1)
    wts = jnp.take_along_axis(comb, order, axis=1)
    token_of = jnp.zeros((n_tiles * tm,), jnp.int32).at[pos.reshape(-1)].set(
        jnp.repeat(jnp.arange(t, dtype=jnp.int32), TOP_K))
    wts_pad = jnp.zeros((t, LANES), F32).at[:, :TOP_K].set(wts)

    def schedule(col_tiles):
        steps = jnp.arange(n_tiles * col_tiles, dtype=jnp.int32)
        live = tile_end[-1] * col_tiles
        last = jnp.maximum(live - 1, 0)
        steps_c = jnp.minimum(steps, last)
        e_of = jnp.minimum(jnp.sum(steps_c[:, None] >= (tile_end * col_tiles)[None, :], axis=1),
                           n_exp - 1).astype(jnp.int32)
        local = steps_c - tile_start[e_of] * col_tiles
        n_e = jnp.maximum(tiles_e[e_of], 1)
        j_of = (local // n_e).astype(jnp.int32)
        t_of = (tile_start[e_of] + local % n_e).astype(jnp.int32)
        n_dead = jnp.maximum(n_tiles - tile_end[-1], 1)
        dead = jnp.maximum(steps - live, 0)
        is_live = steps < live
        ot_of = jnp.where(is_live, t_of, tile_end[-1] + dead % n_dead).astype(jnp.int32)
        oj_of = jnp.where(is_live, j_of, dead // n_dead).astype(jnp.int32)
        in_tile = jnp.clip(cnt[e_of] - (local % n_e) * tm, 0, tm)
        quant = MOE_ROW_QUANT
        rows_live = jnp.where(is_live, (in_tile + quant - 1) // quant * quant, 0).astype(jnp.int32)
        first = is_live & (local % n_e == 0)
        slot = jnp.where(is_live, (jnp.cumsum(first) - 1) % 2, 0).astype(jnp.int32)
        nxt = steps + n_e
        nxt_ok = first & (nxt < live)
        nxt_c = jnp.minimum(nxt, last)
        as_i32 = lambda v: v.astype(jnp.int32)
        return (t_of, ot_of, oj_of, rows_live, as_i32(first), slot, e_of, j_of, as_i32(nxt_ok),
                e_of[nxt_c], j_of[nxt_c])

    return token_of, tile_end[-1] * tm, schedule, pos.T.reshape(-1), wts_pad


def _col_tile(n, pref):
    while n % pref:
        pref //= 2
    return pref


def _moe(h_f32, comb, sel, w_gate, w_up, w_down, layer_idx):
    t, d = h_f32.shape
    n_exp = w_gate.shape[1]
    d_exp = w_gate.shape[3]
    tn_up, tn_down = _col_tile(d_exp, 512), _col_tile(d, 1024)
    token_of, live_rows, schedule, pos, wts = _route_plan(sel[:, :n_exp], comb[:, :n_exp], n_exp)
    hs = _gather_rows(h_f32, token_of, live_rows)
    act = _grouped_matmul(hs, (w_gate, w_up), layer_idx, schedule(d_exp // tn_up), BF16, tn_up)
    out = _grouped_matmul(act, (w_down,), layer_idx, schedule(d // tn_down), F32, tn_down)
    return out, pos, wts


def kernel(x, c, rel_bias, w_ada, b_ada, g_pre_mix, w_in, g_grp_moba, g_grp_sb, g_grp_swa, swa_sinks, w_out, g_post_mix, g_pre_ffn, w_ff_gate, w_ff_up, w_ff_down, w_router, w_moe_gate, w_moe_up, w_moe_down, g_post_ffn):
    b, s, d = x.shape
    depth = w_ada.shape[0]
    m = b * s
    mod = _ada_mod(c, w_ada, b_ada).reshape(depth, b, ADA_CHUNKS, 1, d)
    chunk = lambda layer, idx: mod[layer, :, idx]
    h = _prenorm(x, g_pre_mix[0], chunk(0, 1), chunk(0, 0))
    mh, _, qh, _ = _head_counts(w_in, g_grp_moba, g_grp_sb, g_grp_swa)
    bias_tiles = _bias_tiles(rel_bias, mh, qh)
    for layer in range(depth):
        shift_f, scale_f, gate_f = chunk(layer, 3), chunk(layer, 4), chunk(layer, 5)
        y = _mixing(h, bias_tiles, w_in, w_out, g_grp_moba, g_grp_sb, g_grp_swa, swa_sinks, layer)
        idx = layer // 2
        dense = layer % 2 == 0
        nxt = ()
        if layer + 1 < depth:
            nxt = (g_pre_mix[layer + 1], chunk(layer + 1, 1), chunk(layer + 1, 0))
        if dense:
            x, h = _post(y, x, chunk(layer, 2), g_post_mix[layer], g_pre_ffn[layer], scale_f, shift_f)
            act = _swiglu_up(h.reshape(m, d), w_ff_gate, w_ff_up, idx)
            y = _matmul_ktiled(act, w_ff_down, idx)
            outs = _post(y, x, gate_f, g_post_ffn[layer], *nxt)
        else:
            x, hf, comb, sel = _post(y, x, chunk(layer, 2), g_post_mix[layer], g_pre_ffn[layer],
                                     scale_f, shift_f, w_router=w_router[idx])
            expert_out, pos, wts = _moe(hf.reshape(m, d), comb.reshape(m, LANES),
                                        sel.reshape(m, LANES), w_moe_gate, w_moe_up, w_moe_down, idx)
            outs = _combine_post(expert_out, pos, wts, x, gate_f, g_post_ffn[layer], *nxt)
        x = outs[0]
        if nxt:
            h = outs[1]
    return x
```

```python
import functools
import math

import jax
import jax.numpy as jnp
from jax import lax
from jax.experimental import pallas as pl
from jax.experimental.pallas import tpu as pltpu

HEAD_DIM = 128
MOBA_BLOCK = 256
MOBA_TOPK = 3
SB_BLOCK = 256
SWA_WINDOW = 128
REL_BUCKETS = 32
REL_MAX_EXACT = 16
REL_MAX_DISTANCE = 128
NORM_EPS = 1e-6
ADA_CHUNKS = 6
TOP_K = 2
LANES = 128
SUBLANES = 8

VMEM_LIMIT = 56 * 1024 * 1024
ROW_TILE = 256
MM_ROWS = 1024
MM_COLS = 512
UP_COLS = 256
DOWN_TILE = (2048, 1024, 1024)
MOE_ROW_TILE = 512
MOE_ROW_QUANT = 128
MOE_UP_COLS = 512
MOE_DOWN_COLS = 1024
ATTN_HEADS = 8

F32 = jnp.float32
BF16 = jnp.bfloat16
NEG_INF = float("-inf")
LOG2E = math.log2(math.e)


def _params(sem, vmem=VMEM_LIMIT):
    return pltpu.CompilerParams(dimension_semantics=sem, vmem_limit_bytes=vmem)


def _nt_dot(a, b):
    return lax.dot_general(a, b, (((1,), (1,)), ((), ())), preferred_element_type=F32)


def _dot(a, b):
    return jnp.dot(a, b, preferred_element_type=F32)


def _tn_dot(a, b):
    return lax.dot_general(a, b, (((0,), (0,)), ((), ())), preferred_element_type=F32)


def _split_bf16(x):
    hi = x.astype(BF16)
    lo = (x - hi.astype(F32)).astype(BF16)
    return hi, lo


def _rms(x):
    return x * lax.rsqrt(jnp.mean(x * x, axis=-1, keepdims=True) + NORM_EPS)


def _ada_kernel(c_ref, w_ref, b_ref, o_ref):
    c = c_ref[...]
    sc = c * jax.nn.sigmoid(c)
    hi, lo = _split_bf16(sc)
    w = w_ref[...].astype(BF16)
    o_ref[...] = _dot(hi, w) + _dot(lo, w) + b_ref[...]


def _ada_mod(c, w_ada, b_ada):
    depth, d, n = w_ada.shape
    b = c.shape[0]
    rows = SUBLANES
    c_pad = jnp.zeros((rows, d), F32).at[:b].set(c)
    tn = _col_tile(n, MM_COLS)
    out = pl.pallas_call(
        _ada_kernel,
        grid=(depth, n // tn),
        in_specs=[
            pl.BlockSpec((rows, d), lambda l, j: (0, 0)),
            pl.BlockSpec((None, d, tn), lambda l, j: (l, 0, j)),
            pl.BlockSpec((None, 1, tn), lambda l, j: (l, 0, j)),
        ],
        out_specs=pl.BlockSpec((None, rows, tn), lambda l, j: (l, 0, j)),
        out_shape=jax.ShapeDtypeStruct((depth, rows, n), F32),
        compiler_params=_params(("parallel", "parallel")),
        name="ada_mod",
    )(c_pad, w_ada, b_ada.reshape(depth, 1, n))
    return out[:, :b]


def _prenorm_kernel(x_ref, g_ref, sc_ref, sh_ref, h_ref):
    x = x_ref[...]
    h = (_rms(x) * g_ref[...]) * (1.0 + sc_ref[...]) + sh_ref[...]
    h_ref[...] = h.astype(h_ref.dtype)


def _prenorm(x, g, scale, shift, ts=ROW_TILE):
    b, s, d = x.shape
    return pl.pallas_call(
        _prenorm_kernel,
        grid=(b, s // ts),
        in_specs=[
            pl.BlockSpec((None, ts, d), lambda i, j: (i, j, 0)),
            pl.BlockSpec((1, d), lambda i, j: (0, 0)),
            pl.BlockSpec((None, 1, d), lambda i, j: (i, 0, 0)),
            pl.BlockSpec((None, 1, d), lambda i, j: (i, 0, 0)),
        ],
        out_specs=pl.BlockSpec((None, ts, d), lambda i, j: (i, j, 0)),
        out_shape=jax.ShapeDtypeStruct((b, s, d), BF16),
        compiler_params=_params(("parallel", "parallel")),
        name="prenorm",
    )(x, g.reshape(1, d), scale, shift)


def _post_kernel(y_ref, x_ref, gate_ref, gpost_ref, gnext_ref, sc_ref, sh_ref, *rest,
                 emit_h, route):
    if route:
        wr_ref, rest = rest[0], rest[1:]
    xo_ref, rest = rest[0], rest[1:]
    x_new = x_ref[...] + gate_ref[...] * (_rms(y_ref[...].astype(F32)) * gpost_ref[...])
    xo_ref[...] = x_new
    if not emit_h:
        return
    h = (_rms(x_new) * gnext_ref[...]) * (1.0 + sc_ref[...]) + sh_ref[...]
    if not route:
        rest[0][...] = h.astype(BF16)
        return
    hf_ref, comb_ref, sel_ref = rest
    hf_ref[...] = h
    hi, lo = _split_bf16(h)
    w = wr_ref[...]
    whi, wlo = _split_bf16(w)
    logits = _dot(hi, whi) + _dot(lo, whi) + _dot(hi, wlo)
    n_exp = route
    lane = lax.broadcasted_iota(jnp.int32, logits.shape, 1)
    lg = jnp.where(lane < n_exp, logits, NEG_INF)
    m1 = jnp.max(lg, axis=1, keepdims=True)
    i1 = jnp.min(jnp.where(lg == m1, lane, LANES), axis=1, keepdims=True)
    lg2 = jnp.where(lane == i1, NEG_INF, lg)
    m2 = jnp.max(lg2, axis=1, keepdims=True)
    i2 = jnp.min(jnp.where(lg2 == m2, lane, LANES), axis=1, keepdims=True)
    e2 = jnp.exp(m2 - m1)
    w1 = 1.0 / (1.0 + e2)
    w2 = e2 / (1.0 + e2)
    comb_ref[...] = jnp.where(lane == i1, w1, 0.0) + jnp.where(lane == i2, w2, 0.0)
    sel_ref[...] = jnp.where((lane == i1) | (lane == i2), 1.0, 0.0)


def _post(y, x, gate, g_post, g_next=None, scale=None, shift=None, w_router=None, ts=ROW_TILE):
    b, s, d = x.shape
    emit_h = g_next is not None
    n_exp = 0 if w_router is None else w_router.shape[1]
    if not emit_h:
        g_next, scale, shift = g_post, gate, gate
    row = pl.BlockSpec((None, ts, d), lambda i, j: (i, j, 0))
    vec = pl.BlockSpec((1, d), lambda i, j: (0, 0))
    per_b = pl.BlockSpec((None, 1, d), lambda i, j: (i, 0, 0))
    in_specs = [row, row, per_b, vec, vec, per_b, per_b]
    args = [y.reshape(b, s, d), x, gate, g_post.reshape(1, d), g_next.reshape(1, d), scale, shift]
    out_specs = [row]
    out_shape = [jax.ShapeDtypeStruct((b, s, d), F32)]
    if n_exp:
        wr = jnp.zeros((d, LANES), F32).at[:, :n_exp].set(w_router.astype(F32))
        in_specs.append(pl.BlockSpec((d, LANES), lambda i, j: (0, 0)))
        args.append(wr)
        lane_blk = pl.BlockSpec((None, ts, LANES), lambda i, j: (i, j, 0))
        out_specs += [row, lane_blk, lane_blk]
        out_shape += [jax.ShapeDtypeStruct((b, s, d), F32),
                      jax.ShapeDtypeStruct((b, s, LANES), F32),
                      jax.ShapeDtypeStruct((b, s, LANES), F32)]
    elif emit_h:
        out_specs.append(row)
        out_shape.append(jax.ShapeDtypeStruct((b, s, d), BF16))
    return pl.pallas_call(
        functools.partial(_post_kernel, emit_h=emit_h, route=n_exp),
        grid=(b, s // ts),
        in_specs=in_specs,
        out_specs=out_specs,
        out_shape=out_shape,
        compiler_params=_params(("parallel", "parallel")),
        name="post_norm_residual",
    )(*args)


def _mm_kernel(a_ref, w_ref, o_ref):
    o_ref[...] = _dot(a_ref[...], w_ref[...].astype(BF16)).astype(o_ref.dtype)


def _matmul(a, w, layer, out_dtype, tm=MM_ROWS, tn=MM_COLS):
    m, k = a.shape
    n = w.shape[2]
    tm, tn = _col_tile(m, tm), _col_tile(n, tn)
    return pl.pallas_call(
        _mm_kernel,
        grid=(n // tn, m // tm),
        in_specs=[
            pl.BlockSpec((tm, k), lambda j, i: (i, 0)),
            pl.BlockSpec((None, k, tn), lambda j, i: (layer, 0, j)),
        ],
        out_specs=pl.BlockSpec((tm, tn), lambda j, i: (i, j)),
        out_shape=jax.ShapeDtypeStruct((m, n), out_dtype),
        compiler_params=_params(("parallel", "parallel")),
        name="matmul",
    )(a, w)


def _out_proj_kernel(oa_ref, ob_ref, oc_ref, w_ref, y_ref, *, cuts):
    c1, c2 = cuts
    w = w_ref[...].astype(BF16)
    y = _dot(oa_ref[...], w[:c1]) + _dot(ob_ref[...], w[c1:c2]) + _dot(oc_ref[...], w[c2:])
    y_ref[...] = y.astype(y_ref.dtype)


def _out_proj(oa, ob, oc, w, layer, tm=MM_ROWS, tn=MM_COLS):
    m = oa.shape[0]
    k, n = w.shape[1], w.shape[2]
    c1 = oa.shape[1]
    c2 = c1 + ob.shape[1]
    tm, tn = _col_tile(m, tm), _col_tile(n, tn)
    return pl.pallas_call(
        functools.partial(_out_proj_kernel, cuts=(c1, c2)),
        grid=(n // tn, m // tm),
        in_specs=[
            pl.BlockSpec((tm, oa.shape[1]), lambda j, i: (i, 0)),
            pl.BlockSpec((tm, ob.shape[1]), lambda j, i: (i, 0)),
            pl.BlockSpec((tm, oc.shape[1]), lambda j, i: (i, 0)),
            pl.BlockSpec((None, k, tn), lambda j, i: (layer, 0, j)),
        ],
        out_specs=pl.BlockSpec((tm, tn), lambda j, i: (i, j)),
        out_shape=jax.ShapeDtypeStruct((m, n), BF16),
        compiler_params=_params(("parallel", "parallel")),
        name="out_proj",
    )(oa, ob, oc, w)


def _swiglu_up_kernel(a_ref, wg_ref, wu_ref, o_ref):
    a = a_ref[...]
    g = _dot(a, wg_ref[...].astype(BF16))
    u = _dot(a, wu_ref[...].astype(BF16))
    o_ref[...] = (g * jax.nn.sigmoid(g) * u).astype(o_ref.dtype)


def _swiglu_up(a, w_gate, w_up, layer, tm=MM_ROWS, tn=UP_COLS):
    m, k = a.shape
    n = w_gate.shape[2]
    tm, tn = _col_tile(m, tm), _col_tile(n, tn)
    wspec = pl.BlockSpec((None, k, tn), lambda j, i: (layer, 0, j))
    return pl.pallas_call(
        _swiglu_up_kernel,
        grid=(n // tn, m // tm),
        in_specs=[pl.BlockSpec((tm, k), lambda j, i: (i, 0)), wspec, wspec],
        out_specs=pl.BlockSpec((tm, tn), lambda j, i: (i, j)),
        out_shape=jax.ShapeDtypeStruct((m, n), BF16),
        compiler_params=_params(("parallel", "parallel")),
        name="swiglu_up",
    )(a, w_gate, w_up)


def _mm_acc_kernel(a_ref, w_ref, o_ref, acc_ref):
    kk = pl.program_id(2)

    @pl.when(kk == 0)
    def _():
        acc_ref[...] = jnp.zeros_like(acc_ref)

    acc_ref[...] += _dot(a_ref[...], w_ref[...].astype(BF16))

    @pl.when(kk == pl.num_programs(2) - 1)
    def _():
        o_ref[...] = acc_ref[...].astype(o_ref.dtype)


def _matmul_ktiled(a, w, layer):
    tm, tn, tk = DOWN_TILE
    m, k = a.shape
    n = w.shape[2]
    tm, tn, tk = _col_tile(m, tm), _col_tile(n, tn), _col_tile(k, tk)
    return pl.pallas_call(
        _mm_acc_kernel,
        grid=(n // tn, m // tm, k // tk),
        in_specs=[
            pl.BlockSpec((tm, tk), lambda j, i, kk: (i, kk)),
            pl.BlockSpec((None, tk, tn), lambda j, i, kk: (layer, kk, j)),
        ],
        out_specs=pl.BlockSpec((tm, tn), lambda j, i, kk: (i, j)),
        out_shape=jax.ShapeDtypeStruct((m, n), BF16),
        scratch_shapes=[pltpu.VMEM((tm, tn), F32)],
        compiler_params=_params(("parallel", "parallel", "arbitrary")),
        name="matmul_ktiled",
    )(a, w)


def _rel_bucket(dist):
    n = jnp.maximum(dist, 0)
    nf = jnp.maximum(n, 1).astype(F32)
    large = REL_MAX_EXACT + (jnp.log(nf / REL_MAX_EXACT) / math.log(REL_MAX_DISTANCE / REL_MAX_EXACT)
                             * (REL_BUCKETS - REL_MAX_EXACT)).astype(jnp.int32)
    return jnp.where(n < REL_MAX_EXACT, n, jnp.minimum(large, REL_BUCKETS - 1))


def _head_cols(h):
    return slice(h * HEAD_DIM, (h + 1) * HEAD_DIM)


def _stack_heads(fn, heads):
    return jnp.concatenate([fn(h) for h in range(heads)], axis=0)


def _lane_heads(fn, heads):
    return jnp.concatenate([fn(h) for h in range(heads)], axis=1)


def _store_heads_transposed(o_ref, y_t, g_ref, heads, rows):
    for h in range(heads):
        y = y_t[:, h * rows:(h + 1) * rows].T * g_ref[:, _head_cols(h)]
        o_ref[:, _head_cols(h)] = y.astype(o_ref.dtype)


def _moba_kernel(q_ref, k_ref, v_ref, bias_ref, g_ref, o_ref, kmean_ref, m_ref, l_ref, acc_ref,
                 *, nb, hb, scale):
    i = pl.program_id(2)
    blk = MOBA_BLOCK

    @pl.when(i == 0)
    def _():
        kf = k_ref[...].astype(F32).reshape(nb, blk, hb * HEAD_DIM)
        kmean_ref[...] = jnp.mean(kf, axis=1)

    khi, klo = _split_bf16(kmean_ref[...])

    def gate_of(h):
        q = q_ref[:, _head_cols(h)]
        return _nt_dot(khi[:, _head_cols(h)], q) + _nt_dot(klo[:, _head_cols(h)], q)

    gate = _lane_heads(gate_of, hb)
    blk_id = lax.broadcasted_iota(jnp.int32, gate.shape, 0)
    gate = jnp.where(blk_id < i, gate, NEG_INF)
    sel = jnp.zeros(gate.shape, F32)
    for _ in range(min(MOBA_TOPK, nb)):
        m = jnp.max(gate, axis=0, keepdims=True)
        first = jnp.min(jnp.where((gate == m) & (m > NEG_INF), blk_id, nb), axis=0, keepdims=True)
        pick = blk_id == first
        sel = jnp.where(pick, 1.0, sel)
        gate = jnp.where(pick, NEG_INF, gate)

    def scores(rows, bias_idx):
        return _lane_heads(
            lambda h: _nt_dot(k_ref[rows, _head_cols(h)], q_ref[:, _head_cols(h)]) * scale
            + bias_ref[h, bias_idx], hb)

    def weighted_values(p, rows):
        pb = p.astype(BF16)
        return _lane_heads(
            lambda h: _tn_dot(v_ref[rows, _head_cols(h)], pb[:, h * blk:(h + 1) * blk]), hb)

    key = lax.broadcasted_iota(jnp.int32, (blk, hb * blk), 0)
    qry = lax.broadcasted_iota(jnp.int32, (blk, hb * blk), 1) & (blk - 1)
    own = pl.ds(pl.multiple_of(i * blk, blk), blk)
    s = jnp.where(key <= qry, scores(own, 0), NEG_INF)
    m0 = jnp.max(s, axis=0, keepdims=True)
    p = jnp.exp2(s - m0)
    m_ref[...] = m0
    l_ref[...] = jnp.sum(p, axis=0, keepdims=True)
    acc_ref[...] = weighted_values(p, own)

    def past_blocks(n0, count):
        rows = slice(n0 * blk, (n0 + count) * blk)

        def head_scores(h):
            raw = _nt_dot(k_ref[rows, _head_cols(h)], q_ref[:, _head_cols(h)]) * scale
            bias = [bias_ref[h, jnp.minimum(i - (n0 + t), 2)] for t in range(count)]
            return raw + jnp.concatenate(bias, axis=0)

        keep = jnp.concatenate(
            [jnp.broadcast_to(sel[n0 + t:n0 + t + 1, :], (blk, hb * blk)) for t in range(count)],
            axis=0)
        sn = jnp.where(keep > 0.0, _lane_heads(head_scores, hb), NEG_INF)
        m_old = m_ref[...]
        m_new = jnp.maximum(m_old, jnp.max(sn, axis=0, keepdims=True))
        alpha = jnp.exp2(m_old - m_new)
        pn = jnp.exp2(sn - m_new)
        m_ref[...] = m_new
        l_ref[...] = alpha * l_ref[...] + jnp.sum(pn, axis=0, keepdims=True)
        acc_ref[...] = alpha * acc_ref[...] + weighted_values(pn, rows)

    for n in range(0, nb - 1, 2):
        if n + 1 < nb - 1:
            pl.when(n + 1 < i)(functools.partial(past_blocks, n, 2))
        pl.when(n == i - 1)(functools.partial(past_blocks, n, 1))

    o_t = acc_ref[...] / l_ref[...]
    y_t = o_t * lax.rsqrt(jnp.mean(o_t * o_t, axis=0, keepdims=True) + NORM_EPS)
    _store_heads_transposed(o_ref, y_t, g_ref, hb, blk)


def _moba(proj, bias, g, heads, q0, k0, v0):
    b, s, _ = proj.shape
    blk = MOBA_BLOCK
    nb = s // blk
    hb = math.gcd(ATTN_HEADS, heads, q0, k0, v0)
    wide = hb * HEAD_DIM
    bias = jnp.swapaxes(bias, -1, -2) * LOG2E
    seq = lambda c0: pl.BlockSpec((None, s, wide), lambda bi, h, i: (bi, 0, c0 // hb + h))
    return pl.pallas_call(
        functools.partial(_moba_kernel, nb=nb, hb=hb, scale=HEAD_DIM ** -0.5 * LOG2E),
        grid=(b, heads // hb, nb),
        in_specs=[
            pl.BlockSpec((None, blk, wide), lambda bi, h, i: (bi, i, q0 // hb + h)),
            seq(k0), seq(v0),
            pl.BlockSpec((hb, 3, blk, blk), lambda bi, h, i: (h, 0, 0, 0)),
            pl.BlockSpec((1, wide), lambda bi, h, i: (0, h)),
        ],
        out_specs=pl.BlockSpec((None, blk, wide), lambda bi, h, i: (bi, i, h)),
        out_shape=jax.ShapeDtypeStruct((b, s, heads * HEAD_DIM), BF16),
        scratch_shapes=[pltpu.VMEM((nb, wide), F32), pltpu.VMEM((1, hb * blk), F32),
                        pltpu.VMEM((1, hb * blk), F32), pltpu.VMEM((HEAD_DIM, hb * blk), F32)],
        compiler_params=_params(("parallel", "parallel", "arbitrary")),
        name="moba_attention",
    )(proj, proj, proj, bias, g.reshape(1, -1))


def _sb_kernel(q_ref, k_ref, v_ref, g_ref, o_ref, acc_ref, *, hb, scale):
    i = pl.program_id(2)
    blk = SB_BLOCK
    key = lax.broadcasted_iota(jnp.int32, (blk, hb * blk), 0)
    qry = lax.broadcasted_iota(jnp.int32, (blk, hb * blk), 1) & (blk - 1)
    rs = lax.broadcasted_iota(jnp.int32, (blk, blk), 0)
    cs = lax.broadcasted_iota(jnp.int32, (blk, blk), 1)
    suffix = jnp.where(cs >= rs, 1.0, 0.0).astype(BF16)

    def block(rows, carry, diagonal):
        z = _lane_heads(
            lambda h: _nt_dot(k_ref[rows, _head_cols(h)], q_ref[:, _head_cols(h)]), hb) * scale
        zb = z.astype(BF16)
        sp = jnp.maximum(zb, 0.0) + jnp.log(1.0 + jnp.exp2(jnp.abs(zb) * -LOG2E))
        if diagonal:
            strict = key < qry
            sp = jnp.where(strict, sp, 0.0)
        tot = _dot(suffix, sp.astype(BF16)) + carry
        a = jnp.exp(z - tot)
        if diagonal:
            a = jnp.where(strict, a, 0.0)
        a = a.astype(BF16)
        av = _lane_heads(
            lambda h: _tn_dot(v_ref[rows, _head_cols(h)], a[:, h * blk:(h + 1) * blk]), hb)
        if diagonal:
            acc_ref[...] = av
        else:
            acc_ref[...] += av
        return tot[0:1, :]

    own = pl.ds(pl.multiple_of(i * blk, blk), blk)
    carry = block(own, jnp.zeros((1, hb * blk), F32), True)

    def past(j):
        return pl.ds(pl.multiple_of(j * blk, blk), blk)

    group = 4
    rem = i % group
    carry = lax.fori_loop(0, rem, lambda t, c: block(past(i - 1 - t), c, False), carry)

    def several(step, c):
        j = i - rem - 1 - group * step
        for t in range(group):
            c = block(past(j - t), c, False)
        return c

    lax.fori_loop(0, i // group, several, carry)
    acc = acc_ref[...]
    y_t = acc * lax.rsqrt(jnp.mean(acc * acc, axis=0, keepdims=True) + NORM_EPS)
    _store_heads_transposed(o_ref, y_t, g_ref, hb, blk)


def _stick_breaking(proj, g, heads, q0, k0, v0):
    b, s, _ = proj.shape
    blk = SB_BLOCK
    hb = math.gcd(ATTN_HEADS, heads, q0, k0, v0)
    wide = hb * HEAD_DIM
    seq = lambda c0: pl.BlockSpec((None, s, wide), lambda bi, h, i: (bi, 0, c0 // hb + h))
    return pl.pallas_call(
        functools.partial(_sb_kernel, hb=hb, scale=HEAD_DIM ** -0.5),
        grid=(b, heads // hb, s // blk),
        in_specs=[
            pl.BlockSpec((None, blk, wide), lambda bi, h, i: (bi, i, q0 // hb + h)),
            seq(k0), seq(v0),
            pl.BlockSpec((1, wide), lambda bi, h, i: (0, h)),
        ],
        out_specs=pl.BlockSpec((None, blk, wide), lambda bi, h, i: (bi, i, h)),
        out_shape=jax.ShapeDtypeStruct((b, s, heads * HEAD_DIM), BF16),
        scratch_shapes=[pltpu.VMEM((HEAD_DIM, hb * blk), F32)],
        compiler_params=_params(("parallel", "parallel", "parallel")),
        name="stick_breaking_attention",
    )(proj, proj, proj, g.reshape(1, -1))


def _swa_kernel(q_ref, kp_ref, kc_ref, vp_ref, vc_ref, bias_ref, sink_ref, g_ref, o_ref,
                *, group, scale):
    n = pl.program_id(1)
    w = SWA_WINDOW
    kv_heads = kc_ref.shape[1] // HEAD_DIM
    key = lax.broadcasted_iota(jnp.int32, (2 * w, group * w), 0)
    qry = lax.broadcasted_iota(jnp.int32, (2 * w, group * w), 1) & (w - 1)
    dist = qry + w - key
    mask = (dist >= 0) & (dist < w) & (key + n * w >= w)
    for kh in range(kv_heads):
        kcols = _head_cols(kh)
        kk = jnp.concatenate([kp_ref[:, kcols], kc_ref[:, kcols]], axis=0)
        vv = jnp.concatenate([vp_ref[:, kcols], vc_ref[:, kcols]], axis=0)
        q = _stack_heads(lambda h: q_ref[:, _head_cols(kh * group + h)], group)
        s = jnp.where(mask, _nt_dot(kk, q) * scale + bias_ref[kh], NEG_INF)
        sink = sink_ref[kh]
        m = jnp.maximum(jnp.max(s, axis=0, keepdims=True), sink)
        e = jnp.exp(s - m)
        denom = jnp.sum(e, axis=0, keepdims=True) + jnp.exp(sink - m)
        o_t = _tn_dot(vv, e.astype(BF16)) / denom
        y_t = o_t * lax.rsqrt(jnp.mean(o_t * o_t, axis=0, keepdims=True) + NORM_EPS)
        for h in range(group):
            cols = _head_cols(kh * group + h)
            o_ref[:, cols] = (y_t[:, h * w:(h + 1) * w].T * g_ref[:, cols]).astype(BF16)


def _swa(proj, bias, sinks, g, q_heads, kv_heads, q0, k0, v0):
    b, s, _ = proj.shape
    w = SWA_WINDOW
    group = q_heads // kv_heads
    assert q0 % q_heads == 0 and k0 % kv_heads == 0 and v0 % kv_heads == 0
    bias = bias.reshape(kv_heads, group, w, 2 * w).transpose(0, 3, 1, 2).reshape(
        kv_heads, 2 * w, group * w)
    sink_rows = jnp.broadcast_to(sinks.astype(F32).reshape(kv_heads, 1, group, 1),
                                 (kv_heads, 1, group, w)).reshape(kv_heads, 1, group * w)
    kv = lambda c0, prev: pl.BlockSpec(
        (None, w, kv_heads * HEAD_DIM),
        lambda bi, n: (bi, jnp.maximum(n - 1, 0) if prev else n, c0 // kv_heads))
    whole = lambda shape: pl.BlockSpec(shape, lambda bi, n: (0,) * len(shape))
    return pl.pallas_call(
        functools.partial(_swa_kernel, group=group, scale=HEAD_DIM ** -0.5),
        grid=(b, s // w),
        in_specs=[
            pl.BlockSpec((None, w, q_heads * HEAD_DIM), lambda bi, n: (bi, n, q0 // q_heads)),
            kv(k0, True), kv(k0, False), kv(v0, True), kv(v0, False),
            whole((kv_heads, 2 * w, group * w)), whole((kv_heads, 1, group * w)),
            whole((1, q_heads * HEAD_DIM)),
        ],
        out_specs=pl.BlockSpec((None, w, q_heads * HEAD_DIM), lambda bi, n: (bi, n, 0)),
        out_shape=jax.ShapeDtypeStruct((b, s, q_heads * HEAD_DIM), BF16),
        compiler_params=_params(("parallel", "parallel")),
        name="swa_sink_attention",
    )(proj, proj, proj, proj, proj, bias, sink_rows, g.reshape(1, -1))


def _toeplitz(v, rows, cols):
    heads, length = v.shape
    t = jnp.tile(v, (1, rows))[:, :rows * (length - 1)].reshape(heads, rows, length - 1)
    return t[:, :, :cols]


def _bias_tiles(rel_bias, moba_heads, swa_heads):
    blk, w = MOBA_BLOCK, SWA_WINDOW
    length = 2 * blk
    k = jnp.arange(length)
    bd = rel_bias[_rel_bucket(k)].astype(F32).T
    mb, sw = bd[:moba_heads], bd[moba_heads:moba_heads + swa_heads]
    own = _toeplitz(mb[:, (-k) % length], blk, blk)
    adj = _toeplitz(mb[:, (blk - k) % length], blk, blk)
    far = jnp.broadcast_to(mb[:, length - 1][:, None, None], own.shape)
    swa = _toeplitz(sw[:, (w - k) % length], w, 2 * w)
    return jnp.stack([own, adj, far], axis=1), swa


def _head_counts(w_in, g_moba, g_sb, g_swa):
    moba_w, sb_w, swa_w = g_moba.shape[1], g_sb.shape[1], g_swa.shape[1]
    kv_w = (w_in.shape[2] - 3 * moba_w - 3 * sb_w - swa_w) // 2
    return moba_w // HEAD_DIM, sb_w // HEAD_DIM, swa_w // HEAD_DIM, kv_w // HEAD_DIM


def _mixing(h, bias_tiles, w_in, w_out, g_moba, g_sb, g_swa, sinks, layer):
    b, s, d = h.shape
    in_w = w_in.shape[2]
    mh, sh, qh, kvh = _head_counts(w_in, g_moba, g_sb, g_swa)
    assert qh % kvh == 0 and (3 * mh + 3 * sh) % (qh // kvh) == 0
    proj = _matmul(h.reshape(b * s, d), w_in, layer, BF16).reshape(b, s, in_w)
    o_a = _moba(proj, bias_tiles[0], g_moba[layer], mh, 0, mh, 2 * mh)
    o_b = _stick_breaking(proj, g_sb[layer], sh, 3 * mh, 3 * mh + sh, 3 * mh + 2 * sh)
    c0 = 3 * mh + 3 * sh
    o_c = _swa(proj, bias_tiles[1], sinks[layer], g_swa[layer], qh, kvh, c0, c0 + qh, c0 + qh + kvh)
    m = b * s
    return _out_proj(o_a.reshape(m, -1), o_b.reshape(m, -1), o_c.reshape(m, -1), w_out, layer)


def _row_gather_pipeline(n_live, fetch_rows, wait_rows, compute, idle=None):
    i = pl.program_id(0)

    @pl.when((i == 0) & (n_live > 0))
    def _():
        fetch_rows(0, 0)

    @pl.when(i + 1 < n_live)
    def _():
        fetch_rows(i + 1, (i + 1) % 2)

    @pl.when(i < n_live)
    def _():
        wait_rows(i % 2)
        compute(i % 2)

    if idle is not None:
        pl.when(i >= n_live)(idle)


def _gather_kernel(tok_ref, nlive_ref, src_ref, o_ref, buf_ref, sem_ref, *, rows):
    chunks = src_ref.shape[1]

    def copy(step, slot, group, sub):
        tok = tok_ref[step * rows + group * SUBLANES + sub]
        return pltpu.make_async_copy(src_ref.at[tok], buf_ref.at[slot, group, :, sub, :],
                                     sem_ref.at[slot])

    def fetch_rows(step, slot):
        def start(group, c):
            for sub in range(SUBLANES):
                copy(step, slot, group, sub).start()
            return c
        lax.fori_loop(0, rows // SUBLANES, start, 0)

    def wait_rows(slot):
        def wait(group, c):
            for sub in range(SUBLANES):
                copy(0, slot, group, sub).wait()
            return c
        lax.fori_loop(0, rows // SUBLANES, wait, 0)

    def compute(slot):
        for j in range(chunks):
            o_ref[:, j * LANES:(j + 1) * LANES] = (
                buf_ref[slot, :, j].reshape(rows, LANES).astype(o_ref.dtype))

    def idle():
        o_ref[...] = jnp.zeros_like(o_ref)

    _row_gather_pipeline(nlive_ref[0], fetch_rows, wait_rows, compute, idle)


def _gather_rows(src, token_of, live_rows, rows=ROW_TILE):
    p = token_of.shape[0]
    t, d = src.shape
    chunks = d // LANES
    n_live = (live_rows // rows).astype(jnp.int32).reshape(1)
    return pl.pallas_call(
        functools.partial(_gather_kernel, rows=rows),
        grid_spec=pltpu.PrefetchScalarGridSpec(
            num_scalar_prefetch=2,
            grid=(p // rows,),
            in_specs=[pl.BlockSpec(memory_space=pl.ANY)],
            out_specs=pl.BlockSpec((rows, d), lambda i, tok, nl: (i, 0)),
            scratch_shapes=[pltpu.VMEM((2, rows // SUBLANES, chunks, SUBLANES, LANES), F32),
                            pltpu.SemaphoreType.DMA((2,))],
        ),
        out_shape=jax.ShapeDtypeStruct((p, d), BF16),
        compiler_params=_params(("arbitrary",)),
        name="moe_gather",
    )(token_of, n_live, src.reshape(t, chunks, LANES))


def _gmm_kernel(t_ref, ot_ref, oj_ref, live_ref, first_ref, slot_ref, e_ref, j_ref, nok_ref,
                ne_ref, nj_ref, a_ref, *rest, n_w, layer, tn, quant):
    w_hbm, o_ref, wbuf_ref, sem_ref = rest[:n_w], rest[n_w], rest[n_w + 1], rest[n_w + 2]
    s = pl.program_id(0)
    slot = slot_ref[s]
    tm = o_ref.shape[0]

    def weight_copies(e, j, to_slot):
        cols = pl.ds(pl.multiple_of(j * tn, tn), tn)
        return [pltpu.make_async_copy(w.at[layer, e, :, cols], wbuf_ref.at[to_slot, i],
                                      sem_ref.at[to_slot]) for i, w in enumerate(w_hbm)]

    @pl.when(s == 0)
    def _():
        for c in weight_copies(e_ref[0], j_ref[0], 0):
            c.start()

    @pl.when((first_ref[s] > 0) & (nok_ref[s] > 0))
    def _():
        for c in weight_copies(ne_ref[s], nj_ref[s], 1 - slot):
            c.start()

    @pl.when(first_ref[s] > 0)
    def _():
        for c in weight_copies(e_ref[s], j_ref[s], slot):
            c.wait()

    @pl.when(live_ref[s] == 0)
    def _():
        o_ref[...] = jnp.zeros_like(o_ref)

    for rows in range(quant, tm + 1, quant):
        @pl.when(live_ref[s] == rows)
        def _(rows=rows):
            a = a_ref[:rows]
            if n_w == 2:
                g = _dot(a, wbuf_ref[slot, 0].astype(BF16))
                u = _dot(a, wbuf_ref[slot, 1].astype(BF16))
                o_ref[:rows] = (g * jax.nn.sigmoid(g) * u).astype(o_ref.dtype)
            else:
                o_ref[:rows] = _dot(a, wbuf_ref[slot, 0].astype(BF16)).astype(o_ref.dtype)
            if rows < tm:
                o_ref[rows:] = jnp.zeros((tm - rows, o_ref.shape[1]), o_ref.dtype)


def _grouped_matmul(a, weights, layer, sched, out_dtype, tn):
    p, k = a.shape
    n = weights[0].shape[3]
    tm = MOE_ROW_TILE
    n_w = len(weights)
    steps = sched[0].shape[0]
    n_sched = len(sched)
    return pl.pallas_call(
        functools.partial(_gmm_kernel, n_w=n_w, layer=layer, tn=tn, quant=MOE_ROW_QUANT),
        grid_spec=pltpu.PrefetchScalarGridSpec(
            num_scalar_prefetch=n_sched,
            grid=(steps,),
            in_specs=[pl.BlockSpec((tm, k), lambda s, t, *_: (t[s], 0))]
            + [pl.BlockSpec(memory_space=pl.ANY)] * n_w,
            out_specs=pl.BlockSpec((tm, tn), lambda s, t, ot, oj, *_: (ot[s], oj[s])),
            scratch_shapes=[pltpu.VMEM((2, n_w, k, tn), F32), pltpu.SemaphoreType.DMA((2,))],
        ),
        out_shape=jax.ShapeDtypeStruct((p, n), out_dtype),
        compiler_params=_params(("arbitrary",)),
        name="moe_grouped_matmul",
    )(*sched, a, *weights)


def _combine_kernel(pos_ref, src_ref, w_ref, x_ref, gate_ref, gpost_ref, gnext_ref, sc_ref, sh_ref,
                    xo_ref, *rest, rows, tokens, emit_h):
    buf_ref, sem_ref = rest[-2:]

    def copy(step, slot, r, choice):
        return pltpu.make_async_copy(
            src_ref.at[pl.ds(pos_ref[choice * tokens + step * rows + r], 1)],
            buf_ref.at[slot, choice, pl.ds(r, 1)], sem_ref.at[slot])

    def fetch_rows(step, slot):
        def start(r, c):
            for choice in range(TOP_K):
                copy(step, slot, r, choice).start(priority=choice % 2)
            return c
        lax.fori_loop(0, rows, start, 0, unroll=4)

    def wait_rows(slot):
        def wait(r, c):
            for choice in range(TOP_K):
                copy(0, slot, r, choice).wait()
            return c
        lax.fori_loop(0, rows, wait, 0, unroll=4)

    def compute(slot):
        wts = w_ref[...]
        y = wts[:, 0:1] * buf_ref[slot, 0] + wts[:, 1:2] * buf_ref[slot, 1]
        x_new = x_ref[...] + gate_ref[...] * (_rms(y) * gpost_ref[...])
        xo_ref[...] = x_new
        if emit_h:
            h = (_rms(x_new) * gnext_ref[...]) * (1.0 + sc_ref[...]) + sh_ref[...]
            rest[0][...] = h.astype(BF16)

    _row_gather_pipeline(pl.num_programs(0), fetch_rows, wait_rows, compute)


def _combine_post(expert_out, pos, wts, x, gate, g_post, g_next=None, scale=None, shift=None,
                  rows=ROW_TILE):
    b, s, d = x.shape
    t = b * s
    per_b = s // rows
    emit_h = g_next is not None
    if not emit_h:
        g_next, scale, shift = g_post, gate, gate
    row = pl.BlockSpec((rows, d), lambda i, pos: (i, 0))
    vec = pl.BlockSpec((1, d), lambda i, pos: (0, 0))
    per_batch = pl.BlockSpec((None, 1, d), lambda i, pos: (i // per_b, 0, 0))
    out_specs = [row] + ([row] if emit_h else [])
    out_shape = [jax.ShapeDtypeStruct((t, d), F32)] + (
        [jax.ShapeDtypeStruct((t, d), BF16)] if emit_h else [])
    outs = pl.pallas_call(
        functools.partial(_combine_kernel, rows=rows, tokens=t, emit_h=emit_h),
        grid_spec=pltpu.PrefetchScalarGridSpec(
            num_scalar_prefetch=1,
            grid=(t // rows,),
            in_specs=[pl.BlockSpec(memory_space=pl.ANY),
                      pl.BlockSpec((rows, LANES), lambda i, pos: (i, 0)),
                      row, per_batch, vec, vec, per_batch, per_batch],
            out_specs=out_specs,
            scratch_shapes=[pltpu.VMEM((2, TOP_K, rows, d), F32), pltpu.SemaphoreType.DMA((2,))],
        ),
        out_shape=out_shape,
        compiler_params=_params(("arbitrary",)),
        name="moe_combine_post",
    )(pos, expert_out, wts, x.reshape(t, d), gate, g_post.reshape(1, d), g_next.reshape(1, d),
      scale, shift)
    return [o.reshape(b, s, d) for o in outs]


def _route_plan(sel, comb, n_exp):
    t = sel.shape[0]
    tm = MOE_ROW_TILE
    n_tiles = (t * TOP_K) // tm + n_exp
    mask = sel > 0.5
    cnt = jnp.sum(mask, axis=0).astype(jnp.int32)
    tiles_e = (cnt + tm - 1) // tm
    tile_end = jnp.cumsum(tiles_e)
    tile_start = tile_end - tiles_e
    rank = jnp.cumsum(mask, axis=0).astype(jnp.int32) - 1
    pos_te = tile_start[None, :] * tm + rank
    order = jnp.argsort(jnp.logical_not(mask), axis=1, stable=True)[:, :TOP_K].astype(jnp.int32)
    pos = jnp.take_along_axis(pos_te, order, axis=1)
    wts = jnp.take_along_axis(comb, order, axis=1)
    token_of = jnp.zeros((n_tiles * tm,), jnp.int32).at[pos.reshape(-1)].set(
        jnp.repeat(jnp.arange(t, dtype=jnp.int32), TOP_K))
    wts_pad = jnp.zeros((t, LANES), F32).at[:, :TOP_K].set(wts)

    def schedule(col_tiles):
        steps = jnp.arange(n_tiles * col_tiles, dtype=jnp.int32)
        live = tile_end[-1] * col_tiles
        last = jnp.maximum(live - 1, 0)
        steps_c = jnp.minimum(steps, last)
        e_of = jnp.minimum(jnp.sum(steps_c[:, None] >= (tile_end * col_tiles)[None, :], axis=1),
                           n_exp - 1).astype(jnp.int32)
        local = steps_c - tile_start[e_of] * col_tiles
        n_e = jnp.maximum(tiles_e[e_of], 1)
        j_of = (local // n_e).astype(jnp.int32)
        t_of = (tile_start[e_of] + local % n_e).astype(jnp.int32)
        n_dead = jnp.maximum(n_tiles - tile_end[-1], 1)
        dead = jnp.maximum(steps - live, 0)
        is_live = steps < live
        ot_of = jnp.where(is_live, t_of, tile_end[-1] + dead % n_dead).astype(jnp.int32)
        oj_of = jnp.where(is_live, j_of, dead // n_dead).astype(jnp.int32)
        in_tile = jnp.clip(cnt[e_of] - (local % n_e) * tm, 0, tm)
        quant = MOE_ROW_QUANT
        rows_live = jnp.where(is_live, (in_tile + quant - 1) // quant * quant, 0).astype(jnp.int32)
        first = is_live & (local % n_e == 0)
        slot = jnp.where(is_live, (jnp.cumsum(first) - 1) % 2, 0).astype(jnp.int32)
        nxt = steps + n_e
        nxt_ok = first & (nxt < live)
        nxt_c = jnp.minimum(nxt, last)
        as_i32 = lambda v: v.astype(jnp.int32)
        return (t_of, ot_of, oj_of, rows_live, as_i32(first), slot, e_of, j_of, as_i32(nxt_ok),
                e_of[nxt_c], j_of[nxt_c])

    return token_of, tile_end[-1] * tm, schedule, pos.T.reshape(-1), wts_pad


def _col_tile(n, pref):
    while n % pref:
        pref //= 2
    return pref


def _moe(h_f32, comb, sel, w_gate, w_up, w_down, layer_idx):
    t, d = h_f32.shape
    n_exp = w_gate.shape[1]
    d_exp = w_gate.shape[3]
    tn_up, tn_down = _col_tile(d_exp, MOE_UP_COLS), _col_tile(d, MOE_DOWN_COLS)
    token_of, live_rows, schedule, pos, wts = _route_plan(sel[:, :n_exp], comb[:, :n_exp], n_exp)
    hs = _gather_rows(h_f32, token_of, live_rows)
    act = _grouped_matmul(hs, (w_gate, w_up), layer_idx, schedule(d_exp // tn_up), BF16, tn_up)
    out = _grouped_matmul(act, (w_down,), layer_idx, schedule(d // tn_down), F32, tn_down)
    return out, pos, wts


def kernel(x, c, rel_bias, w_ada, b_ada, g_pre_mix, w_in, g_grp_moba, g_grp_sb, g_grp_swa, swa_sinks, w_out, g_post_mix, g_pre_ffn, w_ff_gate, w_ff_up, w_ff_down, w_router, w_moe_gate, w_moe_up, w_moe_down, g_post_ffn):
    b, s, d = x.shape
    depth = w_ada.shape[0]
    m = b * s
    mod = _ada_mod(c, w_ada, b_ada).reshape(depth, b, ADA_CHUNKS, 1, d)
    chunk = lambda layer, idx: mod[layer, :, idx]
    h = _prenorm(x, g_pre_mix[0], chunk(0, 1), chunk(0, 0))
    mh, _, qh, _ = _head_counts(w_in, g_grp_moba, g_grp_sb, g_grp_swa)
    bias_tiles = _bias_tiles(rel_bias, mh, qh)
    for layer in range(depth):
        shift_f, scale_f, gate_f = chunk(layer, 3), chunk(layer, 4), chunk(layer, 5)
        y = _mixing(h, bias_tiles, w_in, w_out, g_grp_moba, g_grp_sb, g_grp_swa, swa_sinks, layer)
        idx = layer // 2
        dense = layer % 2 == 0
        nxt = ()
        if layer + 1 < depth:
            nxt = (g_pre_mix[layer + 1], chunk(layer + 1, 1), chunk(layer + 1, 0))
        if dense:
            x, h = _post(y, x, chunk(layer, 2), g_post_mix[layer], g_pre_ffn[layer], scale_f, shift_f)
            act = _swiglu_up(h.reshape(m, d), w_ff_gate, w_ff_up, idx)
            y = _matmul_ktiled(act, w_ff_down, idx)
            outs = _post(y, x, gate_f, g_post_ffn[layer], *nxt)
        else:
            x, hf, comb, sel = _post(y, x, chunk(layer, 2), g_post_mix[layer], g_pre_ffn[layer],
                                     scale_f, shift_f, w_router=w_router[idx])
            expert_out, pos, wts = _moe(hf.reshape(m, d), comb.reshape(m, LANES),
                                        sel.reshape(m, LANES), w_moe_gate, w_moe_up, w_moe_down, idx)
            outs = _combine_post(expert_out, pos, wts, x, gate_f, g_post_ffn[layer], *nxt)
        x = outs[0]
        if nxt:
            h = outs[1]
    return x
```

```python
import functools
import math

import jax
import jax.numpy as jnp
from jax import lax
from jax.experimental import pallas as pl
from jax.experimental.pallas import tpu as pltpu

HEAD_DIM = 128
MOBA_BLOCK = 256
MOBA_TOPK = 3
SB_BLOCK = 256
SWA_WINDOW = 128
REL_BUCKETS = 32
REL_MAX_EXACT = 16
REL_MAX_DISTANCE = 128
NORM_EPS = 1e-6
ADA_CHUNKS = 6
TOP_K = 2
LANES = 128
SUBLANES = 8

VMEM_LIMIT = 56 * 1024 * 1024
ROW_TILE = 256
ADA_COLS = 1024
MM_ROWS = 1024
MM_COLS = 512
UP_COLS = 256
DOWN_TILE = (2048, 1024, 1024)
MOE_ROW_TILE = 512
MOE_ROW_QUANT = 128
MOE_UP_COLS = 512
MOE_DOWN_COLS = 1024
ATTN_HEADS = 8
SWA_Q_BLOCKS = 2

F32 = jnp.float32
BF16 = jnp.bfloat16
NEG_INF = float("-inf")
LOG2E = math.log2(math.e)


def _params(sem, vmem=VMEM_LIMIT):
    return pltpu.CompilerParams(dimension_semantics=sem, vmem_limit_bytes=vmem)


def _nt_dot(a, b):
    return lax.dot_general(a, b, (((1,), (1,)), ((), ())), preferred_element_type=F32)


def _dot(a, b):
    return jnp.dot(a, b, preferred_element_type=F32)


def _tn_dot(a, b):
    return lax.dot_general(a, b, (((0,), (0,)), ((), ())), preferred_element_type=F32)


def _split_bf16(x):
    hi = x.astype(BF16)
    lo = (x - hi.astype(F32)).astype(BF16)
    return hi, lo


def _rms(x):
    return x * lax.rsqrt(jnp.mean(x * x, axis=-1, keepdims=True) + NORM_EPS)


def _ada_kernel(c_ref, w_ref, b_ref, o_ref):
    c = c_ref[...]
    sc = c * jax.nn.sigmoid(c)
    hi, lo = _split_bf16(sc)
    w = w_ref[...].astype(BF16)
    o_ref[...] = _dot(hi, w) + _dot(lo, w) + b_ref[...]


def _ada_mod(c, w_ada, b_ada):
    depth, d, n = w_ada.shape
    b = c.shape[0]
    rows = SUBLANES
    c_pad = jnp.zeros((rows, d), F32).at[:b].set(c)
    tn = _col_tile(n, ADA_COLS)
    out = pl.pallas_call(
        _ada_kernel,
        grid=(depth, n // tn),
        in_specs=[
            pl.BlockSpec((rows, d), lambda l, j: (0, 0)),
            pl.BlockSpec((None, d, tn), lambda l, j: (l, 0, j)),
            pl.BlockSpec((None, 1, tn), lambda l, j: (l, 0, j)),
        ],
        out_specs=pl.BlockSpec((None, rows, tn), lambda l, j: (l, 0, j)),
        out_shape=jax.ShapeDtypeStruct((depth, rows, n), F32),
        compiler_params=_params(("parallel", "parallel")),
        name="ada_mod",
    )(c_pad, w_ada, b_ada.reshape(depth, 1, n))
    return out[:, :b]


def _prenorm_kernel(x_ref, g_ref, sc_ref, sh_ref, h_ref):
    x = x_ref[...]
    h = (_rms(x) * g_ref[...]) * (1.0 + sc_ref[...]) + sh_ref[...]
    h_ref[...] = h.astype(h_ref.dtype)


def _prenorm(x, g, scale, shift, ts=ROW_TILE):
    b, s, d = x.shape
    return pl.pallas_call(
        _prenorm_kernel,
        grid=(b, s // ts),
        in_specs=[
            pl.BlockSpec((None, ts, d), lambda i, j: (i, j, 0)),
            pl.BlockSpec((1, d), lambda i, j: (0, 0)),
            pl.BlockSpec((None, 1, d), lambda i, j: (i, 0, 0)),
            pl.BlockSpec((None, 1, d), lambda i, j: (i, 0, 0)),
        ],
        out_specs=pl.BlockSpec((None, ts, d), lambda i, j: (i, j, 0)),
        out_shape=jax.ShapeDtypeStruct((b, s, d), BF16),
        compiler_params=_params(("parallel", "parallel")),
        name="prenorm",
    )(x, g.reshape(1, d), scale, shift)


def _post_kernel(y_ref, x_ref, gate_ref, gpost_ref, gnext_ref, sc_ref, sh_ref, *rest,
                 emit_h, route):
    if route:
        wr_ref, rest = rest[0], rest[1:]
    xo_ref, rest = rest[0], rest[1:]
    x_new = x_ref[...] + gate_ref[...] * (_rms(y_ref[...].astype(F32)) * gpost_ref[...])
    xo_ref[...] = x_new
    if not emit_h:
        return
    h = (_rms(x_new) * gnext_ref[...]) * (1.0 + sc_ref[...]) + sh_ref[...]
    if not route:
        rest[0][...] = h.astype(BF16)
        return
    hf_ref, comb_ref, sel_ref = rest
    hf_ref[...] = h
    hi, lo = _split_bf16(h)
    w = wr_ref[...]
    whi, wlo = _split_bf16(w)
    logits = _dot(hi, whi) + _dot(lo, whi) + _dot(hi, wlo)
    n_exp = route
    lane = lax.broadcasted_iota(jnp.int32, logits.shape, 1)
    lg = jnp.where(lane < n_exp, logits, NEG_INF)
    m1 = jnp.max(lg, axis=1, keepdims=True)
    i1 = jnp.min(jnp.where(lg == m1, lane, LANES), axis=1, keepdims=True)
    lg2 = jnp.where(lane == i1, NEG_INF, lg)
    m2 = jnp.max(lg2, axis=1, keepdims=True)
    i2 = jnp.min(jnp.where(lg2 == m2, lane, LANES), axis=1, keepdims=True)
    e2 = jnp.exp(m2 - m1)
    w1 = 1.0 / (1.0 + e2)
    w2 = e2 / (1.0 + e2)
    comb_ref[...] = jnp.where(lane == i1, w1, 0.0) + jnp.where(lane == i2, w2, 0.0)
    sel_ref[...] = jnp.where((lane == i1) | (lane == i2), 1.0, 0.0)


def _post(y, x, gate, g_post, g_next=None, scale=None, shift=None, w_router=None, ts=ROW_TILE):
    b, s, d = x.shape
    emit_h = g_next is not None
    n_exp = 0 if w_router is None else w_router.shape[1]
    if not emit_h:
        g_next, scale, shift = g_post, gate, gate
    row = pl.BlockSpec((None, ts, d), lambda i, j: (i, j, 0))
    vec = pl.BlockSpec((1, d), lambda i, j: (0, 0))
    per_b = pl.BlockSpec((None, 1, d), lambda i, j: (i, 0, 0))
    in_specs = [row, row, per_b, vec, vec, per_b, per_b]
    args = [y.reshape(b, s, d), x, gate, g_post.reshape(1, d), g_next.reshape(1, d), scale, shift]
    out_specs = [row]
    out_shape = [jax.ShapeDtypeStruct((b, s, d), F32)]
    if n_exp:
        wr = jnp.zeros((d, LANES), F32).at[:, :n_exp].set(w_router.astype(F32))
        in_specs.append(pl.BlockSpec((d, LANES), lambda i, j: (0, 0)))
        args.append(wr)
        lane_blk = pl.BlockSpec((None, ts, LANES), lambda i, j: (i, j, 0))
        out_specs += [row, lane_blk, lane_blk]
        out_shape += [jax.ShapeDtypeStruct((b, s, d), F32),
                      jax.ShapeDtypeStruct((b, s, LANES), F32),
                      jax.ShapeDtypeStruct((b, s, LANES), F32)]
    elif emit_h:
        out_specs.append(row)
        out_shape.append(jax.ShapeDtypeStruct((b, s, d), BF16))
    return pl.pallas_call(
        functools.partial(_post_kernel, emit_h=emit_h, route=n_exp),
        grid=(b, s // ts),
        in_specs=in_specs,
        out_specs=out_specs,
        out_shape=out_shape,
        compiler_params=_params(("parallel", "parallel")),
        name="post_norm_residual",
    )(*args)


def _mm_kernel(a_ref, w_ref, o_ref):
    o_ref[...] = _dot(a_ref[...], w_ref[...].astype(BF16)).astype(o_ref.dtype)


def _matmul(a, w, layer, out_dtype, tm=MM_ROWS, tn=MM_COLS):
    m, k = a.shape
    n = w.shape[2]
    tm, tn = _col_tile(m, tm), _col_tile(n, tn)
    return pl.pallas_call(
        _mm_kernel,
        grid=(n // tn, m // tm),
        in_specs=[
            pl.BlockSpec((tm, k), lambda j, i: (i, 0)),
            pl.BlockSpec((None, k, tn), lambda j, i: (layer, 0, j)),
        ],
        out_specs=pl.BlockSpec((tm, tn), lambda j, i: (i, j)),
        out_shape=jax.ShapeDtypeStruct((m, n), out_dtype),
        compiler_params=_params(("parallel", "parallel")),
        name="matmul",
    )(a, w)


def _out_proj_kernel(oa_ref, ob_ref, oc_ref, w_ref, y_ref, *, cuts):
    c1, c2 = cuts
    w = w_ref[...].astype(BF16)
    y = _dot(oa_ref[...], w[:c1]) + _dot(ob_ref[...], w[c1:c2]) + _dot(oc_ref[...], w[c2:])
    y_ref[...] = y.astype(y_ref.dtype)


def _out_proj(oa, ob, oc, w, layer, tm=MM_ROWS, tn=MM_COLS):
    m = oa.shape[0]
    k, n = w.shape[1], w.shape[2]
    c1 = oa.shape[1]
    c2 = c1 + ob.shape[1]
    tm, tn = _col_tile(m, tm), _col_tile(n, tn)
    return pl.pallas_call(
        functools.partial(_out_proj_kernel, cuts=(c1, c2)),
        grid=(n // tn, m // tm),
        in_specs=[
            pl.BlockSpec((tm, oa.shape[1]), lambda j, i: (i, 0)),
            pl.BlockSpec((tm, ob.shape[1]), lambda j, i: (i, 0)),
            pl.BlockSpec((tm, oc.shape[1]), lambda j, i: (i, 0)),
            pl.BlockSpec((None, k, tn), lambda j, i: (layer, 0, j)),
        ],
        out_specs=pl.BlockSpec((tm, tn), lambda j, i: (i, j)),
        out_shape=jax.ShapeDtypeStruct((m, n), BF16),
        compiler_params=_params(("parallel", "parallel")),
        name="out_proj",
    )(oa, ob, oc, w)


def _swiglu_up_kernel(a_ref, wg_ref, wu_ref, o_ref):
    a = a_ref[...]
    g = _dot(a, wg_ref[...].astype(BF16))
    u = _dot(a, wu_ref[...].astype(BF16))
    o_ref[...] = (g * jax.nn.sigmoid(g) * u).astype(o_ref.dtype)


def _swiglu_up(a, w_gate, w_up, layer, tm=MM_ROWS, tn=UP_COLS):
    m, k = a.shape
    n = w_gate.shape[2]
    tm, tn = _col_tile(m, tm), _col_tile(n, tn)
    wspec = pl.BlockSpec((None, k, tn), lambda j, i: (layer, 0, j))
    return pl.pallas_call(
        _swiglu_up_kernel,
        grid=(n // tn, m // tm),
        in_specs=[pl.BlockSpec((tm, k), lambda j, i: (i, 0)), wspec, wspec],
        out_specs=pl.BlockSpec((tm, tn), lambda j, i: (i, j)),
        out_shape=jax.ShapeDtypeStruct((m, n), BF16),
        compiler_params=_params(("parallel", "parallel")),
        name="swiglu_up",
    )(a, w_gate, w_up)


def _mm_acc_kernel(a_ref, w_ref, o_ref, acc_ref):
    kk = pl.program_id(2)

    @pl.when(kk == 0)
    def _():
        acc_ref[...] = jnp.zeros_like(acc_ref)

    acc_ref[...] += _dot(a_ref[...], w_ref[...].astype(BF16))

    @pl.when(kk == pl.num_programs(2) - 1)
    def _():
        o_ref[...] = acc_ref[...].astype(o_ref.dtype)


def _matmul_ktiled(a, w, layer):
    tm, tn, tk = DOWN_TILE
    m, k = a.shape
    n = w.shape[2]
    tm, tn, tk = _col_tile(m, tm), _col_tile(n, tn), _col_tile(k, tk)
    return pl.pallas_call(
        _mm_acc_kernel,
        grid=(n // tn, m // tm, k // tk),
        in_specs=[
            pl.BlockSpec((tm, tk), lambda j, i, kk: (i, kk)),
            pl.BlockSpec((None, tk, tn), lambda j, i, kk: (layer, kk, j)),
        ],
        out_specs=pl.BlockSpec((tm, tn), lambda j, i, kk: (i, j)),
        out_shape=jax.ShapeDtypeStruct((m, n), BF16),
        scratch_shapes=[pltpu.VMEM((tm, tn), F32)],
        compiler_params=_params(("parallel", "parallel", "arbitrary")),
        name="matmul_ktiled",
    )(a, w)


def _rel_bucket(dist):
    n = jnp.maximum(dist, 0)
    nf = jnp.maximum(n, 1).astype(F32)
    large = REL_MAX_EXACT + (jnp.log(nf / REL_MAX_EXACT) / math.log(REL_MAX_DISTANCE / REL_MAX_EXACT)
                             * (REL_BUCKETS - REL_MAX_EXACT)).astype(jnp.int32)
    return jnp.where(n < REL_MAX_EXACT, n, jnp.minimum(large, REL_BUCKETS - 1))


def _head_cols(h):
    return slice(h * HEAD_DIM, (h + 1) * HEAD_DIM)


def _stack_heads(fn, heads):
    return jnp.concatenate([fn(h) for h in range(heads)], axis=0)


def _lane_heads(fn, heads):
    return jnp.concatenate([fn(h) for h in range(heads)], axis=1)


def _store_heads_transposed(o_ref, y_t, g_ref, heads, rows):
    for h in range(heads):
        y = y_t[:, h * rows:(h + 1) * rows].T * g_ref[:, _head_cols(h)]
        o_ref[:, _head_cols(h)] = y.astype(o_ref.dtype)


def _moba_kernel(q_ref, k_ref, v_ref, bias_ref, g_ref, o_ref, kmean_ref, m_ref, l_ref, acc_ref,
                 *, nb, hb, scale):
    i = pl.program_id(2)
    blk = MOBA_BLOCK

    @pl.when(i == 0)
    def _():
        kf = k_ref[...].astype(F32).reshape(nb, blk, hb * HEAD_DIM)
        kmean_ref[...] = jnp.mean(kf, axis=1)

    khi, klo = _split_bf16(kmean_ref[...])

    def gate_of(h):
        q = q_ref[:, _head_cols(h)]
        return _nt_dot(khi[:, _head_cols(h)], q) + _nt_dot(klo[:, _head_cols(h)], q)

    gate = _lane_heads(gate_of, hb)
    blk_id = lax.broadcasted_iota(jnp.int32, gate.shape, 0)
    gate = jnp.where(blk_id < i, gate, NEG_INF)
    sel = jnp.zeros(gate.shape, F32)
    for _ in range(min(MOBA_TOPK, nb)):
        m = jnp.max(gate, axis=0, keepdims=True)
        first = jnp.min(jnp.where((gate == m) & (m > NEG_INF), blk_id, nb), axis=0, keepdims=True)
        pick = blk_id == first
        sel = jnp.where(pick, 1.0, sel)
        gate = jnp.where(pick, NEG_INF, gate)

    def scores(rows, bias_idx):
        return _lane_heads(
            lambda h: _nt_dot(k_ref[rows, _head_cols(h)], q_ref[:, _head_cols(h)]) * scale
            + bias_ref[h, bias_idx], hb)

    def weighted_values(p, rows):
        pb = p.astype(BF16)
        return _lane_heads(
            lambda h: _tn_dot(v_ref[rows, _head_cols(h)], pb[:, h * blk:(h + 1) * blk]), hb)

    key = lax.broadcasted_iota(jnp.int32, (blk, hb * blk), 0)
    qry = lax.broadcasted_iota(jnp.int32, (blk, hb * blk), 1) & (blk - 1)
    own = pl.ds(pl.multiple_of(i * blk, blk), blk)
    s = jnp.where(key <= qry, scores(own, 0), NEG_INF)
    m0 = jnp.max(s, axis=0, keepdims=True)
    p = jnp.exp2(s - m0)
    m_ref[...] = m0
    l_ref[...] = jnp.sum(p, axis=0, keepdims=True)
    acc_ref[...] = weighted_values(p, own)

    def past_blocks(n0, count):
        rows = slice(n0 * blk, (n0 + count) * blk)

        def head_scores(h):
            raw = _nt_dot(k_ref[rows, _head_cols(h)], q_ref[:, _head_cols(h)]) * scale
            bias = [bias_ref[h, jnp.minimum(i - (n0 + t), 2)] for t in range(count)]
            return raw + jnp.concatenate(bias, axis=0)

        keep = jnp.concatenate(
            [jnp.broadcast_to(sel[n0 + t:n0 + t + 1, :], (blk, hb * blk)) for t in range(count)],
            axis=0)
        sn = jnp.where(keep > 0.0, _lane_heads(head_scores, hb), NEG_INF)
        m_old = m_ref[...]
        m_new = jnp.maximum(m_old, jnp.max(sn, axis=0, keepdims=True))
        alpha = jnp.exp2(m_old - m_new)
        pn = jnp.exp2(sn - m_new)
        m_ref[...] = m_new
        l_ref[...] = alpha * l_ref[...] + jnp.sum(pn, axis=0, keepdims=True)
        acc_ref[...] = alpha * acc_ref[...] + weighted_values(pn, rows)

    for n in range(0, nb - 1, 2):
        if n + 1 < nb - 1:
            pl.when(n + 1 < i)(functools.partial(past_blocks, n, 2))
        pl.when(n == i - 1)(functools.partial(past_blocks, n, 1))

    o_t = acc_ref[...] / l_ref[...]
    y_t = o_t * lax.rsqrt(jnp.mean(o_t * o_t, axis=0, keepdims=True) + NORM_EPS)
    _store_heads_transposed(o_ref, y_t, g_ref, hb, blk)


def _moba(proj, bias, g, heads, q0, k0, v0):
    b, s, _ = proj.shape
    blk = MOBA_BLOCK
    nb = s // blk
    hb = math.gcd(ATTN_HEADS, heads, q0, k0, v0)
    wide = hb * HEAD_DIM
    bias = jnp.swapaxes(bias, -1, -2) * LOG2E
    seq = lambda c0: pl.BlockSpec((None, s, wide), lambda bi, h, i: (bi, 0, c0 // hb + h))
    return pl.pallas_call(
        functools.partial(_moba_kernel, nb=nb, hb=hb, scale=HEAD_DIM ** -0.5 * LOG2E),
        grid=(b, heads // hb, nb),
        in_specs=[
            pl.BlockSpec((None, blk, wide), lambda bi, h, i: (bi, i, q0 // hb + h)),
            seq(k0), seq(v0),
            pl.BlockSpec((hb, 3, blk, blk), lambda bi, h, i: (h, 0, 0, 0)),
            pl.BlockSpec((1, wide), lambda bi, h, i: (0, h)),
        ],
        out_specs=pl.BlockSpec((None, blk, wide), lambda bi, h, i: (bi, i, h)),
        out_shape=jax.ShapeDtypeStruct((b, s, heads * HEAD_DIM), BF16),
        scratch_shapes=[pltpu.VMEM((nb, wide), F32), pltpu.VMEM((1, hb * blk), F32),
                        pltpu.VMEM((1, hb * blk), F32), pltpu.VMEM((HEAD_DIM, hb * blk), F32)],
        compiler_params=_params(("parallel", "parallel", "arbitrary")),
        name="moba_attention",
    )(proj, proj, proj, bias, g.reshape(1, -1))


def _sb_kernel(q_ref, k_ref, v_ref, g_ref, o_ref, acc_ref, *, hb, scale):
    i = pl.program_id(2)
    blk = SB_BLOCK
    key = lax.broadcasted_iota(jnp.int32, (blk, hb * blk), 0)
    qry = lax.broadcasted_iota(jnp.int32, (blk, hb * blk), 1) & (blk - 1)
    rs = lax.broadcasted_iota(jnp.int32, (blk, blk), 0)
    cs = lax.broadcasted_iota(jnp.int32, (blk, blk), 1)
    suffix = jnp.where(cs >= rs, 1.0, 0.0).astype(BF16)

    def block(rows, carry, diagonal):
        z = _lane_heads(
            lambda h: _nt_dot(k_ref[rows, _head_cols(h)], q_ref[:, _head_cols(h)]), hb) * scale
        zb = z.astype(BF16)
        sp = jnp.maximum(zb, 0.0) + jnp.log(1.0 + jnp.exp2(jnp.abs(zb) * -LOG2E))
        if diagonal:
            strict = key < qry
            sp = jnp.where(strict, sp, 0.0)
        tot = _dot(suffix, sp.astype(BF16)) + carry
        a = jnp.exp((z - tot).astype(BF16))
        if diagonal:
            a = jnp.where(strict, a, 0.0).astype(BF16)
        av = _lane_heads(
            lambda h: _tn_dot(v_ref[rows, _head_cols(h)], a[:, h * blk:(h + 1) * blk]), hb)
        if diagonal:
            acc_ref[...] = av
        else:
            acc_ref[...] += av
        return tot[0:1, :]

    own = pl.ds(pl.multiple_of(i * blk, blk), blk)
    carry = block(own, jnp.zeros((1, hb * blk), F32), True)

    def past(j):
        return pl.ds(pl.multiple_of(j * blk, blk), blk)

    group = 4
    rem = i % group
    carry = lax.fori_loop(0, rem, lambda t, c: block(past(i - 1 - t), c, False), carry)

    def several(step, c):
        j = i - rem - 1 - group * step
        for t in range(group):
            c = block(past(j - t), c, False)
        return c

    lax.fori_loop(0, i // group, several, carry)
    acc = acc_ref[...]
    y_t = acc * lax.rsqrt(jnp.mean(acc * acc, axis=0, keepdims=True) + NORM_EPS)
    _store_heads_transposed(o_ref, y_t, g_ref, hb, blk)


def _stick_breaking(proj, g, heads, q0, k0, v0):
    b, s, _ = proj.shape
    blk = SB_BLOCK
    hb = math.gcd(ATTN_HEADS, heads, q0, k0, v0)
    wide = hb * HEAD_DIM
    seq = lambda c0: pl.BlockSpec((None, s, wide), lambda bi, h, i: (bi, 0, c0 // hb + h))
    return pl.pallas_call(
        functools.partial(_sb_kernel, hb=hb, scale=HEAD_DIM ** -0.5),
        grid=(b, heads // hb, s // blk),
        in_specs=[
            pl.BlockSpec((None, blk, wide), lambda bi, h, i: (bi, i, q0 // hb + h)),
            seq(k0), seq(v0),
            pl.BlockSpec((1, wide), lambda bi, h, i: (0, h)),
        ],
        out_specs=pl.BlockSpec((None, blk, wide), lambda bi, h, i: (bi, i, h)),
        out_shape=jax.ShapeDtypeStruct((b, s, heads * HEAD_DIM), BF16),
        scratch_shapes=[pltpu.VMEM((HEAD_DIM, hb * blk), F32)],
        compiler_params=_params(("parallel", "parallel", "parallel")),
        name="stick_breaking_attention",
    )(proj, proj, proj, g.reshape(1, -1))


def _swa_kernel(q_ref, kp_ref, kc_ref, vp_ref, vc_ref, bias_ref, sink_ref, g_ref, o_ref,
                *, group, scale):
    w = SWA_WINDOW
    kv_heads = kc_ref.shape[1] // HEAD_DIM
    key = lax.broadcasted_iota(jnp.int32, (2 * w, group * w), 0)
    qry = lax.broadcasted_iota(jnp.int32, (2 * w, group * w), 1) & (w - 1)
    dist = qry + w - key
    in_window = (dist >= 0) & (dist < w)
    for sub in range(SWA_Q_BLOCKS):
        n = pl.program_id(1) * SWA_Q_BLOCKS + sub
        mask = in_window & (key + n * w >= w)
        rows = slice(sub * w, (sub + 1) * w)
        for kh in range(kv_heads):
            kcols = _head_cols(kh)
            if sub == 0:
                kk = jnp.concatenate([kp_ref[:, kcols], kc_ref[rows, kcols]], axis=0)
                vv = jnp.concatenate([vp_ref[:, kcols], vc_ref[rows, kcols]], axis=0)
            else:
                kk = kc_ref[(sub - 1) * w:(sub + 1) * w, kcols]
                vv = vc_ref[(sub - 1) * w:(sub + 1) * w, kcols]
            q = _stack_heads(lambda h: q_ref[rows, _head_cols(kh * group + h)], group)
            s = jnp.where(mask, _nt_dot(kk, q) * scale + bias_ref[kh], NEG_INF)
            sink = sink_ref[kh]
            m = jnp.maximum(jnp.max(s, axis=0, keepdims=True), sink)
            e = jnp.exp(s - m)
            denom = jnp.sum(e, axis=0, keepdims=True) + jnp.exp(sink - m)
            o_t = _tn_dot(vv, e.astype(BF16)) / denom
            y_t = o_t * lax.rsqrt(jnp.mean(o_t * o_t, axis=0, keepdims=True) + NORM_EPS)
            for h in range(group):
                cols = _head_cols(kh * group + h)
                o_ref[rows, cols] = (y_t[:, h * w:(h + 1) * w].T * g_ref[:, cols]).astype(BF16)


def _swa(proj, bias, sinks, g, q_heads, kv_heads, q0, k0, v0):
    b, s, _ = proj.shape
    w = SWA_WINDOW
    group = q_heads // kv_heads
    assert q0 % q_heads == 0 and k0 % kv_heads == 0 and v0 % kv_heads == 0
    bias = bias.reshape(kv_heads, group, w, 2 * w).transpose(0, 3, 1, 2).reshape(
        kv_heads, 2 * w, group * w)
    sink_rows = jnp.broadcast_to(sinks.astype(F32).reshape(kv_heads, 1, group, 1),
                                 (kv_heads, 1, group, w)).reshape(kv_heads, 1, group * w)
    nq = SWA_Q_BLOCKS
    own = lambda c0: pl.BlockSpec((None, nq * w, kv_heads * HEAD_DIM),
                                  lambda bi, n: (bi, n, c0 // kv_heads))
    prev = lambda c0: pl.BlockSpec((None, w, kv_heads * HEAD_DIM),
                                   lambda bi, n: (bi, jnp.maximum(n * nq - 1, 0), c0 // kv_heads))
    whole = lambda shape: pl.BlockSpec(shape, lambda bi, n: (0,) * len(shape))
    return pl.pallas_call(
        functools.partial(_swa_kernel, group=group, scale=HEAD_DIM ** -0.5),
        grid=(b, s // (nq * w)),
        in_specs=[
            pl.BlockSpec((None, nq * w, q_heads * HEAD_DIM), lambda bi, n: (bi, n, q0 // q_heads)),
            prev(k0), own(k0), prev(v0), own(v0),
            whole((kv_heads, 2 * w, group * w)), whole((kv_heads, 1, group * w)),
            whole((1, q_heads * HEAD_DIM)),
        ],
        out_specs=pl.BlockSpec((None, nq * w, q_heads * HEAD_DIM), lambda bi, n: (bi, n, 0)),
        out_shape=jax.ShapeDtypeStruct((b, s, q_heads * HEAD_DIM), BF16),
        compiler_params=_params(("parallel", "parallel")),
        name="swa_sink_attention",
    )(proj, proj, proj, proj, proj, bias, sink_rows, g.reshape(1, -1))


def _toeplitz(v, rows, cols):
    heads, length = v.shape
    t = jnp.tile(v, (1, rows))[:, :rows * (length - 1)].reshape(heads, rows, length - 1)
    return t[:, :, :cols]


def _bias_tiles(rel_bias, moba_heads, swa_heads):
    blk, w = MOBA_BLOCK, SWA_WINDOW
    length = 2 * blk
    k = jnp.arange(length)
    bd = rel_bias[_rel_bucket(k)].astype(F32).T
    mb, sw = bd[:moba_heads], bd[moba_heads:moba_heads + swa_heads]
    own = _toeplitz(mb[:, (-k) % length], blk, blk)
    adj = _toeplitz(mb[:, (blk - k) % length], blk, blk)
    far = jnp.broadcast_to(mb[:, length - 1][:, None, None], own.shape)
    swa = _toeplitz(sw[:, (w - k) % length], w, 2 * w)
    return jnp.stack([own, adj, far], axis=1), swa


def _head_counts(w_in, g_moba, g_sb, g_swa):
    moba_w, sb_w, swa_w = g_moba.shape[1], g_sb.shape[1], g_swa.shape[1]
    kv_w = (w_in.shape[2] - 3 * moba_w - 3 * sb_w - swa_w) // 2
    return moba_w // HEAD_DIM, sb_w // HEAD_DIM, swa_w // HEAD_DIM, kv_w // HEAD_DIM


def _mixing(h, bias_tiles, w_in, w_out, g_moba, g_sb, g_swa, sinks, layer):
    b, s, d = h.shape
    in_w = w_in.shape[2]
    mh, sh, qh, kvh = _head_counts(w_in, g_moba, g_sb, g_swa)
    assert qh % kvh == 0 and (3 * mh + 3 * sh) % (qh // kvh) == 0
    proj = _matmul(h.reshape(b * s, d), w_in, layer, BF16).reshape(b, s, in_w)
    o_a = _moba(proj, bias_tiles[0], g_moba[layer], mh, 0, mh, 2 * mh)
    o_b = _stick_breaking(proj, g_sb[layer], sh, 3 * mh, 3 * mh + sh, 3 * mh + 2 * sh)
    c0 = 3 * mh + 3 * sh
    o_c = _swa(proj, bias_tiles[1], sinks[layer], g_swa[layer], qh, kvh, c0, c0 + qh, c0 + qh + kvh)
    m = b * s
    return _out_proj(o_a.reshape(m, -1), o_b.reshape(m, -1), o_c.reshape(m, -1), w_out, layer)


def _row_gather_pipeline(n_live, fetch_rows, wait_rows, compute, idle=None):
    i = pl.program_id(0)

    @pl.when((i == 0) & (n_live > 0))
    def _():
        fetch_rows(0, 0)

    @pl.when(i + 1 < n_live)
    def _():
        fetch_rows(i + 1, (i + 1) % 2)

    @pl.when(i < n_live)
    def _():
        wait_rows(i % 2)
        compute(i % 2)

    if idle is not None:
        pl.when(i >= n_live)(idle)


def _gather_kernel(tok_ref, nlive_ref, src_ref, o_ref, buf_ref, sem_ref, *, rows):
    chunks = src_ref.shape[1]

    def copy(step, slot, group, sub):
        tok = tok_ref[step * rows + group * SUBLANES + sub]
        return pltpu.make_async_copy(src_ref.at[tok], buf_ref.at[slot, group, :, sub, :],
                                     sem_ref.at[slot])

    def fetch_rows(step, slot):
        def start(group, c):
            for sub in range(SUBLANES):
                copy(step, slot, group, sub).start()
            return c
        lax.fori_loop(0, rows // SUBLANES, start, 0)

    def wait_rows(slot):
        def wait(group, c):
            for sub in range(SUBLANES):
                copy(0, slot, group, sub).wait()
            return c
        lax.fori_loop(0, rows // SUBLANES, wait, 0)

    def compute(slot):
        for j in range(chunks):
            o_ref[:, j * LANES:(j + 1) * LANES] = (
                buf_ref[slot, :, j].reshape(rows, LANES).astype(o_ref.dtype))

    def idle():
        o_ref[...] = jnp.zeros_like(o_ref)

    _row_gather_pipeline(nlive_ref[0], fetch_rows, wait_rows, compute, idle)


def _gather_rows(src, token_of, live_rows, rows=ROW_TILE):
    p = token_of.shape[0]
    t, d = src.shape
    chunks = d // LANES
    n_live = (live_rows // rows).astype(jnp.int32).reshape(1)
    return pl.pallas_call(
        functools.partial(_gather_kernel, rows=rows),
        grid_spec=pltpu.PrefetchScalarGridSpec(
            num_scalar_prefetch=2,
            grid=(p // rows,),
            in_specs=[pl.BlockSpec(memory_space=pl.ANY)],
            out_specs=pl.BlockSpec((rows, d), lambda i, tok, nl: (i, 0)),
            scratch_shapes=[pltpu.VMEM((2, rows // SUBLANES, chunks, SUBLANES, LANES), F32),
                            pltpu.SemaphoreType.DMA((2,))],
        ),
        out_shape=jax.ShapeDtypeStruct((p, d), BF16),
        compiler_params=_params(("arbitrary",)),
        name="moe_gather",
    )(token_of, n_live, src.reshape(t, chunks, LANES))


def _gmm_kernel(t_ref, ot_ref, oj_ref, live_ref, first_ref, slot_ref, e_ref, j_ref, nok_ref,
                ne_ref, nj_ref, a_ref, *rest, n_w, layer, tn, quant):
    w_hbm, o_ref, wbuf_ref, sem_ref = rest[:n_w], rest[n_w], rest[n_w + 1], rest[n_w + 2]
    s = pl.program_id(0)
    slot = slot_ref[s]
    tm = o_ref.shape[0]

    def weight_copies(e, j, to_slot):
        cols = pl.ds(pl.multiple_of(j * tn, tn), tn)
        return [pltpu.make_async_copy(w.at[layer, e, :, cols], wbuf_ref.at[to_slot, i],
                                      sem_ref.at[to_slot]) for i, w in enumerate(w_hbm)]

    @pl.when(s == 0)
    def _():
        for c in weight_copies(e_ref[0], j_ref[0], 0):
            c.start()

    @pl.when((first_ref[s] > 0) & (nok_ref[s] > 0))
    def _():
        for c in weight_copies(ne_ref[s], nj_ref[s], 1 - slot):
            c.start()

    @pl.when(first_ref[s] > 0)
    def _():
        for c in weight_copies(e_ref[s], j_ref[s], slot):
            c.wait()

    @pl.when(live_ref[s] == 0)
    def _():
        o_ref[...] = jnp.zeros_like(o_ref)

    for rows in range(quant, tm + 1, quant):
        @pl.when(live_ref[s] == rows)
        def _(rows=rows):
            a = a_ref[:rows]
            if n_w == 2:
                g = _dot(a, wbuf_ref[slot, 0].astype(BF16))
                u = _dot(a, wbuf_ref[slot, 1].astype(BF16))
                o_ref[:rows] = (g * jax.nn.sigmoid(g) * u).astype(o_ref.dtype)
            else:
                o_ref[:rows] = _dot(a, wbuf_ref[slot, 0].astype(BF16)).astype(o_ref.dtype)
            if rows < tm:
                o_ref[rows:] = jnp.zeros((tm - rows, o_ref.shape[1]), o_ref.dtype)


def _grouped_matmul(a, weights, layer, sched, out_dtype, tn):
    p, k = a.shape
    n = weights[0].shape[3]
    tm = MOE_ROW_TILE
    n_w = len(weights)
    steps = sched[0].shape[0]
    n_sched = len(sched)
    return pl.pallas_call(
        functools.partial(_gmm_kernel, n_w=n_w, layer=layer, tn=tn, quant=MOE_ROW_QUANT),
        grid_spec=pltpu.PrefetchScalarGridSpec(
            num_scalar_prefetch=n_sched,
            grid=(steps,),
            in_specs=[pl.BlockSpec((tm, k), lambda s, t, *_: (t[s], 0))]
            + [pl.BlockSpec(memory_space=pl.ANY)] * n_w,
            out_specs=pl.BlockSpec((tm, tn), lambda s, t, ot, oj, *_: (ot[s], oj[s])),
            scratch_shapes=[pltpu.VMEM((2, n_w, k, tn), F32), pltpu.SemaphoreType.DMA((2,))],
        ),
        out_shape=jax.ShapeDtypeStruct((p, n), out_dtype),
        compiler_params=_params(("arbitrary",)),
        name="moe_grouped_matmul",
    )(*sched, a, *weights)


def _combine_kernel(pos_ref, src_ref, w_ref, x_ref, gate_ref, gpost_ref, gnext_ref, sc_ref, sh_ref,
                    xo_ref, *rest, rows, tokens, emit_h):
    buf_ref, sem_ref = rest[-2:]

    def copy(step, slot, r, choice):
        return pltpu.make_async_copy(
            src_ref.at[pl.ds(pos_ref[choice * tokens + step * rows + r], 1)],
            buf_ref.at[slot, choice, pl.ds(r, 1)], sem_ref.at[slot])

    def fetch_rows(step, slot):
        def start(r, c):
            for choice in range(TOP_K):
                copy(step, slot, r, choice).start(priority=choice % 2)
            return c
        lax.fori_loop(0, rows, start, 0, unroll=4)

    def wait_rows(slot):
        def wait(r, c):
            for choice in range(TOP_K):
                copy(0, slot, r, choice).wait()
            return c
        lax.fori_loop(0, rows, wait, 0, unroll=4)

    def compute(slot):
        wts = w_ref[...]
        y = wts[:, 0:1] * buf_ref[slot, 0] + wts[:, 1:2] * buf_ref[slot, 1]
        x_new = x_ref[...] + gate_ref[...] * (_rms(y) * gpost_ref[...])
        xo_ref[...] = x_new
        if emit_h:
            h = (_rms(x_new) * gnext_ref[...]) * (1.0 + sc_ref[...]) + sh_ref[...]
            rest[0][...] = h.astype(BF16)

    _row_gather_pipeline(pl.num_programs(0), fetch_rows, wait_rows, compute)


def _combine_post(expert_out, pos, wts, x, gate, g_post, g_next=None, scale=None, shift=None,
                  rows=ROW_TILE):
    b, s, d = x.shape
    t = b * s
    per_b = s // rows
    emit_h = g_next is not None
    if not emit_h:
        g_next, scale, shift = g_post, gate, gate
    row = pl.BlockSpec((rows, d), lambda i, pos: (i, 0))
    vec = pl.BlockSpec((1, d), lambda i, pos: (0, 0))
    per_batch = pl.BlockSpec((None, 1, d), lambda i, pos: (i // per_b, 0, 0))
    out_specs = [row] + ([row] if emit_h else [])
    out_shape = [jax.ShapeDtypeStruct((t, d), F32)] + (
        [jax.ShapeDtypeStruct((t, d), BF16)] if emit_h else [])
    outs = pl.pallas_call(
        functools.partial(_combine_kernel, rows=rows, tokens=t, emit_h=emit_h),
        grid_spec=pltpu.PrefetchScalarGridSpec(
            num_scalar_prefetch=1,
            grid=(t // rows,),
            in_specs=[pl.BlockSpec(memory_space=pl.ANY),
                      pl.BlockSpec((rows, LANES), lambda i, pos: (i, 0)),
                      row, per_batch, vec, vec, per_batch, per_batch],
            out_specs=out_specs,
            scratch_shapes=[pltpu.VMEM((2, TOP_K, rows, d), F32), pltpu.SemaphoreType.DMA((2,))],
        ),
        out_shape=out_shape,
        compiler_params=_params(("arbitrary",)),
        name="moe_combine_post",
    )(pos, expert_out, wts, x.reshape(t, d), gate, g_post.reshape(1, d), g_next.reshape(1, d),
      scale, shift)
    return [o.reshape(b, s, d) for o in outs]


def _route_plan(sel, comb, n_exp):
    t = sel.shape[0]
    tm = MOE_ROW_TILE
    n_tiles = (t * TOP_K) // tm + n_exp
    mask = sel > 0.5
    cnt = jnp.sum(mask, axis=0).astype(jnp.int32)
    tiles_e = (cnt + tm - 1) // tm
    tile_end = jnp.cumsum(tiles_e)
    tile_start = tile_end - tiles_e
    rank = jnp.cumsum(mask, axis=0).astype(jnp.int32) - 1
    pos_te = tile_start[None, :] * tm + rank
    order = jnp.argsort(jnp.logical_not(mask), axis=1, stable=True)[:, :TOP_K].astype(jnp.int32)
    pos = jnp.take_along_axis(pos_te, order, axis=1)
    wts = jnp.take_along_axis(comb, order, axis=1)
    token_of = jnp.zeros((n_tiles * tm,), jnp.int32).at[pos.reshape(-1)].set(
        jnp.repeat(jnp.arange(t, dtype=jnp.int32), TOP_K))
    wts_pad = jnp.zeros((t, LANES), F32).at[:, :TOP_K].set(wts)

    def schedule(col_tiles):
        steps = jnp.arange(n_tiles * col_tiles, dtype=jnp.int32)
        live = tile_end[-1] * col_tiles
        last = jnp.maximum(live - 1, 0)
        steps_c = jnp.minimum(steps, last)
        e_of = jnp.minimum(jnp.sum(steps_c[:, None] >= (tile_end * col_tiles)[None, :], axis=1),
                           n_exp - 1).astype(jnp.int32)
        local = steps_c - tile_start[e_of] * col_tiles
        n_e = jnp.maximum(tiles_e[e_of], 1)
        j_of = (local // n_e).astype(jnp.int32)
        t_of = (tile_start[e_of] + local % n_e).astype(jnp.int32)
        n_dead = jnp.maximum(n_tiles - tile_end[-1], 1)
        dead = jnp.maximum(steps - live, 0)
        is_live = steps < live
        ot_of = jnp.where(is_live, t_of, tile_end[-1] + dead % n_dead).astype(jnp.int32)
        oj_of = jnp.where(is_live, j_of, dead // n_dead).astype(jnp.int32)
        in_tile = jnp.clip(cnt[e_of] - (local % n_e) * tm, 0, tm)
        quant = MOE_ROW_QUANT
        rows_live = jnp.where(is_live, (in_tile + quant - 1) // quant * quant, 0).astype(jnp.int32)
        first = is_live & (local % n_e == 0)
        slot = jnp.where(is_live, (jnp.cumsum(first) - 1) % 2, 0).astype(jnp.int32)
        nxt = steps + n_e
        nxt_ok = first & (nxt < live)
        nxt_c = jnp.minimum(nxt, last)
        as_i32 = lambda v: v.astype(jnp.int32)
        return (t_of, ot_of, oj_of, rows_live, as_i32(first), slot, e_of, j_of, as_i32(nxt_ok),
                e_of[nxt_c], j_of[nxt_c])

    return token_of, tile_end[-1] * tm, schedule, pos.T.reshape(-1), wts_pad


def _col_tile(n, pref):
    while n % pref:
        pref //= 2
    return pref


def _moe(h_f32, comb, sel, w_gate, w_up, w_down, layer_idx):
    t, d = h_f32.shape
    n_exp = w_gate.shape[1]
    d_exp = w_gate.shape[3]
    tn_up, tn_down = _col_tile(d_exp, MOE_UP_COLS), _col_tile(d, MOE_DOWN_COLS)
    token_of, live_rows, schedule, pos, wts = _route_plan(sel[:, :n_exp], comb[:, :n_exp], n_exp)
    hs = _gather_rows(h_f32, token_of, live_rows)
    act = _grouped_matmul(hs, (w_gate, w_up), layer_idx, schedule(d_exp // tn_up), BF16, tn_up)
    out = _grouped_matmul(act, (w_down,), layer_idx, schedule(d // tn_down), F32, tn_down)
    return out, pos, wts


def kernel(x, c, rel_bias, w_ada, b_ada, g_pre_mix, w_in, g_grp_moba, g_grp_sb, g_grp_swa, swa_sinks, w_out, g_post_mix, g_pre_ffn, w_ff_gate, w_ff_up, w_ff_down, w_router, w_moe_gate, w_moe_up, w_moe_down, g_post_ffn):
    b, s, d = x.shape
    depth = w_ada.shape[0]
    m = b * s
    mod = _ada_mod(c, w_ada, b_ada).reshape(depth, b, ADA_CHUNKS, 1, d)
    chunk = lambda layer, idx: mod[layer, :, idx]
    h = _prenorm(x, g_pre_mix[0], chunk(0, 1), chunk(0, 0))
    mh, _, qh, _ = _head_counts(w_in, g_grp_moba, g_grp_sb, g_grp_swa)
    bias_tiles = _bias_tiles(rel_bias, mh, qh)
    for layer in range(depth):
        shift_f, scale_f, gate_f = chunk(layer, 3), chunk(layer, 4), chunk(layer, 5)
        y = _mixing(h, bias_tiles, w_in, w_out, g_grp_moba, g_grp_sb, g_grp_swa, swa_sinks, layer)
        idx = layer // 2
        dense = layer % 2 == 0
        nxt = ()
        if layer + 1 < depth:
            nxt = (g_pre_mix[layer + 1], chunk(layer + 1, 1), chunk(layer + 1, 0))
        if dense:
            x, h = _post(y, x, chunk(layer, 2), g_post_mix[layer], g_pre_ffn[layer], scale_f, shift_f)
            act = _swiglu_up(h.reshape(m, d), w_ff_gate, w_ff_up, idx)
            y = _matmul_ktiled(act, w_ff_down, idx)
            outs = _post(y, x, gate_f, g_post_ffn[layer], *nxt)
        else:
            x, hf, comb, sel = _post(y, x, chunk(layer, 2), g_post_mix[layer], g_pre_ffn[layer],
                                     scale_f, shift_f, w_router=w_router[idx])
            expert_out, pos, wts = _moe(hf.reshape(m, d), comb.reshape(m, LANES),
                                        sel.reshape(m, LANES), w_moe_gate, w_moe_up, w_moe_down, idx)
            outs = _combine_post(expert_out, pos, wts, x, gate_f, g_post_ffn[layer], *nxt)
        x = outs[0]
        if nxt:
            h = outs[1]
    return x
```

```python
import functools
import math

import jax
import jax.numpy as jnp
from jax import lax
from jax.experimental import pallas as pl
from jax.experimental.pallas import tpu as pltpu

HEAD_DIM = 128
MOBA_BLOCK = 256
MOBA_TOPK = 3
SB_BLOCK = 256
SWA_WINDOW = 128
REL_BUCKETS = 32
REL_MAX_EXACT = 16
REL_MAX_DISTANCE = 128
NORM_EPS = 1e-6
ADA_CHUNKS = 6
TOP_K = 2
LANES = 128
SUBLANES = 8

VMEM_LIMIT = 56 * 1024 * 1024
ROW_TILE = 256
ADA_COLS = 512
MM_ROWS = 1024
MM_COLS = 512
UP_COLS = 256
DOWN_TILE = (2048, 1024, 1024)
MOE_ROW_TILE = 512
MOE_ROW_QUANT = 128
MOE_UP_COLS = 512
MOE_DOWN_COLS = 1024
ATTN_HEADS = 8
SWA_Q_BLOCKS = 4

F32 = jnp.float32
BF16 = jnp.bfloat16
NEG_INF = float("-inf")
LOG2E = math.log2(math.e)


def _params(sem, vmem=VMEM_LIMIT):
    return pltpu.CompilerParams(dimension_semantics=sem, vmem_limit_bytes=vmem)


def _nt_dot(a, b):
    return lax.dot_general(a, b, (((1,), (1,)), ((), ())), preferred_element_type=F32)


def _dot(a, b):
    return jnp.dot(a, b, preferred_element_type=F32)


def _tn_dot(a, b):
    return lax.dot_general(a, b, (((0,), (0,)), ((), ())), preferred_element_type=F32)


def _split_bf16(x):
    hi = x.astype(BF16)
    lo = (x - hi.astype(F32)).astype(BF16)
    return hi, lo


def _rms(x):
    return x * lax.rsqrt(jnp.mean(x * x, axis=-1, keepdims=True) + NORM_EPS)


def _ada_kernel(c_ref, w_ref, b_ref, o_ref):
    c = c_ref[...]
    sc = c * jax.nn.sigmoid(c)
    hi, lo = _split_bf16(sc)
    w = w_ref[...].astype(BF16)
    o_ref[...] = _dot(hi, w) + _dot(lo, w) + b_ref[...]


def _ada_mod(c, w_ada, b_ada):
    depth, d, n = w_ada.shape
    b = c.shape[0]
    rows = SUBLANES
    c_pad = jnp.zeros((rows, d), F32).at[:b].set(c)
    tn = _col_tile(n, ADA_COLS)
    out = pl.pallas_call(
        _ada_kernel,
        grid=(depth, n // tn),
        in_specs=[
            pl.BlockSpec((rows, d), lambda l, j: (0, 0)),
            pl.BlockSpec((None, d, tn), lambda l, j: (l, 0, j)),
            pl.BlockSpec((None, 1, tn), lambda l, j: (l, 0, j)),
        ],
        out_specs=pl.BlockSpec((None, rows, tn), lambda l, j: (l, 0, j)),
        out_shape=jax.ShapeDtypeStruct((depth, rows, n), F32),
        compiler_params=_params(("parallel", "parallel")),
        name="ada_mod",
    )(c_pad, w_ada, b_ada.reshape(depth, 1, n))
    return out[:, :b]


def _prenorm_kernel(x_ref, g_ref, sc_ref, sh_ref, h_ref):
    x = x_ref[...]
    h = (_rms(x) * g_ref[...]) * (1.0 + sc_ref[...]) + sh_ref[...]
    h_ref[...] = h.astype(h_ref.dtype)


def _prenorm(x, g, scale, shift, ts=ROW_TILE):
    b, s, d = x.shape
    return pl.pallas_call(
        _prenorm_kernel,
        grid=(b, s // ts),
        in_specs=[
            pl.BlockSpec((None, ts, d), lambda i, j: (i, j, 0)),
            pl.BlockSpec((1, d), lambda i, j: (0, 0)),
            pl.BlockSpec((None, 1, d), lambda i, j: (i, 0, 0)),
            pl.BlockSpec((None, 1, d), lambda i, j: (i, 0, 0)),
        ],
        out_specs=pl.BlockSpec((None, ts, d), lambda i, j: (i, j, 0)),
        out_shape=jax.ShapeDtypeStruct((b, s, d), BF16),
        compiler_params=_params(("parallel", "parallel")),
        name="prenorm",
    )(x, g.reshape(1, d), scale, shift)


def _post_kernel(y_ref, x_ref, gate_ref, gpost_ref, gnext_ref, sc_ref, sh_ref, *rest,
                 emit_h, route):
    if route:
        wr_ref, rest = rest[0], rest[1:]
    xo_ref, rest = rest[0], rest[1:]
    x_new = x_ref[...] + gate_ref[...] * (_rms(y_ref[...].astype(F32)) * gpost_ref[...])
    xo_ref[...] = x_new
    if not emit_h:
        return
    h = (_rms(x_new) * gnext_ref[...]) * (1.0 + sc_ref[...]) + sh_ref[...]
    if not route:
        rest[0][...] = h.astype(BF16)
        return
    hf_ref, comb_ref, sel_ref = rest
    hf_ref[...] = h
    hi, lo = _split_bf16(h)
    w = wr_ref[...]
    whi, wlo = _split_bf16(w)
    logits = _dot(hi, whi) + _dot(lo, whi) + _dot(hi, wlo)
    n_exp = route
    lane = lax.broadcasted_iota(jnp.int32, logits.shape, 1)
    lg = jnp.where(lane < n_exp, logits, NEG_INF)
    m1 = jnp.max(lg, axis=1, keepdims=True)
    i1 = jnp.min(jnp.where(lg == m1, lane, LANES), axis=1, keepdims=True)
    lg2 = jnp.where(lane == i1, NEG_INF, lg)
    m2 = jnp.max(lg2, axis=1, keepdims=True)
    i2 = jnp.min(jnp.where(lg2 == m2, lane, LANES), axis=1, keepdims=True)
    e2 = jnp.exp(m2 - m1)
    w1 = 1.0 / (1.0 + e2)
    w2 = e2 / (1.0 + e2)
    comb_ref[...] = jnp.where(lane == i1, w1, 0.0) + jnp.where(lane == i2, w2, 0.0)
    sel_ref[...] = jnp.where((lane == i1) | (lane == i2), 1.0, 0.0)


def _post(y, x, gate, g_post, g_next=None, scale=None, shift=None, w_router=None, ts=ROW_TILE):
    b, s, d = x.shape
    emit_h = g_next is not None
    n_exp = 0 if w_router is None else w_router.shape[1]
    if not emit_h:
        g_next, scale, shift = g_post, gate, gate
    row = pl.BlockSpec((None, ts, d), lambda i, j: (i, j, 0))
    vec = pl.BlockSpec((1, d), lambda i, j: (0, 0))
    per_b = pl.BlockSpec((None, 1, d), lambda i, j: (i, 0, 0))
    in_specs = [row, row, per_b, vec, vec, per_b, per_b]
    args = [y.reshape(b, s, d), x, gate, g_post.reshape(1, d), g_next.reshape(1, d), scale, shift]
    out_specs = [row]
    out_shape = [jax.ShapeDtypeStruct((b, s, d), F32)]
    if n_exp:
        wr = jnp.zeros((d, LANES), F32).at[:, :n_exp].set(w_router.astype(F32))
        in_specs.append(pl.BlockSpec((d, LANES), lambda i, j: (0, 0)))
        args.append(wr)
        lane_blk = pl.BlockSpec((None, ts, LANES), lambda i, j: (i, j, 0))
        out_specs += [row, lane_blk, lane_blk]
        out_shape += [jax.ShapeDtypeStruct((b, s, d), F32),
                      jax.ShapeDtypeStruct((b, s, LANES), F32),
                      jax.ShapeDtypeStruct((b, s, LANES), F32)]
    elif emit_h:
        out_specs.append(row)
        out_shape.append(jax.ShapeDtypeStruct((b, s, d), BF16))
    return pl.pallas_call(
        functools.partial(_post_kernel, emit_h=emit_h, route=n_exp),
        grid=(b, s // ts),
        in_specs=in_specs,
        out_specs=out_specs,
        out_shape=out_shape,
        compiler_params=_params(("parallel", "parallel")),
        name="post_norm_residual",
    )(*args)


def _mm_kernel(a_ref, w_ref, o_ref):
    o_ref[...] = _dot(a_ref[...], w_ref[...].astype(BF16)).astype(o_ref.dtype)


def _matmul(a, w, layer, out_dtype, tm=MM_ROWS, tn=MM_COLS):
    m, k = a.shape
    n = w.shape[2]
    tm, tn = _col_tile(m, tm), _col_tile(n, tn)
    return pl.pallas_call(
        _mm_kernel,
        grid=(n // tn, m // tm),
        in_specs=[
            pl.BlockSpec((tm, k), lambda j, i: (i, 0)),
            pl.BlockSpec((None, k, tn), lambda j, i: (layer, 0, j)),
        ],
        out_specs=pl.BlockSpec((tm, tn), lambda j, i: (i, j)),
        out_shape=jax.ShapeDtypeStruct((m, n), out_dtype),
        compiler_params=_params(("parallel", "parallel")),
        name="matmul",
    )(a, w)


def _out_proj_kernel(oa_ref, ob_ref, oc_ref, w_ref, y_ref, *, cuts):
    c1, c2 = cuts
    w = w_ref[...].astype(BF16)
    y = _dot(oa_ref[...], w[:c1]) + _dot(ob_ref[...], w[c1:c2]) + _dot(oc_ref[...], w[c2:])
    y_ref[...] = y.astype(y_ref.dtype)


def _out_proj(oa, ob, oc, w, layer, tm=MM_ROWS, tn=MM_COLS):
    m = oa.shape[0]
    k, n = w.shape[1], w.shape[2]
    c1 = oa.shape[1]
    c2 = c1 + ob.shape[1]
    tm, tn = _col_tile(m, tm), _col_tile(n, tn)
    return pl.pallas_call(
        functools.partial(_out_proj_kernel, cuts=(c1, c2)),
        grid=(n // tn, m // tm),
        in_specs=[
            pl.BlockSpec((tm, oa.shape[1]), lambda j, i: (i, 0)),
            pl.BlockSpec((tm, ob.shape[1]), lambda j, i: (i, 0)),
            pl.BlockSpec((tm, oc.shape[1]), lambda j, i: (i, 0)),
            pl.BlockSpec((None, k, tn), lambda j, i: (layer, 0, j)),
        ],
        out_specs=pl.BlockSpec((tm, tn), lambda j, i: (i, j)),
        out_shape=jax.ShapeDtypeStruct((m, n), BF16),
        compiler_params=_params(("parallel", "parallel")),
        name="out_proj",
    )(oa, ob, oc, w)


def _swiglu_up_kernel(a_ref, wg_ref, wu_ref, o_ref):
    a = a_ref[...]
    g = _dot(a, wg_ref[...].astype(BF16))
    u = _dot(a, wu_ref[...].astype(BF16))
    o_ref[...] = (g * jax.nn.sigmoid(g) * u).astype(o_ref.dtype)


def _swiglu_up(a, w_gate, w_up, layer, tm=MM_ROWS, tn=UP_COLS):
    m, k = a.shape
    n = w_gate.shape[2]
    tm, tn = _col_tile(m, tm), _col_tile(n, tn)
    wspec = pl.BlockSpec((None, k, tn), lambda j, i: (layer, 0, j))
    return pl.pallas_call(
        _swiglu_up_kernel,
        grid=(n // tn, m // tm),
        in_specs=[pl.BlockSpec((tm, k), lambda j, i: (i, 0)), wspec, wspec],
        out_specs=pl.BlockSpec((tm, tn), lambda j, i: (i, j)),
        out_shape=jax.ShapeDtypeStruct((m, n), BF16),
        compiler_params=_params(("parallel", "parallel")),
        name="swiglu_up",
    )(a, w_gate, w_up)


def _mm_acc_kernel(a_ref, w_ref, o_ref, acc_ref):
    kk = pl.program_id(2)

    @pl.when(kk == 0)
    def _():
        acc_ref[...] = jnp.zeros_like(acc_ref)

    acc_ref[...] += _dot(a_ref[...], w_ref[...].astype(BF16))

    @pl.when(kk == pl.num_programs(2) - 1)
    def _():
        o_ref[...] = acc_ref[...].astype(o_ref.dtype)


def _matmul_ktiled(a, w, layer):
    tm, tn, tk = DOWN_TILE
    m, k = a.shape
    n = w.shape[2]
    tm, tn, tk = _col_tile(m, tm), _col_tile(n, tn), _col_tile(k, tk)
    return pl.pallas_call(
        _mm_acc_kernel,
        grid=(n // tn, m // tm, k // tk),
        in_specs=[
            pl.BlockSpec((tm, tk), lambda j, i, kk: (i, kk)),
            pl.BlockSpec((None, tk, tn), lambda j, i, kk: (layer, kk, j)),
        ],
        out_specs=pl.BlockSpec((tm, tn), lambda j, i, kk: (i, j)),
        out_shape=jax.ShapeDtypeStruct((m, n), BF16),
        scratch_shapes=[pltpu.VMEM((tm, tn), F32)],
        compiler_params=_params(("parallel", "parallel", "arbitrary")),
        name="matmul_ktiled",
    )(a, w)


def _rel_bucket(dist):
    n = jnp.maximum(dist, 0)
    nf = jnp.maximum(n, 1).astype(F32)
    large = REL_MAX_EXACT + (jnp.log(nf / REL_MAX_EXACT) / math.log(REL_MAX_DISTANCE / REL_MAX_EXACT)
                             * (REL_BUCKETS - REL_MAX_EXACT)).astype(jnp.int32)
    return jnp.where(n < REL_MAX_EXACT, n, jnp.minimum(large, REL_BUCKETS - 1))


def _head_cols(h):
    return slice(h * HEAD_DIM, (h + 1) * HEAD_DIM)


def _stack_heads(fn, heads):
    return jnp.concatenate([fn(h) for h in range(heads)], axis=0)


def _lane_heads(fn, heads):
    return jnp.concatenate([fn(h) for h in range(heads)], axis=1)


def _store_heads_transposed(o_ref, y_t, g_ref, heads, rows):
    for h in range(heads):
        y = y_t[:, h * rows:(h + 1) * rows].T * g_ref[:, _head_cols(h)]
        o_ref[:, _head_cols(h)] = y.astype(o_ref.dtype)


def _moba_kernel(q_ref, k_ref, v_ref, bias_ref, g_ref, o_ref, kmean_ref, m_ref, l_ref, acc_ref,
                 *, nb, hb, scale):
    i = pl.program_id(2)
    blk = MOBA_BLOCK

    @pl.when(i == 0)
    def _():
        kf = k_ref[...].astype(F32).reshape(nb, blk, hb * HEAD_DIM)
        kmean_ref[...] = jnp.mean(kf, axis=1)

    khi, klo = _split_bf16(kmean_ref[...])

    def gate_of(h):
        q = q_ref[:, _head_cols(h)]
        return _nt_dot(khi[:, _head_cols(h)], q) + _nt_dot(klo[:, _head_cols(h)], q)

    gate = _lane_heads(gate_of, hb)
    blk_id = lax.broadcasted_iota(jnp.int32, gate.shape, 0)
    gate = jnp.where(blk_id < i, gate, NEG_INF)
    sel = jnp.zeros(gate.shape, F32)
    for _ in range(min(MOBA_TOPK, nb)):
        m = jnp.max(gate, axis=0, keepdims=True)
        first = jnp.min(jnp.where((gate == m) & (m > NEG_INF), blk_id, nb), axis=0, keepdims=True)
        pick = blk_id == first
        sel = jnp.where(pick, 1.0, sel)
        gate = jnp.where(pick, NEG_INF, gate)

    def scores(rows, bias_idx):
        return _lane_heads(
            lambda h: _nt_dot(k_ref[rows, _head_cols(h)], q_ref[:, _head_cols(h)]) * scale
            + bias_ref[h, bias_idx], hb)

    def weighted_values(p, rows):
        pb = p.astype(BF16)
        return _lane_heads(
            lambda h: _tn_dot(v_ref[rows, _head_cols(h)], pb[:, h * blk:(h + 1) * blk]), hb)

    key = lax.broadcasted_iota(jnp.int32, (blk, hb * blk), 0)
    qry = lax.broadcasted_iota(jnp.int32, (blk, hb * blk), 1) & (blk - 1)
    own = pl.ds(pl.multiple_of(i * blk, blk), blk)
    s = jnp.where(key <= qry, scores(own, 0), NEG_INF)
    m0 = jnp.max(s, axis=0, keepdims=True)
    p = jnp.exp2(s - m0)
    m_ref[...] = m0
    l_ref[...] = jnp.sum(p, axis=0, keepdims=True)
    acc_ref[...] = weighted_values(p, own)

    def past_blocks(n0, count):
        rows = slice(n0 * blk, (n0 + count) * blk)

        def head_scores(h):
            raw = _nt_dot(k_ref[rows, _head_cols(h)], q_ref[:, _head_cols(h)]) * scale
            bias = [bias_ref[h, jnp.minimum(i - (n0 + t), 2)] for t in range(count)]
            return raw + jnp.concatenate(bias, axis=0)

        keep = jnp.concatenate(
            [jnp.broadcast_to(sel[n0 + t:n0 + t + 1, :], (blk, hb * blk)) for t in range(count)],
            axis=0)
        sn = jnp.where(keep > 0.0, _lane_heads(head_scores, hb), NEG_INF)
        m_old = m_ref[...]
        m_new = jnp.maximum(m_old, jnp.max(sn, axis=0, keepdims=True))
        alpha = jnp.exp2(m_old - m_new)
        pn = jnp.exp2(sn - m_new)
        m_ref[...] = m_new
        l_ref[...] = alpha * l_ref[...] + jnp.sum(pn, axis=0, keepdims=True)
        acc_ref[...] = alpha * acc_ref[...] + weighted_values(pn, rows)

    for n in range(0, nb - 1, 2):
        if n + 1 < nb - 1:
            pl.when(n + 1 < i)(functools.partial(past_blocks, n, 2))
        pl.when(n == i - 1)(functools.partial(past_blocks, n, 1))

    o_t = acc_ref[...] / l_ref[...]
    y_t = o_t * lax.rsqrt(jnp.mean(o_t * o_t, axis=0, keepdims=True) + NORM_EPS)
    _store_heads_transposed(o_ref, y_t, g_ref, hb, blk)


def _moba(proj, bias, g, heads, q0, k0, v0):
    b, s, _ = proj.shape
    blk = MOBA_BLOCK
    nb = s // blk
    hb = math.gcd(ATTN_HEADS, heads, q0, k0, v0)
    wide = hb * HEAD_DIM
    bias = jnp.swapaxes(bias, -1, -2) * LOG2E
    seq = lambda c0: pl.BlockSpec((None, s, wide), lambda bi, h, i: (bi, 0, c0 // hb + h))
    return pl.pallas_call(
        functools.partial(_moba_kernel, nb=nb, hb=hb, scale=HEAD_DIM ** -0.5 * LOG2E),
        grid=(b, heads // hb, nb),
        in_specs=[
            pl.BlockSpec((None, blk, wide), lambda bi, h, i: (bi, i, q0 // hb + h)),
            seq(k0), seq(v0),
            pl.BlockSpec((hb, 3, blk, blk), lambda bi, h, i: (h, 0, 0, 0)),
            pl.BlockSpec((1, wide), lambda bi, h, i: (0, h)),
        ],
        out_specs=pl.BlockSpec((None, blk, wide), lambda bi, h, i: (bi, i, h)),
        out_shape=jax.ShapeDtypeStruct((b, s, heads * HEAD_DIM), BF16),
        scratch_shapes=[pltpu.VMEM((nb, wide), F32), pltpu.VMEM((1, hb * blk), F32),
                        pltpu.VMEM((1, hb * blk), F32), pltpu.VMEM((HEAD_DIM, hb * blk), F32)],
        compiler_params=_params(("parallel", "parallel", "arbitrary")),
        name="moba_attention",
    )(proj, proj, proj, bias, g.reshape(1, -1))


def _sb_kernel(q_ref, k_ref, v_ref, g_ref, o_ref, acc_ref, *, hb, scale):
    i = pl.program_id(2)
    blk = SB_BLOCK
    key = lax.broadcasted_iota(jnp.int32, (blk, hb * blk), 0)
    qry = lax.broadcasted_iota(jnp.int32, (blk, hb * blk), 1) & (blk - 1)
    rs = lax.broadcasted_iota(jnp.int32, (blk, blk), 0)
    cs = lax.broadcasted_iota(jnp.int32, (blk, blk), 1)
    suffix = jnp.where(cs >= rs, 1.0, 0.0).astype(BF16)

    def block(rows, carry, diagonal):
        z = _lane_heads(
            lambda h: _nt_dot(k_ref[rows, _head_cols(h)], q_ref[:, _head_cols(h)]), hb) * scale
        zb = z.astype(BF16)
        sp = jnp.maximum(zb, 0.0) + jnp.log(1.0 + jnp.exp2(jnp.abs(zb) * -LOG2E))
        if diagonal:
            strict = key < qry
            sp = jnp.where(strict, sp, 0.0)
        tot = _dot(suffix, sp.astype(BF16)) + carry
        a = jnp.exp((z - tot).astype(BF16))
        if diagonal:
            a = jnp.where(strict, a, 0.0).astype(BF16)
        av = _lane_heads(
            lambda h: _tn_dot(v_ref[rows, _head_cols(h)], a[:, h * blk:(h + 1) * blk]), hb)
        if diagonal:
            acc_ref[...] = av
        else:
            acc_ref[...] += av
        return tot[0:1, :]

    own = pl.ds(pl.multiple_of(i * blk, blk), blk)
    carry = block(own, jnp.zeros((1, hb * blk), F32), True)

    def past(j):
        return pl.ds(pl.multiple_of(j * blk, blk), blk)

    group = 4
    rem = i % group
    carry = lax.fori_loop(0, rem, lambda t, c: block(past(i - 1 - t), c, False), carry)

    def several(step, c):
        j = i - rem - 1 - group * step
        for t in range(group):
            c = block(past(j - t), c, False)
        return c

    lax.fori_loop(0, i // group, several, carry)
    acc = acc_ref[...]
    y_t = acc * lax.rsqrt(jnp.mean(acc * acc, axis=0, keepdims=True) + NORM_EPS)
    _store_heads_transposed(o_ref, y_t, g_ref, hb, blk)


def _stick_breaking(proj, g, heads, q0, k0, v0):
    b, s, _ = proj.shape
    blk = SB_BLOCK
    hb = math.gcd(ATTN_HEADS, heads, q0, k0, v0)
    wide = hb * HEAD_DIM
    seq = lambda c0: pl.BlockSpec((None, s, wide), lambda bi, h, i: (bi, 0, c0 // hb + h))
    return pl.pallas_call(
        functools.partial(_sb_kernel, hb=hb, scale=HEAD_DIM ** -0.5),
        grid=(b, heads // hb, s // blk),
        in_specs=[
            pl.BlockSpec((None, blk, wide), lambda bi, h, i: (bi, i, q0 // hb + h)),
            seq(k0), seq(v0),
            pl.BlockSpec((1, wide), lambda bi, h, i: (0, h)),
        ],
        out_specs=pl.BlockSpec((None, blk, wide), lambda bi, h, i: (bi, i, h)),
        out_shape=jax.ShapeDtypeStruct((b, s, heads * HEAD_DIM), BF16),
        scratch_shapes=[pltpu.VMEM((HEAD_DIM, hb * blk), F32)],
        compiler_params=_params(("parallel", "parallel", "parallel")),
        name="stick_breaking_attention",
    )(proj, proj, proj, g.reshape(1, -1))


def _swa_kernel(q_ref, kp_ref, kc_ref, vp_ref, vc_ref, bias_ref, sink_ref, g_ref, o_ref,
                *, group, scale):
    w = SWA_WINDOW
    kv_heads = kc_ref.shape[1] // HEAD_DIM
    key = lax.broadcasted_iota(jnp.int32, (2 * w, group * w), 0)
    qry = lax.broadcasted_iota(jnp.int32, (2 * w, group * w), 1) & (w - 1)
    dist = qry + w - key
    in_window = (dist >= 0) & (dist < w)
    for sub in range(SWA_Q_BLOCKS):
        n = pl.program_id(1) * SWA_Q_BLOCKS + sub
        mask = in_window & (key + n * w >= w)
        rows = slice(sub * w, (sub + 1) * w)
        for kh in range(kv_heads):
            kcols = _head_cols(kh)
            if sub == 0:
                kk = jnp.concatenate([kp_ref[:, kcols], kc_ref[rows, kcols]], axis=0)
                vv = jnp.concatenate([vp_ref[:, kcols], vc_ref[rows, kcols]], axis=0)
            else:
                kk = kc_ref[(sub - 1) * w:(sub + 1) * w, kcols]
                vv = vc_ref[(sub - 1) * w:(sub + 1) * w, kcols]
            q = _stack_heads(lambda h: q_ref[rows, _head_cols(kh * group + h)], group)
            s = jnp.where(mask, _nt_dot(kk, q) * scale + bias_ref[kh], NEG_INF)
            sink = sink_ref[kh]
            m = jnp.maximum(jnp.max(s, axis=0, keepdims=True), sink)
            e = jnp.exp(s - m)
            denom = jnp.sum(e, axis=0, keepdims=True) + jnp.exp(sink - m)
            o_t = _tn_dot(vv, e.astype(BF16)) / denom
            y_t = o_t * lax.rsqrt(jnp.mean(o_t * o_t, axis=0, keepdims=True) + NORM_EPS)
            for h in range(group):
                cols = _head_cols(kh * group + h)
                o_ref[rows, cols] = (y_t[:, h * w:(h + 1) * w].T * g_ref[:, cols]).astype(BF16)


def _swa(proj, bias, sinks, g, q_heads, kv_heads, q0, k0, v0):
    b, s, _ = proj.shape
    w = SWA_WINDOW
    group = q_heads // kv_heads
    assert q0 % q_heads == 0 and k0 % kv_heads == 0 and v0 % kv_heads == 0
    bias = bias.reshape(kv_heads, group, w, 2 * w).transpose(0, 3, 1, 2).reshape(
        kv_heads, 2 * w, group * w)
    sink_rows = jnp.broadcast_to(sinks.astype(F32).reshape(kv_heads, 1, group, 1),
                                 (kv_heads, 1, group, w)).reshape(kv_heads, 1, group * w)
    nq = SWA_Q_BLOCKS
    own = lambda c0: pl.BlockSpec((None, nq * w, kv_heads * HEAD_DIM),
                                  lambda bi, n: (bi, n, c0 // kv_heads))
    prev = lambda c0: pl.BlockSpec((None, w, kv_heads * HEAD_DIM),
                                   lambda bi, n: (bi, jnp.maximum(n * nq - 1, 0), c0 // kv_heads))
    whole = lambda shape: pl.BlockSpec(shape, lambda bi, n: (0,) * len(shape))
    return pl.pallas_call(
        functools.partial(_swa_kernel, group=group, scale=HEAD_DIM ** -0.5),
        grid=(b, s // (nq * w)),
        in_specs=[
            pl.BlockSpec((None, nq * w, q_heads * HEAD_DIM), lambda bi, n: (bi, n, q0 // q_heads)),
            prev(k0), own(k0), prev(v0), own(v0),
            whole((kv_heads, 2 * w, group * w)), whole((kv_heads, 1, group * w)),
            whole((1, q_heads * HEAD_DIM)),
        ],
        out_specs=pl.BlockSpec((None, nq * w, q_heads * HEAD_DIM), lambda bi, n: (bi, n, 0)),
        out_shape=jax.ShapeDtypeStruct((b, s, q_heads * HEAD_DIM), BF16),
        compiler_params=_params(("parallel", "parallel")),
        name="swa_sink_attention",
    )(proj, proj, proj, proj, proj, bias, sink_rows, g.reshape(1, -1))


def _toeplitz(v, rows, cols):
    heads, length = v.shape
    t = jnp.tile(v, (1, rows))[:, :rows * (length - 1)].reshape(heads, rows, length - 1)
    return t[:, :, :cols]


def _bias_tiles(rel_bias, moba_heads, swa_heads):
    blk, w = MOBA_BLOCK, SWA_WINDOW
    length = 2 * blk
    k = jnp.arange(length)
    bd = rel_bias[_rel_bucket(k)].astype(F32).T
    mb, sw = bd[:moba_heads], bd[moba_heads:moba_heads + swa_heads]
    own = _toeplitz(mb[:, (-k) % length], blk, blk)
    adj = _toeplitz(mb[:, (blk - k) % length], blk, blk)
    far = jnp.broadcast_to(mb[:, length - 1][:, None, None], own.shape)
    swa = _toeplitz(sw[:, (w - k) % length], w, 2 * w)
    return jnp.stack([own, adj, far], axis=1), swa


def _head_counts(w_in, g_moba, g_sb, g_swa):
    moba_w, sb_w, swa_w = g_moba.shape[1], g_sb.shape[1], g_swa.shape[1]
    kv_w = (w_in.shape[2] - 3 * moba_w - 3 * sb_w - swa_w) // 2
    return moba_w // HEAD_DIM, sb_w // HEAD_DIM, swa_w // HEAD_DIM, kv_w // HEAD_DIM


def _mixing(h, bias_tiles, w_in, w_out, g_moba, g_sb, g_swa, sinks, layer):
    b, s, d = h.shape
    in_w = w_in.shape[2]
    mh, sh, qh, kvh = _head_counts(w_in, g_moba, g_sb, g_swa)
    assert qh % kvh == 0 and (3 * mh + 3 * sh) % (qh // kvh) == 0
    proj = _matmul(h.reshape(b * s, d), w_in, layer, BF16).reshape(b, s, in_w)
    o_a = _moba(proj, bias_tiles[0], g_moba[layer], mh, 0, mh, 2 * mh)
    o_b = _stick_breaking(proj, g_sb[layer], sh, 3 * mh, 3 * mh + sh, 3 * mh + 2 * sh)
    c0 = 3 * mh + 3 * sh
    o_c = _swa(proj, bias_tiles[1], sinks[layer], g_swa[layer], qh, kvh, c0, c0 + qh, c0 + qh + kvh)
    m = b * s
    return _out_proj(o_a.reshape(m, -1), o_b.reshape(m, -1), o_c.reshape(m, -1), w_out, layer)


def _row_gather_pipeline(n_live, fetch_rows, wait_rows, compute, idle=None):
    i = pl.program_id(0)

    @pl.when((i == 0) & (n_live > 0))
    def _():
        fetch_rows(0, 0)

    @pl.when(i + 1 < n_live)
    def _():
        fetch_rows(i + 1, (i + 1) % 2)

    @pl.when(i < n_live)
    def _():
        wait_rows(i % 2)
        compute(i % 2)

    if idle is not None:
        pl.when(i >= n_live)(idle)


def _gather_kernel(tok_ref, nlive_ref, src_ref, o_ref, buf_ref, sem_ref, *, rows):
    chunks = src_ref.shape[1]

    def copy(step, slot, group, sub):
        tok = tok_ref[step * rows + group * SUBLANES + sub]
        return pltpu.make_async_copy(src_ref.at[tok], buf_ref.at[slot, group, :, sub, :],
                                     sem_ref.at[slot])

    def fetch_rows(step, slot):
        def start(group, c):
            for sub in range(SUBLANES):
                copy(step, slot, group, sub).start()
            return c
        lax.fori_loop(0, rows // SUBLANES, start, 0)

    def wait_rows(slot):
        def wait(group, c):
            for sub in range(SUBLANES):
                copy(0, slot, group, sub).wait()
            return c
        lax.fori_loop(0, rows // SUBLANES, wait, 0)

    def compute(slot):
        for j in range(chunks):
            o_ref[:, j * LANES:(j + 1) * LANES] = (
                buf_ref[slot, :, j].reshape(rows, LANES).astype(o_ref.dtype))

    def idle():
        o_ref[...] = jnp.zeros_like(o_ref)

    _row_gather_pipeline(nlive_ref[0], fetch_rows, wait_rows, compute, idle)


def _gather_rows(src, token_of, live_rows, rows=ROW_TILE):
    p = token_of.shape[0]
    t, d = src.shape
    chunks = d // LANES
    n_live = (live_rows // rows).astype(jnp.int32).reshape(1)
    return pl.pallas_call(
        functools.partial(_gather_kernel, rows=rows),
        grid_spec=pltpu.PrefetchScalarGridSpec(
            num_scalar_prefetch=2,
            grid=(p // rows,),
            in_specs=[pl.BlockSpec(memory_space=pl.ANY)],
            out_specs=pl.BlockSpec((rows, d), lambda i, tok, nl: (i, 0)),
            scratch_shapes=[pltpu.VMEM((2, rows // SUBLANES, chunks, SUBLANES, LANES), F32),
                            pltpu.SemaphoreType.DMA((2,))],
        ),
        out_shape=jax.ShapeDtypeStruct((p, d), BF16),
        compiler_params=_params(("arbitrary",)),
        name="moe_gather",
    )(token_of, n_live, src.reshape(t, chunks, LANES))


def _gmm_kernel(t_ref, ot_ref, oj_ref, live_ref, first_ref, slot_ref, e_ref, j_ref, nok_ref,
                ne_ref, nj_ref, a_ref, *rest, n_w, layer, tn, quant):
    w_hbm, o_ref, wbuf_ref, sem_ref = rest[:n_w], rest[n_w], rest[n_w + 1], rest[n_w + 2]
    s = pl.program_id(0)
    slot = slot_ref[s]
    tm = o_ref.shape[0]

    def weight_copies(e, j, to_slot):
        cols = pl.ds(pl.multiple_of(j * tn, tn), tn)
        return [pltpu.make_async_copy(w.at[layer, e, :, cols], wbuf_ref.at[to_slot, i],
                                      sem_ref.at[to_slot]) for i, w in enumerate(w_hbm)]

    @pl.when(s == 0)
    def _():
        for c in weight_copies(e_ref[0], j_ref[0], 0):
            c.start()

    @pl.when((first_ref[s] > 0) & (nok_ref[s] > 0))
    def _():
        for c in weight_copies(ne_ref[s], nj_ref[s], 1 - slot):
            c.start()

    @pl.when(first_ref[s] > 0)
    def _():
        for c in weight_copies(e_ref[s], j_ref[s], slot):
            c.wait()

    @pl.when(live_ref[s] == 0)
    def _():
        o_ref[...] = jnp.zeros_like(o_ref)

    for rows in range(quant, tm + 1, quant):
        @pl.when(live_ref[s] == rows)
        def _(rows=rows):
            a = a_ref[:rows]
            if n_w == 2:
                g = _dot(a, wbuf_ref[slot, 0].astype(BF16))
                u = _dot(a, wbuf_ref[slot, 1].astype(BF16))
                o_ref[:rows] = (g * jax.nn.sigmoid(g) * u).astype(o_ref.dtype)
            else:
                o_ref[:rows] = _dot(a, wbuf_ref[slot, 0].astype(BF16)).astype(o_ref.dtype)
            if rows < tm:
                o_ref[rows:] = jnp.zeros((tm - rows, o_ref.shape[1]), o_ref.dtype)


def _grouped_matmul(a, weights, layer, sched, out_dtype, tn):
    p, k = a.shape
    n = weights[0].shape[3]
    tm = MOE_ROW_TILE
    n_w = len(weights)
    steps = sched[0].shape[0]
    n_sched = len(sched)
    return pl.pallas_call(
        functools.partial(_gmm_kernel, n_w=n_w, layer=layer, tn=tn, quant=MOE_ROW_QUANT),
        grid_spec=pltpu.PrefetchScalarGridSpec(
            num_scalar_prefetch=n_sched,
            grid=(steps,),
            in_specs=[pl.BlockSpec((tm, k), lambda s, t, *_: (t[s], 0))]
            + [pl.BlockSpec(memory_space=pl.ANY)] * n_w,
            out_specs=pl.BlockSpec((tm, tn), lambda s, t, ot, oj, *_: (ot[s], oj[s])),
            scratch_shapes=[pltpu.VMEM((2, n_w, k, tn), F32), pltpu.SemaphoreType.DMA((2,))],
        ),
        out_shape=jax.ShapeDtypeStruct((p, n), out_dtype),
        compiler_params=_params(("arbitrary",)),
        name="moe_grouped_matmul",
    )(*sched, a, *weights)


def _combine_kernel(pos_ref, src_ref, w_ref, x_ref, gate_ref, gpost_ref, gnext_ref, sc_ref, sh_ref,
                    xo_ref, *rest, rows, tokens, emit_h):
    buf_ref, sem_ref = rest[-2:]

    def copy(step, slot, r, choice):
        return pltpu.make_async_copy(
            src_ref.at[pl.ds(pos_ref[choice * tokens + step * rows + r], 1)],
            buf_ref.at[slot, choice, pl.ds(r, 1)], sem_ref.at[slot])

    def fetch_rows(step, slot):
        def start(r, c):
            for choice in range(TOP_K):
                copy(step, slot, r, choice).start(priority=choice % 2)
            return c
        lax.fori_loop(0, rows, start, 0, unroll=4)

    def wait_rows(slot):
        def wait(r, c):
            for choice in range(TOP_K):
                copy(0, slot, r, choice).wait()
            return c
        lax.fori_loop(0, rows, wait, 0, unroll=4)

    def compute(slot):
        wts = w_ref[...]
        y = wts[:, 0:1] * buf_ref[slot, 0] + wts[:, 1:2] * buf_ref[slot, 1]
        x_new = x_ref[...] + gate_ref[...] * (_rms(y) * gpost_ref[...])
        xo_ref[...] = x_new
        if emit_h:
            h = (_rms(x_new) * gnext_ref[...]) * (1.0 + sc_ref[...]) + sh_ref[...]
            rest[0][...] = h.astype(BF16)

    _row_gather_pipeline(pl.num_programs(0), fetch_rows, wait_rows, compute)


def _combine_post(expert_out, pos, wts, x, gate, g_post, g_next=None, scale=None, shift=None,
                  rows=ROW_TILE):
    b, s, d = x.shape
    t = b * s
    per_b = s // rows
    emit_h = g_next is not None
    if not emit_h:
        g_next, scale, shift = g_post, gate, gate
    row = pl.BlockSpec((rows, d), lambda i, pos: (i, 0))
    vec = pl.BlockSpec((1, d), lambda i, pos: (0, 0))
    per_batch = pl.BlockSpec((None, 1, d), lambda i, pos: (i // per_b, 0, 0))
    out_specs = [row] + ([row] if emit_h else [])
    out_shape = [jax.ShapeDtypeStruct((t, d), F32)] + (
        [jax.ShapeDtypeStruct((t, d), BF16)] if emit_h else [])
    outs = pl.pallas_call(
        functools.partial(_combine_kernel, rows=rows, tokens=t, emit_h=emit_h),
        grid_spec=pltpu.PrefetchScalarGridSpec(
            num_scalar_prefetch=1,
            grid=(t // rows,),
            in_specs=[pl.BlockSpec(memory_space=pl.ANY),
                      pl.BlockSpec((rows, LANES), lambda i, pos: (i, 0)),
                      row, per_batch, vec, vec, per_batch, per_batch],
            out_specs=out_specs,
            scratch_shapes=[pltpu.VMEM((2, TOP_K, rows, d), F32), pltpu.SemaphoreType.DMA((2,))],
        ),
        out_shape=out_shape,
        compiler_params=_params(("arbitrary",)),
        name="moe_combine_post",
    )(pos, expert_out, wts, x.reshape(t, d), gate, g_post.reshape(1, d), g_next.reshape(1, d),
      scale, shift)
    return [o.reshape(b, s, d) for o in outs]


def _route_plan(sel, comb, n_exp):
    t = sel.shape[0]
    tm = MOE_ROW_TILE
    n_tiles = (t * TOP_K) // tm + n_exp
    mask = sel > 0.5
    cnt = jnp.sum(mask, axis=0).astype(jnp.int32)
    tiles_e = (cnt + tm - 1) // tm
    tile_end = jnp.cumsum(tiles_e)
    tile_start = tile_end - tiles_e
    rank = jnp.cumsum(mask, axis=0).astype(jnp.int32) - 1
    pos_te = tile_start[None, :] * tm + rank
    order = jnp.argsort(jnp.logical_not(mask), axis=1, stable=True)[:, :TOP_K].astype(jnp.int32)
    pos = jnp.take_along_axis(pos_te, order, axis=1)
    wts = jnp.take_along_axis(comb, order, axis=1)
    token_of = jnp.zeros((n_tiles * tm,), jnp.int32).at[pos.reshape(-1)].set(
        jnp.repeat(jnp.arange(t, dtype=jnp.int32), TOP_K))
    wts_pad = jnp.zeros((t, LANES), F32).at[:, :TOP_K].set(wts)

    def schedule(col_tiles):
        steps = jnp.arange(n_tiles * col_tiles, dtype=jnp.int32)
        live = tile_end[-1] * col_tiles
        last = jnp.maximum(live - 1, 0)
        steps_c = jnp.minimum(steps, last)
        e_of = jnp.minimum(jnp.sum(steps_c[:, None] >= (tile_end * col_tiles)[None, :], axis=1),
                           n_exp - 1).astype(jnp.int32)
        local = steps_c - tile_start[e_of] * col_tiles
        n_e = jnp.maximum(tiles_e[e_of], 1)
        j_of = (local // n_e).astype(jnp.int32)
        t_of = (tile_start[e_of] + local % n_e).astype(jnp.int32)
        n_dead = jnp.maximum(n_tiles - tile_end[-1], 1)
        dead = jnp.maximum(steps - live, 0)
        is_live = steps < live
        ot_of = jnp.where(is_live, t_of, tile_end[-1] + dead % n_dead).astype(jnp.int32)
        oj_of = jnp.where(is_live, j_of, dead // n_dead).astype(jnp.int32)
        in_tile = jnp.clip(cnt[e_of] - (local % n_e) * tm, 0, tm)
        quant = MOE_ROW_QUANT
        rows_live = jnp.where(is_live, (in_tile + quant - 1) // quant * quant, 0).astype(jnp.int32)
        first = is_live & (local % n_e == 0)
        slot = jnp.where(is_live, (jnp.cumsum(first) - 1) % 2, 0).astype(jnp.int32)
        nxt = steps + n_e
        nxt_ok = first & (nxt < live)
        nxt_c = jnp.minimum(nxt, last)
        as_i32 = lambda v: v.astype(jnp.int32)
        return (t_of, ot_of, oj_of, rows_live, as_i32(first), slot, e_of, j_of, as_i32(nxt_ok),
                e_of[nxt_c], j_of[nxt_c])

    return token_of, tile_end[-1] * tm, schedule, pos.T.reshape(-1), wts_pad


def _col_tile(n, pref):
    while n % pref:
        pref //= 2
    return pref


def _moe(h_f32, comb, sel, w_gate, w_up, w_down, layer_idx):
    t, d = h_f32.shape
    n_exp = w_gate.shape[1]
    d_exp = w_gate.shape[3]
    tn_up, tn_down = _col_tile(d_exp, MOE_UP_COLS), _col_tile(d, MOE_DOWN_COLS)
    token_of, live_rows, schedule, pos, wts = _route_plan(sel[:, :n_exp], comb[:, :n_exp], n_exp)
    hs = _gather_rows(h_f32, token_of, live_rows)
    act = _grouped_matmul(hs, (w_gate, w_up), layer_idx, schedule(d_exp // tn_up), BF16, tn_up)
    out = _grouped_matmul(act, (w_down,), layer_idx, schedule(d // tn_down), F32, tn_down)
    return out, pos, wts


def kernel(x, c, rel_bias, w_ada, b_ada, g_pre_mix, w_in, g_grp_moba, g_grp_sb, g_grp_swa, swa_sinks, w_out, g_post_mix, g_pre_ffn, w_ff_gate, w_ff_up, w_ff_down, w_router, w_moe_gate, w_moe_up, w_moe_down, g_post_ffn):
    b, s, d = x.shape
    depth = w_ada.shape[0]
    m = b * s
    mod = _ada_mod(c, w_ada, b_ada).reshape(depth, b, ADA_CHUNKS, 1, d)
    chunk = lambda layer, idx: mod[layer, :, idx]
    h = _prenorm(x, g_pre_mix[0], chunk(0, 1), chunk(0, 0))
    mh, _, qh, _ = _head_counts(w_in, g_grp_moba, g_grp_sb, g_grp_swa)
    bias_tiles = _bias_tiles(rel_bias, mh, qh)
    for layer in range(depth):
        shift_f, scale_f, gate_f = chunk(layer, 3), chunk(layer, 4), chunk(layer, 5)
        y = _mixing(h, bias_tiles, w_in, w_out, g_grp_moba, g_grp_sb, g_grp_swa, swa_sinks, layer)
        idx = layer // 2
        dense = layer % 2 == 0
        nxt = ()
        if layer + 1 < depth:
            nxt = (g_pre_mix[layer + 1], chunk(layer + 1, 1), chunk(layer + 1, 0))
        if dense:
            x, h = _post(y, x, chunk(layer, 2), g_post_mix[layer], g_pre_ffn[layer], scale_f, shift_f)
            act = _swiglu_up(h.reshape(m, d), w_ff_gate, w_ff_up, idx)
            y = _matmul_ktiled(act, w_ff_down, idx)
            outs = _post(y, x, gate_f, g_post_ffn[layer], *nxt)
        else:
            x, hf, comb, sel = _post(y, x, chunk(layer, 2), g_post_mix[layer], g_pre_ffn[layer],
                                     scale_f, shift_f, w_router=w_router[idx])
            expert_out, pos, wts = _moe(hf.reshape(m, d), comb.reshape(m, LANES),
                                        sel.reshape(m, LANES), w_moe_gate, w_moe_up, w_moe_down, idx)
            outs = _combine_post(expert_out, pos, wts, x, gate_f, g_post_ffn[layer], *nxt)
        x = outs[0]
        if nxt:
            h = outs[1]
    return x
```

```python
import functools
import math

import jax
import jax.numpy as jnp
from jax import lax
from jax.experimental import pallas as pl
from jax.experimental.pallas import tpu as pltpu

HEAD_DIM = 128
MOBA_BLOCK = 256
MOBA_TOPK = 3
SB_BLOCK = 256
SWA_WINDOW = 128
REL_BUCKETS = 32
REL_MAX_EXACT = 16
REL_MAX_DISTANCE = 128
NORM_EPS = 1e-6
ADA_CHUNKS = 6
TOP_K = 2
LANES = 128
SUBLANES = 8

VMEM_LIMIT = 56 * 1024 * 1024
ROW_TILE = 256
ADA_COLS = 512
MM_ROWS = 1024
MM_COLS = 512
UP_COLS = 256
DOWN_TILE = (2048, 1024, 1024)
MOE_ROW_TILE = 512
MOE_ROW_QUANT = 128
MOE_UP_COLS = 512
MOE_DOWN_COLS = 1024
ATTN_HEADS = 8
SWA_Q_BLOCKS = 4

F32 = jnp.float32
BF16 = jnp.bfloat16
NEG_INF = float("-inf")
LOG2E = math.log2(math.e)


def _params(sem, vmem=VMEM_LIMIT):
    return pltpu.CompilerParams(dimension_semantics=sem, vmem_limit_bytes=vmem)


def _nt_dot(a, b):
    return lax.dot_general(a, b, (((1,), (1,)), ((), ())), preferred_element_type=F32)


def _dot(a, b):
    return jnp.dot(a, b, preferred_element_type=F32)


def _tn_dot(a, b):
    return lax.dot_general(a, b, (((0,), (0,)), ((), ())), preferred_element_type=F32)


def _split_bf16(x):
    hi = x.astype(BF16)
    lo = (x - hi.astype(F32)).astype(BF16)
    return hi, lo


def _rms(x):
    return x * lax.rsqrt(jnp.mean(x * x, axis=-1, keepdims=True) + NORM_EPS)


def _ada_kernel(c_ref, w_ref, b_ref, o_ref):
    c = c_ref[...]
    sc = c * jax.nn.sigmoid(c)
    hi, lo = _split_bf16(sc)
    w = w_ref[...].astype(BF16)
    o_ref[...] = _dot(hi, w) + _dot(lo, w) + b_ref[...]


def _ada_mod(c, w_ada, b_ada):
    depth, d, n = w_ada.shape
    b = c.shape[0]
    rows = SUBLANES
    c_pad = jnp.zeros((rows, d), F32).at[:b].set(c)
    tn = _col_tile(n, ADA_COLS)
    out = pl.pallas_call(
        _ada_kernel,
        grid=(depth, n // tn),
        in_specs=[
            pl.BlockSpec((rows, d), lambda l, j: (0, 0)),
            pl.BlockSpec((None, d, tn), lambda l, j: (l, 0, j)),
            pl.BlockSpec((None, 1, tn), lambda l, j: (l, 0, j)),
        ],
        out_specs=pl.BlockSpec((None, rows, tn), lambda l, j: (l, 0, j)),
        out_shape=jax.ShapeDtypeStruct((depth, rows, n), F32),
        compiler_params=_params(("parallel", "parallel")),
        name="ada_mod",
    )(c_pad, w_ada, b_ada.reshape(depth, 1, n))
    return out[:, :b]


def _prenorm_kernel(x_ref, g_ref, sc_ref, sh_ref, h_ref):
    x = x_ref[...]
    h = (_rms(x) * g_ref[...]) * (1.0 + sc_ref[...]) + sh_ref[...]
    h_ref[...] = h.astype(h_ref.dtype)


def _prenorm(x, g, scale, shift, ts=ROW_TILE):
    b, s, d = x.shape
    return pl.pallas_call(
        _prenorm_kernel,
        grid=(b, s // ts),
        in_specs=[
            pl.BlockSpec((None, ts, d), lambda i, j: (i, j, 0)),
            pl.BlockSpec((1, d), lambda i, j: (0, 0)),
            pl.BlockSpec((None, 1, d), lambda i, j: (i, 0, 0)),
            pl.BlockSpec((None, 1, d), lambda i, j: (i, 0, 0)),
        ],
        out_specs=pl.BlockSpec((None, ts, d), lambda i, j: (i, j, 0)),
        out_shape=jax.ShapeDtypeStruct((b, s, d), BF16),
        compiler_params=_params(("parallel", "parallel")),
        name="prenorm",
    )(x, g.reshape(1, d), scale, shift)


def _post_kernel(y_ref, x_ref, gate_ref, gpost_ref, gnext_ref, sc_ref, sh_ref, *rest,
                 emit_h, route):
    if route:
        wr_ref, rest = rest[0], rest[1:]
    xo_ref, rest = rest[0], rest[1:]
    x_new = x_ref[...] + gate_ref[...] * (_rms(y_ref[...].astype(F32)) * gpost_ref[...])
    xo_ref[...] = x_new
    if not emit_h:
        return
    h = (_rms(x_new) * gnext_ref[...]) * (1.0 + sc_ref[...]) + sh_ref[...]
    if not route:
        rest[0][...] = h.astype(BF16)
        return
    hf_ref, comb_ref, sel_ref = rest
    hf_ref[...] = h
    hi, lo = _split_bf16(h)
    w = wr_ref[...]
    whi, wlo = _split_bf16(w)
    logits = _dot(hi, whi) + _dot(lo, whi) + _dot(hi, wlo)
    n_exp = route
    lane = lax.broadcasted_iota(jnp.int32, logits.shape, 1)
    lg = jnp.where(lane < n_exp, logits, NEG_INF)
    m1 = jnp.max(lg, axis=1, keepdims=True)
    i1 = jnp.min(jnp.where(lg == m1, lane, LANES), axis=1, keepdims=True)
    lg2 = jnp.where(lane == i1, NEG_INF, lg)
    m2 = jnp.max(lg2, axis=1, keepdims=True)
    i2 = jnp.min(jnp.where(lg2 == m2, lane, LANES), axis=1, keepdims=True)
    e2 = jnp.exp(m2 - m1)
    w1 = 1.0 / (1.0 + e2)
    w2 = e2 / (1.0 + e2)
    comb_ref[...] = jnp.where(lane == i1, w1, 0.0) + jnp.where(lane == i2, w2, 0.0)
    sel_ref[...] = jnp.where((lane == i1) | (lane == i2), 1.0, 0.0)


def _post(y, x, gate, g_post, g_next=None, scale=None, shift=None, w_router=None, ts=ROW_TILE):
    b, s, d = x.shape
    emit_h = g_next is not None
    n_exp = 0 if w_router is None else w_router.shape[1]
    if not emit_h:
        g_next, scale, shift = g_post, gate, gate
    row = pl.BlockSpec((None, ts, d), lambda i, j: (i, j, 0))
    vec = pl.BlockSpec((1, d), lambda i, j: (0, 0))
    per_b = pl.BlockSpec((None, 1, d), lambda i, j: (i, 0, 0))
    in_specs = [row, row, per_b, vec, vec, per_b, per_b]
    args = [y.reshape(b, s, d), x, gate, g_post.reshape(1, d), g_next.reshape(1, d), scale, shift]
    out_specs = [row]
    out_shape = [jax.ShapeDtypeStruct((b, s, d), F32)]
    if n_exp:
        wr = jnp.zeros((d, LANES), F32).at[:, :n_exp].set(w_router.astype(F32))
        in_specs.append(pl.BlockSpec((d, LANES), lambda i, j: (0, 0)))
        args.append(wr)
        lane_blk = pl.BlockSpec((None, ts, LANES), lambda i, j: (i, j, 0))
        out_specs += [row, lane_blk, lane_blk]
        out_shape += [jax.ShapeDtypeStruct((b, s, d), F32),
                      jax.ShapeDtypeStruct((b, s, LANES), F32),
                      jax.ShapeDtypeStruct((b, s, LANES), F32)]
    elif emit_h:
        out_specs.append(row)
        out_shape.append(jax.ShapeDtypeStruct((b, s, d), BF16))
    return pl.pallas_call(
        functools.partial(_post_kernel, emit_h=emit_h, route=n_exp),
        grid=(b, s // ts),
        in_specs=in_specs,
        out_specs=out_specs,
        out_shape=out_shape,
        compiler_params=_params(("parallel", "parallel")),
        name="post_norm_residual",
    )(*args)


def _mm_kernel(a_ref, w_ref, o_ref):
    o_ref[...] = _dot(a_ref[...], w_ref[...].astype(BF16)).astype(o_ref.dtype)


def _matmul(a, w, layer, out_dtype, tm=MM_ROWS, tn=MM_COLS):
    m, k = a.shape
    n = w.shape[2]
    tm, tn = _col_tile(m, tm), _col_tile(n, tn)
    return pl.pallas_call(
        _mm_kernel,
        grid=(n // tn, m // tm),
        in_specs=[
            pl.BlockSpec((tm, k), lambda j, i: (i, 0)),
            pl.BlockSpec((None, k, tn), lambda j, i: (layer, 0, j)),
        ],
        out_specs=pl.BlockSpec((tm, tn), lambda j, i: (i, j)),
        out_shape=jax.ShapeDtypeStruct((m, n), out_dtype),
        compiler_params=_params(("parallel", "parallel")),
        name="matmul",
    )(a, w)


def _out_proj_kernel(oa_ref, ob_ref, oc_ref, w_ref, y_ref, *, cuts):
    c1, c2 = cuts
    w = w_ref[...].astype(BF16)
    y = _dot(oa_ref[...], w[:c1]) + _dot(ob_ref[...], w[c1:c2]) + _dot(oc_ref[...], w[c2:])
    y_ref[...] = y.astype(y_ref.dtype)


def _out_proj(oa, ob, oc, w, layer, tm=MM_ROWS, tn=MM_COLS):
    m = oa.shape[0]
    k, n = w.shape[1], w.shape[2]
    c1 = oa.shape[1]
    c2 = c1 + ob.shape[1]
    tm, tn = _col_tile(m, tm), _col_tile(n, tn)
    return pl.pallas_call(
        functools.partial(_out_proj_kernel, cuts=(c1, c2)),
        grid=(n // tn, m // tm),
        in_specs=[
            pl.BlockSpec((tm, oa.shape[1]), lambda j, i: (i, 0)),
            pl.BlockSpec((tm, ob.shape[1]), lambda j, i: (i, 0)),
            pl.BlockSpec((tm, oc.shape[1]), lambda j, i: (i, 0)),
            pl.BlockSpec((None, k, tn), lambda j, i: (layer, 0, j)),
        ],
        out_specs=pl.BlockSpec((tm, tn), lambda j, i: (i, j)),
        out_shape=jax.ShapeDtypeStruct((m, n), BF16),
        compiler_params=_params(("parallel", "parallel")),
        name="out_proj",
    )(oa, ob, oc, w)


def _swiglu_up_kernel(a_ref, wg_ref, wu_ref, o_ref):
    a = a_ref[...]
    g = _dot(a, wg_ref[...].astype(BF16))
    u = _dot(a, wu_ref[...].astype(BF16))
    o_ref[...] = (g * jax.nn.sigmoid(g) * u).astype(o_ref.dtype)


def _swiglu_up(a, w_gate, w_up, layer, tm=MM_ROWS, tn=UP_COLS):
    m, k = a.shape
    n = w_gate.shape[2]
    tm, tn = _col_tile(m, tm), _col_tile(n, tn)
    wspec = pl.BlockSpec((None, k, tn), lambda j, i: (layer, 0, j))
    return pl.pallas_call(
        _swiglu_up_kernel,
        grid=(n // tn, m // tm),
        in_specs=[pl.BlockSpec((tm, k), lambda j, i: (i, 0)), wspec, wspec],
        out_specs=pl.BlockSpec((tm, tn), lambda j, i: (i, j)),
        out_shape=jax.ShapeDtypeStruct((m, n), BF16),
        compiler_params=_params(("parallel", "parallel")),
        name="swiglu_up",
    )(a, w_gate, w_up)


def _mm_acc_kernel(a_ref, w_ref, o_ref, acc_ref):
    kk = pl.program_id(2)

    @pl.when(kk == 0)
    def _():
        acc_ref[...] = jnp.zeros_like(acc_ref)

    acc_ref[...] += _dot(a_ref[...], w_ref[...].astype(BF16))

    @pl.when(kk == pl.num_programs(2) - 1)
    def _():
        o_ref[...] = acc_ref[...].astype(o_ref.dtype)


def _matmul_ktiled(a, w, layer):
    tm, tn, tk = DOWN_TILE
    m, k = a.shape
    n = w.shape[2]
    tm, tn, tk = _col_tile(m, tm), _col_tile(n, tn), _col_tile(k, tk)
    return pl.pallas_call(
        _mm_acc_kernel,
        grid=(n // tn, m // tm, k // tk),
        in_specs=[
            pl.BlockSpec((tm, tk), lambda j, i, kk: (i, kk)),
            pl.BlockSpec((None, tk, tn), lambda j, i, kk: (layer, kk, j)),
        ],
        out_specs=pl.BlockSpec((tm, tn), lambda j, i, kk: (i, j)),
        out_shape=jax.ShapeDtypeStruct((m, n), BF16),
        scratch_shapes=[pltpu.VMEM((tm, tn), F32)],
        compiler_params=_params(("parallel", "parallel", "arbitrary")),
        name="matmul_ktiled",
    )(a, w)


def _rel_bucket(dist):
    n = jnp.maximum(dist, 0)
    nf = jnp.maximum(n, 1).astype(F32)
    large = REL_MAX_EXACT + (jnp.log(nf / REL_MAX_EXACT) / math.log(REL_MAX_DISTANCE / REL_MAX_EXACT)
                             * (REL_BUCKETS - REL_MAX_EXACT)).astype(jnp.int32)
    return jnp.where(n < REL_MAX_EXACT, n, jnp.minimum(large, REL_BUCKETS - 1))


def _head_cols(h):
    return slice(h * HEAD_DIM, (h + 1) * HEAD_DIM)


def _stack_heads(fn, heads):
    return jnp.concatenate([fn(h) for h in range(heads)], axis=0)


def _lane_heads(fn, heads):
    return jnp.concatenate([fn(h) for h in range(heads)], axis=1)


def _store_heads_transposed(o_ref, y_t, g_ref, heads, rows):
    for h in range(heads):
        y = y_t[:, h * rows:(h + 1) * rows].T * g_ref[:, _head_cols(h)]
        o_ref[:, _head_cols(h)] = y.astype(o_ref.dtype)


def _moba_kernel(q_ref, k_ref, v_ref, bias_ref, g_ref, o_ref, kmean_ref, sel_ref, m_ref, l_ref,
                 acc_ref, *, nb, hb, scale):
    i = pl.program_id(2)
    blk = MOBA_BLOCK

    @pl.when(i == 0)
    def _():
        kf = k_ref[...].astype(F32).reshape(nb, blk, hb * HEAD_DIM)
        kmean_ref[...] = jnp.mean(kf, axis=1)

    khi, klo = _split_bf16(kmean_ref[...])

    def gate_of(h):
        q = q_ref[:, _head_cols(h)]
        return _nt_dot(khi[:, _head_cols(h)], q) + _nt_dot(klo[:, _head_cols(h)], q)

    gate = _lane_heads(gate_of, hb)
    blk_id = lax.broadcasted_iota(jnp.int32, gate.shape, 0)
    gate = jnp.where(blk_id < i, gate, NEG_INF)
    sel = jnp.zeros(gate.shape, F32)
    for _ in range(min(MOBA_TOPK, nb)):
        m = jnp.max(gate, axis=0, keepdims=True)
        first = jnp.min(jnp.where((gate == m) & (m > NEG_INF), blk_id, nb), axis=0, keepdims=True)
        pick = blk_id == first
        sel = jnp.where(pick, 1.0, sel)
        gate = jnp.where(pick, NEG_INF, gate)

    def scores(rows, bias_idx):
        return _lane_heads(
            lambda h: _nt_dot(k_ref[rows, _head_cols(h)], q_ref[:, _head_cols(h)]) * scale
            + bias_ref[h, bias_idx], hb)

    def weighted_values(p, rows):
        pb = p.astype(BF16)
        return _lane_heads(
            lambda h: _tn_dot(v_ref[rows, _head_cols(h)], pb[:, h * blk:(h + 1) * blk]), hb)

    key = lax.broadcasted_iota(jnp.int32, (blk, hb * blk), 0)
    qry = lax.broadcasted_iota(jnp.int32, (blk, hb * blk), 1) & (blk - 1)
    own = pl.ds(pl.multiple_of(i * blk, blk), blk)
    s = jnp.where(key <= qry, scores(own, 0), NEG_INF)
    m0 = jnp.max(s, axis=0, keepdims=True)
    p = jnp.exp2(s - m0)
    m_ref[...] = m0
    l_ref[...] = jnp.sum(p, axis=0, keepdims=True)
    acc_ref[...] = weighted_values(p, own)

    sel_ref[...] = sel

    def past_blocks(n0, count):
        rows = pl.ds(pl.multiple_of(n0 * blk, blk), count * blk)

        def head_scores(h):
            raw = _nt_dot(k_ref[rows, _head_cols(h)], q_ref[:, _head_cols(h)]) * scale
            bias = [bias_ref[h, jnp.minimum(i - (n0 + t), 2)] for t in range(count)]
            return raw + jnp.concatenate(bias, axis=0)

        keep = jnp.concatenate(
            [jnp.broadcast_to(sel_ref[pl.ds(n0 + t, 1), :], (blk, hb * blk)) for t in range(count)],
            axis=0)
        sn = jnp.where(keep > 0.0, _lane_heads(head_scores, hb), NEG_INF)
        m_old = m_ref[...]
        m_new = jnp.maximum(m_old, jnp.max(sn, axis=0, keepdims=True))
        alpha = jnp.exp2(m_old - m_new)
        pn = jnp.exp2(sn - m_new)
        m_ref[...] = m_new
        l_ref[...] = alpha * l_ref[...] + jnp.sum(pn, axis=0, keepdims=True)
        acc_ref[...] = alpha * acc_ref[...] + weighted_values(pn, rows)

    def pair(p, c):
        past_blocks(2 * p, 2)
        return c

    lax.fori_loop(0, i >> 1, pair, 0)
    pl.when((i & 1) == 1)(lambda: past_blocks(i - 1, 1))

    o_t = acc_ref[...] / l_ref[...]
    y_t = o_t * lax.rsqrt(jnp.mean(o_t * o_t, axis=0, keepdims=True) + NORM_EPS)
    _store_heads_transposed(o_ref, y_t, g_ref, hb, blk)


def _moba(proj, bias, g, heads, q0, k0, v0):
    b, s, _ = proj.shape
    blk = MOBA_BLOCK
    nb = s // blk
    hb = math.gcd(ATTN_HEADS, heads, q0, k0, v0)
    wide = hb * HEAD_DIM
    bias = jnp.swapaxes(bias, -1, -2) * LOG2E
    seq = lambda c0: pl.BlockSpec((None, s, wide), lambda bi, h, i: (bi, 0, c0 // hb + h))
    return pl.pallas_call(
        functools.partial(_moba_kernel, nb=nb, hb=hb, scale=HEAD_DIM ** -0.5 * LOG2E),
        grid=(b, heads // hb, nb),
        in_specs=[
            pl.BlockSpec((None, blk, wide), lambda bi, h, i: (bi, i, q0 // hb + h)),
            seq(k0), seq(v0),
            pl.BlockSpec((hb, 3, blk, blk), lambda bi, h, i: (h, 0, 0, 0)),
            pl.BlockSpec((1, wide), lambda bi, h, i: (0, h)),
        ],
        out_specs=pl.BlockSpec((None, blk, wide), lambda bi, h, i: (bi, i, h)),
        out_shape=jax.ShapeDtypeStruct((b, s, heads * HEAD_DIM), BF16),
        scratch_shapes=[pltpu.VMEM((nb, wide), F32), pltpu.VMEM((nb, hb * blk), F32),
                        pltpu.VMEM((1, hb * blk), F32), pltpu.VMEM((1, hb * blk), F32),
                        pltpu.VMEM((HEAD_DIM, hb * blk), F32)],
        compiler_params=_params(("parallel", "parallel", "arbitrary")),
        name="moba_attention",
    )(proj, proj, proj, bias, g.reshape(1, -1))


def _sb_kernel(q_ref, k_ref, v_ref, g_ref, o_ref, acc_ref, *, hb, scale):
    i = pl.program_id(2)
    blk = SB_BLOCK
    key = lax.broadcasted_iota(jnp.int32, (blk, hb * blk), 0)
    qry = lax.broadcasted_iota(jnp.int32, (blk, hb * blk), 1) & (blk - 1)
    rs = lax.broadcasted_iota(jnp.int32, (blk, blk), 0)
    cs = lax.broadcasted_iota(jnp.int32, (blk, blk), 1)
    suffix = jnp.where(cs >= rs, 1.0, 0.0).astype(BF16)

    def block(rows, carry, diagonal):
        z = _lane_heads(
            lambda h: _nt_dot(k_ref[rows, _head_cols(h)], q_ref[:, _head_cols(h)]), hb) * scale
        zb = z.astype(BF16)
        sp = jnp.maximum(zb, 0.0) + jnp.log(1.0 + jnp.exp2(jnp.abs(zb) * -LOG2E))
        if diagonal:
            strict = key < qry
            sp = jnp.where(strict, sp, 0.0)
        tot = _dot(suffix, sp.astype(BF16)) + carry
        a = jnp.exp((z - tot).astype(BF16))
        if diagonal:
            a = jnp.where(strict, a, 0.0).astype(BF16)
        av = _lane_heads(
            lambda h: _tn_dot(v_ref[rows, _head_cols(h)], a[:, h * blk:(h + 1) * blk]), hb)
        if diagonal:
            acc_ref[...] = av
        else:
            acc_ref[...] += av
        return tot[0:1, :]

    own = pl.ds(pl.multiple_of(i * blk, blk), blk)
    carry = block(own, jnp.zeros((1, hb * blk), F32), True)

    def past(j):
        return pl.ds(pl.multiple_of(j * blk, blk), blk)

    group = 4
    rem = i % group
    carry = lax.fori_loop(0, rem, lambda t, c: block(past(i - 1 - t), c, False), carry)

    def several(step, c):
        j = i - rem - 1 - group * step
        for t in range(group):
            c = block(past(j - t), c, False)
        return c

    lax.fori_loop(0, i // group, several, carry)
    acc = acc_ref[...]
    y_t = acc * lax.rsqrt(jnp.mean(acc * acc, axis=0, keepdims=True) + NORM_EPS)
    _store_heads_transposed(o_ref, y_t, g_ref, hb, blk)


def _stick_breaking(proj, g, heads, q0, k0, v0):
    b, s, _ = proj.shape
    blk = SB_BLOCK
    hb = math.gcd(ATTN_HEADS, heads, q0, k0, v0)
    wide = hb * HEAD_DIM
    seq = lambda c0: pl.BlockSpec((None, s, wide), lambda bi, h, i: (bi, 0, c0 // hb + h))
    return pl.pallas_call(
        functools.partial(_sb_kernel, hb=hb, scale=HEAD_DIM ** -0.5),
        grid=(b, heads // hb, s // blk),
        in_specs=[
            pl.BlockSpec((None, blk, wide), lambda bi, h, i: (bi, i, q0 // hb + h)),
            seq(k0), seq(v0),
            pl.BlockSpec((1, wide), lambda bi, h, i: (0, h)),
        ],
        out_specs=pl.BlockSpec((None, blk, wide), lambda bi, h, i: (bi, i, h)),
        out_shape=jax.ShapeDtypeStruct((b, s, heads * HEAD_DIM), BF16),
        scratch_shapes=[pltpu.VMEM((HEAD_DIM, hb * blk), F32)],
        compiler_params=_params(("parallel", "parallel", "parallel")),
        name="stick_breaking_attention",
    )(proj, proj, proj, g.reshape(1, -1))


def _swa_kernel(q_ref, kp_ref, kc_ref, vp_ref, vc_ref, bias_ref, sink_ref, g_ref, o_ref,
                *, group, scale):
    w = SWA_WINDOW
    kv_heads = kc_ref.shape[1] // HEAD_DIM
    key = lax.broadcasted_iota(jnp.int32, (2 * w, group * w), 0)
    qry = lax.broadcasted_iota(jnp.int32, (2 * w, group * w), 1) & (w - 1)
    dist = qry + w - key
    in_window = (dist >= 0) & (dist < w)
    for sub in range(SWA_Q_BLOCKS):
        n = pl.program_id(1) * SWA_Q_BLOCKS + sub
        mask = in_window & (key + n * w >= w)
        rows = slice(sub * w, (sub + 1) * w)
        for kh in range(kv_heads):
            kcols = _head_cols(kh)
            if sub == 0:
                kk = jnp.concatenate([kp_ref[:, kcols], kc_ref[rows, kcols]], axis=0)
                vv = jnp.concatenate([vp_ref[:, kcols], vc_ref[rows, kcols]], axis=0)
            else:
                kk = kc_ref[(sub - 1) * w:(sub + 1) * w, kcols]
                vv = vc_ref[(sub - 1) * w:(sub + 1) * w, kcols]
            q = _stack_heads(lambda h: q_ref[rows, _head_cols(kh * group + h)], group)
            s = jnp.where(mask, _nt_dot(kk, q) * scale + bias_ref[kh], NEG_INF)
            sink = sink_ref[kh]
            m = jnp.maximum(jnp.max(s, axis=0, keepdims=True), sink)
            e = jnp.exp(s - m)
            denom = jnp.sum(e, axis=0, keepdims=True) + jnp.exp(sink - m)
            o_t = _tn_dot(vv, e.astype(BF16)) / denom
            y_t = o_t * lax.rsqrt(jnp.mean(o_t * o_t, axis=0, keepdims=True) + NORM_EPS)
            for h in range(group):
                cols = _head_cols(kh * group + h)
                o_ref[rows, cols] = (y_t[:, h * w:(h + 1) * w].T * g_ref[:, cols]).astype(BF16)


def _swa(proj, bias, sinks, g, q_heads, kv_heads, q0, k0, v0):
    b, s, _ = proj.shape
    w = SWA_WINDOW
    group = q_heads // kv_heads
    assert q0 % q_heads == 0 and k0 % kv_heads == 0 and v0 % kv_heads == 0
    bias = bias.reshape(kv_heads, group, w, 2 * w).transpose(0, 3, 1, 2).reshape(
        kv_heads, 2 * w, group * w)
    sink_rows = jnp.broadcast_to(sinks.astype(F32).reshape(kv_heads, 1, group, 1),
                                 (kv_heads, 1, group, w)).reshape(kv_heads, 1, group * w)
    nq = SWA_Q_BLOCKS
    own = lambda c0: pl.BlockSpec((None, nq * w, kv_heads * HEAD_DIM),
                                  lambda bi, n: (bi, n, c0 // kv_heads))
    prev = lambda c0: pl.BlockSpec((None, w, kv_heads * HEAD_DIM),
                                   lambda bi, n: (bi, jnp.maximum(n * nq - 1, 0), c0 // kv_heads))
    whole = lambda shape: pl.BlockSpec(shape, lambda bi, n: (0,) * len(shape))
    return pl.pallas_call(
        functools.partial(_swa_kernel, group=group, scale=HEAD_DIM ** -0.5),
        grid=(b, s // (nq * w)),
        in_specs=[
            pl.BlockSpec((None, nq * w, q_heads * HEAD_DIM), lambda bi, n: (bi, n, q0 // q_heads)),
            prev(k0), own(k0), prev(v0), own(v0),
            whole((kv_heads, 2 * w, group * w)), whole((kv_heads, 1, group * w)),
            whole((1, q_heads * HEAD_DIM)),
        ],
        out_specs=pl.BlockSpec((None, nq * w, q_heads * HEAD_DIM), lambda bi, n: (bi, n, 0)),
        out_shape=jax.ShapeDtypeStruct((b, s, q_heads * HEAD_DIM), BF16),
        compiler_params=_params(("parallel", "parallel")),
        name="swa_sink_attention",
    )(proj, proj, proj, proj, proj, bias, sink_rows, g.reshape(1, -1))


def _toeplitz(v, rows, cols):
    heads, length = v.shape
    t = jnp.tile(v, (1, rows))[:, :rows * (length - 1)].reshape(heads, rows, length - 1)
    return t[:, :, :cols]


def _bias_tiles(rel_bias, moba_heads, swa_heads):
    blk, w = MOBA_BLOCK, SWA_WINDOW
    length = 2 * blk
    k = jnp.arange(length)
    bd = rel_bias[_rel_bucket(k)].astype(F32).T
    mb, sw = bd[:moba_heads], bd[moba_heads:moba_heads + swa_heads]
    own = _toeplitz(mb[:, (-k) % length], blk, blk)
    adj = _toeplitz(mb[:, (blk - k) % length], blk, blk)
    far = jnp.broadcast_to(mb[:, length - 1][:, None, None], own.shape)
    swa = _toeplitz(sw[:, (w - k) % length], w, 2 * w)
    return jnp.stack([own, adj, far], axis=1), swa


def _head_counts(w_in, g_moba, g_sb, g_swa):
    moba_w, sb_w, swa_w = g_moba.shape[1], g_sb.shape[1], g_swa.shape[1]
    kv_w = (w_in.shape[2] - 3 * moba_w - 3 * sb_w - swa_w) // 2
    return moba_w // HEAD_DIM, sb_w // HEAD_DIM, swa_w // HEAD_DIM, kv_w // HEAD_DIM


def _mixing(h, bias_tiles, w_in, w_out, g_moba, g_sb, g_swa, sinks, layer):
    b, s, d = h.shape
    in_w = w_in.shape[2]
    mh, sh, qh, kvh = _head_counts(w_in, g_moba, g_sb, g_swa)
    assert qh % kvh == 0 and (3 * mh + 3 * sh) % (qh // kvh) == 0
    proj = _matmul(h.reshape(b * s, d), w_in, layer, BF16).reshape(b, s, in_w)
    o_a = _moba(proj, bias_tiles[0], g_moba[layer], mh, 0, mh, 2 * mh)
    o_b = _stick_breaking(proj, g_sb[layer], sh, 3 * mh, 3 * mh + sh, 3 * mh + 2 * sh)
    c0 = 3 * mh + 3 * sh
    o_c = _swa(proj, bias_tiles[1], sinks[layer], g_swa[layer], qh, kvh, c0, c0 + qh, c0 + qh + kvh)
    m = b * s
    return _out_proj(o_a.reshape(m, -1), o_b.reshape(m, -1), o_c.reshape(m, -1), w_out, layer)


def _row_gather_pipeline(n_live, fetch_rows, wait_rows, compute, idle=None):
    i = pl.program_id(0)

    @pl.when((i == 0) & (n_live > 0))
    def _():
        fetch_rows(0, 0)

    @pl.when(i + 1 < n_live)
    def _():
        fetch_rows(i + 1, (i + 1) % 2)

    @pl.when(i < n_live)
    def _():
        wait_rows(i % 2)
        compute(i % 2)

    if idle is not None:
        pl.when(i >= n_live)(idle)


def _gather_kernel(tok_ref, nlive_ref, src_ref, o_ref, buf_ref, sem_ref, *, rows):
    chunks = src_ref.shape[1]

    def copy(step, slot, group, sub):
        tok = tok_ref[step * rows + group * SUBLANES + sub]
        return pltpu.make_async_copy(src_ref.at[tok], buf_ref.at[slot, group, :, sub, :],
                                     sem_ref.at[slot])

    def fetch_rows(step, slot):
        def start(group, c):
            for sub in range(SUBLANES):
                copy(step, slot, group, sub).start()
            return c
        lax.fori_loop(0, rows // SUBLANES, start, 0)

    def wait_rows(slot):
        def wait(group, c):
            for sub in range(SUBLANES):
                copy(0, slot, group, sub).wait()
            return c
        lax.fori_loop(0, rows // SUBLANES, wait, 0)

    def compute(slot):
        for j in range(chunks):
            o_ref[:, j * LANES:(j + 1) * LANES] = (
                buf_ref[slot, :, j].reshape(rows, LANES).astype(o_ref.dtype))

    def idle():
        o_ref[...] = jnp.zeros_like(o_ref)

    _row_gather_pipeline(nlive_ref[0], fetch_rows, wait_rows, compute, idle)


def _gather_rows(src, token_of, live_rows, rows=ROW_TILE):
    p = token_of.shape[0]
    t, d = src.shape
    chunks = d // LANES
    n_live = (live_rows // rows).astype(jnp.int32).reshape(1)
    return pl.pallas_call(
        functools.partial(_gather_kernel, rows=rows),
        grid_spec=pltpu.PrefetchScalarGridSpec(
            num_scalar_prefetch=2,
            grid=(p // rows,),
            in_specs=[pl.BlockSpec(memory_space=pl.ANY)],
            out_specs=pl.BlockSpec((rows, d), lambda i, tok, nl: (i, 0)),
            scratch_shapes=[pltpu.VMEM((2, rows // SUBLANES, chunks, SUBLANES, LANES), F32),
                            pltpu.SemaphoreType.DMA((2,))],
        ),
        out_shape=jax.ShapeDtypeStruct((p, d), BF16),
        compiler_params=_params(("arbitrary",)),
        name="moe_gather",
    )(token_of, n_live, src.reshape(t, chunks, LANES))


def _gmm_kernel(t_ref, ot_ref, oj_ref, live_ref, first_ref, slot_ref, e_ref, j_ref, nok_ref,
                ne_ref, nj_ref, a_ref, *rest, n_w, layer, tn, quant):
    w_hbm, o_ref, wbuf_ref, sem_ref = rest[:n_w], rest[n_w], rest[n_w + 1], rest[n_w + 2]
    s = pl.program_id(0)
    slot = slot_ref[s]
    tm = o_ref.shape[0]

    def weight_copies(e, j, to_slot):
        cols = pl.ds(pl.multiple_of(j * tn, tn), tn)
        return [pltpu.make_async_copy(w.at[layer, e, :, cols], wbuf_ref.at[to_slot, i],
                                      sem_ref.at[to_slot]) for i, w in enumerate(w_hbm)]

    @pl.when(s == 0)
    def _():
        for c in weight_copies(e_ref[0], j_ref[0], 0):
            c.start()

    @pl.when((first_ref[s] > 0) & (nok_ref[s] > 0))
    def _():
        for c in weight_copies(ne_ref[s], nj_ref[s], 1 - slot):
            c.start()

    @pl.when(first_ref[s] > 0)
    def _():
        for c in weight_copies(e_ref[s], j_ref[s], slot):
            c.wait()

    @pl.when(live_ref[s] == 0)
    def _():
        o_ref[...] = jnp.zeros_like(o_ref)

    for rows in range(quant, tm + 1, quant):
        @pl.when(live_ref[s] == rows)
        def _(rows=rows):
            a = a_ref[:rows]
            if n_w == 2:
                g = _dot(a, wbuf_ref[slot, 0].astype(BF16))
                u = _dot(a, wbuf_ref[slot, 1].astype(BF16))
                o_ref[:rows] = (g * jax.nn.sigmoid(g) * u).astype(o_ref.dtype)
            else:
                o_ref[:rows] = _dot(a, wbuf_ref[slot, 0].astype(BF16)).astype(o_ref.dtype)
            if rows < tm:
                o_ref[rows:] = jnp.zeros((tm - rows, o_ref.shape[1]), o_ref.dtype)


def _grouped_matmul(a, weights, layer, sched, out_dtype, tn):
    p, k = a.shape
    n = weights[0].shape[3]
    tm = MOE_ROW_TILE
    n_w = len(weights)
    steps = sched[0].shape[0]
    n_sched = len(sched)
    return pl.pallas_call(
        functools.partial(_gmm_kernel, n_w=n_w, layer=layer, tn=tn, quant=MOE_ROW_QUANT),
        grid_spec=pltpu.PrefetchScalarGridSpec(
            num_scalar_prefetch=n_sched,
            grid=(steps,),
            in_specs=[pl.BlockSpec((tm, k), lambda s, t, *_: (t[s], 0))]
            + [pl.BlockSpec(memory_space=pl.ANY)] * n_w,
            out_specs=pl.BlockSpec((tm, tn), lambda s, t, ot, oj, *_: (ot[s], oj[s])),
            scratch_shapes=[pltpu.VMEM((2, n_w, k, tn), F32), pltpu.SemaphoreType.DMA((2,))],
        ),
        out_shape=jax.ShapeDtypeStruct((p, n), out_dtype),
        compiler_params=_params(("arbitrary",)),
        name="moe_grouped_matmul",
    )(*sched, a, *weights)


def _combine_kernel(pos_ref, src_ref, w_ref, x_ref, gate_ref, gpost_ref, gnext_ref, sc_ref, sh_ref,
                    xo_ref, *rest, rows, tokens, emit_h):
    buf_ref, sem_ref = rest[-2:]

    def copy(step, slot, r, choice):
        return pltpu.make_async_copy(
            src_ref.at[pl.ds(pos_ref[choice * tokens + step * rows + r], 1)],
            buf_ref.at[slot, choice, pl.ds(r, 1)], sem_ref.at[slot])

    def fetch_rows(step, slot):
        def start(r, c):
            for choice in range(TOP_K):
                copy(step, slot, r, choice).start(priority=choice % 2)
            return c
        lax.fori_loop(0, rows, start, 0, unroll=4)

    def wait_rows(slot):
        def wait(r, c):
            for choice in range(TOP_K):
                copy(0, slot, r, choice).wait()
            return c
        lax.fori_loop(0, rows, wait, 0, unroll=4)

    def compute(slot):
        wts = w_ref[...]
        y = wts[:, 0:1] * buf_ref[slot, 0] + wts[:, 1:2] * buf_ref[slot, 1]
        x_new = x_ref[...] + gate_ref[...] * (_rms(y) * gpost_ref[...])
        xo_ref[...] = x_new
        if emit_h:
            h = (_rms(x_new) * gnext_ref[...]) * (1.0 + sc_ref[...]) + sh_ref[...]
            rest[0][...] = h.astype(BF16)

    _row_gather_pipeline(pl.num_programs(0), fetch_rows, wait_rows, compute)


def _combine_post(expert_out, pos, wts, x, gate, g_post, g_next=None, scale=None, shift=None,
                  rows=ROW_TILE):
    b, s, d = x.shape
    t = b * s
    per_b = s // rows
    emit_h = g_next is not None
    if not emit_h:
        g_next, scale, shift = g_post, gate, gate
    row = pl.BlockSpec((rows, d), lambda i, pos: (i, 0))
    vec = pl.BlockSpec((1, d), lambda i, pos: (0, 0))
    per_batch = pl.BlockSpec((None, 1, d), lambda i, pos: (i // per_b, 0, 0))
    out_specs = [row] + ([row] if emit_h else [])
    out_shape = [jax.ShapeDtypeStruct((t, d), F32)] + (
        [jax.ShapeDtypeStruct((t, d), BF16)] if emit_h else [])
    outs = pl.pallas_call(
        functools.partial(_combine_kernel, rows=rows, tokens=t, emit_h=emit_h),
        grid_spec=pltpu.PrefetchScalarGridSpec(
            num_scalar_prefetch=1,
            grid=(t // rows,),
            in_specs=[pl.BlockSpec(memory_space=pl.ANY),
                      pl.BlockSpec((rows, LANES), lambda i, pos: (i, 0)),
                      row, per_batch, vec, vec, per_batch, per_batch],
            out_specs=out_specs,
            scratch_shapes=[pltpu.VMEM((2, TOP_K, rows, d), F32), pltpu.SemaphoreType.DMA((2,))],
        ),
        out_shape=out_shape,
        compiler_params=_params(("arbitrary",)),
        name="moe_combine_post",
    )(pos, expert_out, wts, x.reshape(t, d), gate, g_post.reshape(1, d), g_next.reshape(1, d),
      scale, shift)
    return [o.reshape(b, s, d) for o in outs]


def _route_plan(sel, comb, n_exp):
    t = sel.shape[0]
    tm = MOE_ROW_TILE
    n_tiles = (t * TOP_K) // tm + n_exp
    mask = sel > 0.5
    cnt = jnp.sum(mask, axis=0).astype(jnp.int32)
    tiles_e = (cnt + tm - 1) // tm
    tile_end = jnp.cumsum(tiles_e)
    tile_start = tile_end - tiles_e
    rank = jnp.cumsum(mask, axis=0).astype(jnp.int32) - 1
    pos_te = tile_start[None, :] * tm + rank
    order = jnp.argsort(jnp.logical_not(mask), axis=1, stable=True)[:, :TOP_K].astype(jnp.int32)
    pos = jnp.take_along_axis(pos_te, order, axis=1)
    wts = jnp.take_along_axis(comb, order, axis=1)
    token_of = jnp.zeros((n_tiles * tm,), jnp.int32).at[pos.reshape(-1)].set(
        jnp.repeat(jnp.arange(t, dtype=jnp.int32), TOP_K))
    wts_pad = jnp.zeros((t, LANES), F32).at[:, :TOP_K].set(wts)

    def schedule(col_tiles):
        steps = jnp.arange(n_tiles * col_tiles, dtype=jnp.int32)
        live = tile_end[-1] * col_tiles
        last = jnp.maximum(live - 1, 0)
        steps_c = jnp.minimum(steps, last)
        e_of = jnp.minimum(jnp.sum(steps_c[:, None] >= (tile_end * col_tiles)[None, :], axis=1),
                           n_exp - 1).astype(jnp.int32)
        local = steps_c - tile_start[e_of] * col_tiles
        n_e = jnp.maximum(tiles_e[e_of], 1)
        j_of = (local // n_e).astype(jnp.int32)
        t_of = (tile_start[e_of] + local % n_e).astype(jnp.int32)
        n_dead = jnp.maximum(n_tiles - tile_end[-1], 1)
        dead = jnp.maximum(steps - live, 0)
        is_live = steps < live
        ot_of = jnp.where(is_live, t_of, tile_end[-1] + dead % n_dead).astype(jnp.int32)
        oj_of = jnp.where(is_live, j_of, dead // n_dead).astype(jnp.int32)
        in_tile = jnp.clip(cnt[e_of] - (local % n_e) * tm, 0, tm)
        quant = MOE_ROW_QUANT
        rows_live = jnp.where(is_live, (in_tile + quant - 1) // quant * quant, 0).astype(jnp.int32)
        first = is_live & (local % n_e == 0)
        slot = jnp.where(is_live, (jnp.cumsum(first) - 1) % 2, 0).astype(jnp.int32)
        nxt = steps + n_e
        nxt_ok = first & (nxt < live)
        nxt_c = jnp.minimum(nxt, last)
        as_i32 = lambda v: v.astype(jnp.int32)
        return (t_of, ot_of, oj_of, rows_live, as_i32(first), slot, e_of, j_of, as_i32(nxt_ok),
                e_of[nxt_c], j_of[nxt_c])

    return token_of, tile_end[-1] * tm, schedule, pos.T.reshape(-1), wts_pad


def _col_tile(n, pref):
    while n % pref:
        pref //= 2
    return pref


def _moe(h_f32, comb, sel, w_gate, w_up, w_down, layer_idx):
    t, d = h_f32.shape
    n_exp = w_gate.shape[1]
    d_exp = w_gate.shape[3]
    tn_up, tn_down = _col_tile(d_exp, MOE_UP_COLS), _col_tile(d, MOE_DOWN_COLS)
    token_of, live_rows, schedule, pos, wts = _route_plan(sel[:, :n_exp], comb[:, :n_exp], n_exp)
    hs = _gather_rows(h_f32, token_of, live_rows)
    act = _grouped_matmul(hs, (w_gate, w_up), layer_idx, schedule(d_exp // tn_up), BF16, tn_up)
    out = _grouped_matmul(act, (w_down,), layer_idx, schedule(d // tn_down), F32, tn_down)
    return out, pos, wts


def kernel(x, c, rel_bias, w_ada, b_ada, g_pre_mix, w_in, g_grp_moba, g_grp_sb, g_grp_swa, swa_sinks, w_out, g_post_mix, g_pre_ffn, w_ff_gate, w_ff_up, w_ff_down, w_router, w_moe_gate, w_moe_up, w_moe_down, g_post_ffn):
    b, s, d = x.shape
    depth = w_ada.shape[0]
    m = b * s
    mod = _ada_mod(c, w_ada, b_ada).reshape(depth, b, ADA_CHUNKS, 1, d)
    chunk = lambda layer, idx: mod[layer, :, idx]
    h = _prenorm(x, g_pre_mix[0], chunk(0, 1), chunk(0, 0))
    mh, _, qh, _ = _head_counts(w_in, g_grp_moba, g_grp_sb, g_grp_swa)
    bias_tiles = _bias_tiles(rel_bias, mh, qh)
    for layer in range(depth):
        shift_f, scale_f, gate_f = chunk(layer, 3), chunk(layer, 4), chunk(layer, 5)
        y = _mixing(h, bias_tiles, w_in, w_out, g_grp_moba, g_grp_sb, g_grp_swa, swa_sinks, layer)
        idx = layer // 2
        dense = layer % 2 == 0
        nxt = ()
        if layer + 1 < depth:
            nxt = (g_pre_mix[layer + 1], chunk(layer + 1, 1), chunk(layer + 1, 0))
        if dense:
            x, h = _post(y, x, chunk(layer, 2), g_post_mix[layer], g_pre_ffn[layer], scale_f, shift_f)
            act = _swiglu_up(h.reshape(m, d), w_ff_gate, w_ff_up, idx)
            y = _matmul_ktiled(act, w_ff_down, idx)
            outs = _post(y, x, gate_f, g_post_ffn[layer], *nxt)
        else:
            x, hf, comb, sel = _post(y, x, chunk(layer, 2), g_post_mix[layer], g_pre_ffn[layer],
                                     scale_f, shift_f, w_router=w_router[idx])
            expert_out, pos, wts = _moe(hf.reshape(m, d), comb.reshape(m, LANES),
                                        sel.reshape(m, LANES), w_moe_gate, w_moe_up, w_moe_down, idx)
            outs = _combine_post(expert_out, pos, wts, x, gate_f, g_post_ffn[layer], *nxt)
        x = outs[0]
        if nxt:
            h = outs[1]
    return x
```
